```python
import jax, jax.numpy as jnp
from jax import lax
import numpy as np

D_MODEL = 1024
BATCH = 8
SEQ = 2048
DEPTH = 2
DEC_BATCH = 32
DEC_SEQ = 1
PAST_LEN = 16384
PAGE_SIZE = 128

D_INNER = D_MODEL
SSM_HEAD_DIM = 64
SSM_HEADS = D_INNER // SSM_HEAD_DIM
SSM_GROUPS = 2
SSM_HPG = SSM_HEADS // SSM_GROUPS
D_STATE = 128
CONV_WIDTH = 4
CONV_DIM = D_INNER + 2 * SSM_GROUPS * D_STATE
SSD_CHUNK = 128
MLA_HEADS = 8
QK_NOPE = 64
QK_ROPE = 32
V_HEAD = 64
Q_LORA = 384
KV_LORA = 256
ROPE_THETA = 10000.0
Q_BLOCK = 128
ATTN_SCALE = (QK_NOPE + QK_ROPE) ** -0.5
PEER_HEADS = 8
N_KEYS = 128
N_EXPERTS = N_KEYS * N_KEYS
PEER_D_KEY = 256
PEER_TOPK = 16
PEER_BLOCK = 128
EPS = 1e-6

_SIZES = (D_MODEL, D_MODEL, D_INNER, CONV_DIM, SSM_HEADS, Q_LORA, KV_LORA, QK_ROPE)
IN_DIM = sum(_SIZES)
SPLIT_POINTS = tuple(int(s) for s in np.cumsum(_SIZES)[:-1])

kernel_name = "hybrid_ssd_mla_peer_step"


def rmsnorm(x, w):
    xf = x.astype(jnp.float32)
    xf = xf * lax.rsqrt(jnp.mean(xf * xf, axis=-1, keepdims=True) + EPS)
    return (xf * w.astype(jnp.float32)).astype(x.dtype)


def rope(x, pos):
    half = x.shape[-1] // 2
    inv = ROPE_THETA ** (-jnp.arange(half, dtype=jnp.float32) / half)
    ang = pos.astype(jnp.float32)[:, None] * inv[None, :]
    shape = (1, ang.shape[0]) + (1,) * (x.ndim - 3) + (half,)
    cos = jnp.cos(ang).reshape(shape)
    sin = jnp.sin(ang).reshape(shape)
    xf = x.astype(jnp.float32)
    x1, x2 = xf[..., :half], xf[..., half:]
    return jnp.concatenate([x1 * cos - x2 * sin, x1 * sin + x2 * cos], axis=-1).astype(x.dtype)


def causal_conv(xbc, prev, w, b):
    L = xbc.shape[1]
    xpad = jnp.concatenate([prev.astype(xbc.dtype), xbc], axis=1)
    out = b
    for k in range(CONV_WIDTH):
        out = out + xpad[:, k:k + L] * w[k]
    return jax.nn.silu(out), xpad[:, L:]


def ssd_scan(xh, dt, A, Bm, Cm, h0):
    b, L = xh.shape[:2]
    cl = SSD_CHUNK if L % SSD_CHUNK == 0 else L
    nc = L // cl
    f32 = jnp.float32
    x_c = xh.astype(f32).reshape((b, nc, cl) + xh.shape[2:])
    dt_c = dt.astype(f32).reshape((b, nc, cl) + dt.shape[2:])
    B_c = Bm.astype(f32).reshape((b, nc, cl) + Bm.shape[2:])
    C_c = Cm.astype(f32).reshape((b, nc, cl) + Cm.shape[2:])
    a_cs = jnp.cumsum(dt_c * A.astype(f32), axis=2)
    seg = a_cs[:, :, :, None] - a_cs[:, :, None, :]
    causal = (jnp.arange(cl)[:, None] >= jnp.arange(cl)[None, :])[None, None, :, :, None, None]
    Lmat = jnp.exp(jnp.where(causal, seg, -jnp.inf))
    xdt = x_c * dt_c[..., None]
    CB = jnp.einsum('bclgn,bcsgn->bclsg', C_c, B_c)
    y_diag = jnp.einsum('bclsg,bclsge,bcsgep->bclgep', CB, Lmat, xdt)
    decay_to_end = jnp.exp(a_cs[:, :, -1:] - a_cs)
    chunk_states = jnp.einsum('bclgn,bclge,bclgep->bcgepn', B_c, decay_to_end, xdt)
    chunk_decay = jnp.exp(a_cs[:, :, -1])

    def step(h, inp):
        dec, st = inp
        return dec[..., None, None] * h + st, h

    h_final, h_prev = lax.scan(step, h0.astype(f32),
                               (jnp.moveaxis(chunk_decay, 1, 0), jnp.moveaxis(chunk_states, 1, 0)))
    h_prev = jnp.moveaxis(h_prev, 0, 1)
    y_off = jnp.einsum('bclgn,bcgepn,bclge->bclgep', C_c, h_prev, jnp.exp(a_cs))
    y = (y_diag + y_off).reshape(xh.shape)
    return y.astype(xh.dtype), h_final.astype(h0.dtype)


def ssd_branch(z, xbc_raw, dt_raw, h0, conv_prev, conv_w, conv_b, dt_bias, a_log, d_skip, ssm_norm, w_ssm_proj):
    b, L = z.shape[:2]
    xbc, conv_new = causal_conv(xbc_raw, conv_prev, conv_w, conv_b)
    xs = xbc[..., :D_INNER]
    Bm = xbc[..., D_INNER:D_INNER + SSM_GROUPS * D_STATE].reshape(b, L, SSM_GROUPS, D_STATE)
    Cm = xbc[..., D_INNER + SSM_GROUPS * D_STATE:].reshape(b, L, SSM_GROUPS, D_STATE)
    xh = xs.reshape(b, L, SSM_GROUPS, SSM_HPG, SSM_HEAD_DIM)
    dt = jax.nn.softplus(dt_raw + dt_bias).reshape(b, L, SSM_GROUPS, SSM_HPG)
    A = -jnp.exp(a_log).reshape(SSM_GROUPS, SSM_HPG)
    y, h_new = ssd_scan(xh, dt, A, Bm, Cm, h0)
    y = y + d_skip.reshape(SSM_GROUPS, SSM_HPG)[..., None] * xh
    y = y.reshape(b, L, D_INNER) * jax.nn.silu(z)
    y = rmsnorm(y.reshape(b, L, SSM_GROUPS, D_INNER // SSM_GROUPS),
                ssm_norm.reshape(SSM_GROUPS, D_INNER // SSM_GROUPS)).reshape(b, L, D_INNER)
    return y @ w_ssm_proj, h_new, conv_new


def mla_attention(q_lat, q_pe, kv, kr, q_pos, k_pos):
    b, L = q_lat.shape[:2]
    blk = Q_BLOCK if L % Q_BLOCK == 0 else L
    nb = L // blk

    def to_blocks(a):
        return jnp.moveaxis(a.reshape((b, nb, blk) + a.shape[2:]), 1, 0)

    def one(args):
        ql, qp, qpos = args
        s = (jnp.einsum('bqhr,btr->bhqt', ql, kv) + jnp.einsum('bqhd,btd->bhqt', qp, kr)).astype(jnp.float32) * ATTN_SCALE
        mask = k_pos[None, :] <= qpos[:, None]
        p = jax.nn.softmax(jnp.where(mask, s, -jnp.inf), axis=-1).astype(kv.dtype)
        return jnp.einsum('bhqt,btr->bqhr', p, kv)

    o = lax.map(one, (to_blocks(q_lat), to_blocks(q_pe), q_pos.reshape(nb, blk)))
    return jnp.moveaxis(o, 0, 1).reshape(b, L, MLA_HEADS, KV_LORA)


def mla_branch(c_q, c_kv, kr_raw, pos, past_lat, past_kr, q_norm, w_qb, kv_norm, w_kvb, w_mla_proj):
    b, L = c_q.shape[:2]
    q = (rmsnorm(c_q, q_norm) @ w_qb).reshape(b, L, MLA_HEADS, QK_NOPE + QK_ROPE)
    q_nope = q[..., :QK_NOPE]
    q_pe = rope(q[..., QK_NOPE:], pos)
    lat = rmsnorm(c_kv, kv_norm)
    kpe = rope(kr_raw[:, :, None, :], pos)[:, :, 0]
    wkv = w_kvb.reshape(KV_LORA, MLA_HEADS, QK_NOPE + V_HEAD)
    w_kb, w_vb = wkv[..., :QK_NOPE], wkv[..., QK_NOPE:]
    q_lat = jnp.einsum('blhd,rhd->blhr', q_nope, w_kb)
    if past_lat is None:
        kv_all, kr_all, k_pos = lat, kpe, pos
    else:
        kv_all = jnp.concatenate([past_lat.astype(lat.dtype), lat], axis=1)
        kr_all = jnp.concatenate([past_kr.astype(kpe.dtype), kpe], axis=1)
        k_pos = jnp.arange(past_lat.shape[1] + L, dtype=jnp.int32)
    o_lat = mla_attention(q_lat, q_pe, kv_all, kr_all, pos, k_pos)
    o = jnp.einsum('blhr,rhv->blhv', o_lat, w_vb).reshape(b, L, MLA_HEADS * V_HEAD)
    return o @ w_mla_proj, lat, kpe


def peer(h, wq, keys, u, v):
    b, L, D = h.shape
    n = b * L
    blk = PEER_BLOCK if n % PEER_BLOCK == 0 else n
    half = PEER_D_KEY // 2

    def one(tb):
        q = (tb @ wq).reshape(blk, PEER_HEADS, PEER_D_KEY)
        s1 = jnp.einsum('thd,hkd->thk', q[..., :half], keys[:, 0])
        s2 = jnp.einsum('thd,hkd->thk', q[..., half:], keys[:, 1])
        sc1, i1 = lax.top_k(s1, PEER_TOPK)
        sc2, i2 = lax.top_k(s2, PEER_TOPK)
        cand = (sc1[..., :, None] + sc2[..., None, :]).reshape(blk, PEER_HEADS, PEER_TOPK * PEER_TOPK)
        cidx = (i1[..., :, None] * N_KEYS + i2[..., None, :]).reshape(blk, PEER_HEADS, PEER_TOPK * PEER_TOPK)
        sc, sel = lax.top_k(cand, PEER_TOPK)
        idx = jnp.take_along_axis(cidx, sel, axis=-1)
        g = jax.nn.softmax(sc.astype(jnp.float32), axis=-1)
        act = jax.nn.gelu(jnp.einsum('thkd,td->thk', u[idx], tb), approximate=False)
        wgt = (g * act.astype(jnp.float32)).astype(tb.dtype)
        return jnp.einsum('thk,thkd->td', wgt, v[idx])

    out = lax.map(one, h.reshape(n // blk, blk, D))
    return out.reshape(b, L, D)


def trunk_layer(x, pos, h0, conv_prev, past_lat, past_kr, p):
    (norm_mix, w_in, b_gate, conv_w, conv_b, dt_bias, a_log, d_skip, ssm_norm, w_ssm_proj,
     q_norm, w_qb, kv_norm, w_kvb, w_mla_proj, w_out, norm_ffn, peer_wq, peer_keys, peer_u, peer_v) = p
    h = rmsnorm(x, norm_mix)
    proj = h @ w_in
    gate_a, gate_b, z, xbc, dt_raw, c_q, c_kv, kr_raw = jnp.split(proj, SPLIT_POINTS, axis=-1)
    ya, h_new, conv_new = ssd_branch(z, xbc, dt_raw, h0, conv_prev, conv_w, conv_b, dt_bias, a_log,
                                     d_skip, ssm_norm, w_ssm_proj)
    yb, lat_new, kr_new = mla_branch(c_q, c_kv, kr_raw, pos, past_lat, past_kr, q_norm, w_qb, kv_norm,
                                     w_kvb, w_mla_proj)
    ga = jax.nn.sigmoid(gate_a + b_gate[:D_MODEL])
    gb = jax.nn.sigmoid(gate_b + b_gate[D_MODEL:])
    x = x + (ga * ya + gb * yb) @ w_out
    x = x + peer(rmsnorm(x, norm_ffn), peer_wq, peer_keys, peer_u, peer_v)
    return x, h_new, conv_new, lat_new, kr_new


def setup_inputs(seed: int = 0) -> dict:
    key = jax.random.key(seed)
    ks = jax.random.split(key, 40)
    f32 = jnp.float32
    n_pages = PAST_LEN // PAGE_SIZE
    n_used = DEC_BATCH * n_pages
    n_pool = n_used + max(1, n_used // 4)
    nrm = lambda k, s, sc: jax.random.normal(k, s, f32) * sc
    gain = lambda k, s: 1.0 + 0.02 * jax.random.normal(k, s, f32)
    dt0 = jnp.exp(jax.random.uniform(ks[10], (DEPTH, SSM_HEADS), f32, np.log(1e-3), np.log(1e-1)))
    return {
        "x_prompt": nrm(ks[0], (BATCH, SEQ, D_MODEL), 1.0),
        "x_sample": nrm(ks[1], (DEC_BATCH, DEC_SEQ, D_MODEL), 1.0),
        "cache_latent": nrm(ks[2], (DEPTH, n_pool, PAGE_SIZE, KV_LORA), 1.0),
        "cache_krope": nrm(ks[3], (DEPTH, n_pool, PAGE_SIZE, QK_ROPE), 1.0),
        "state_ssm": nrm(ks[4], (DEPTH, DEC_BATCH, SSM_GROUPS, SSM_HPG, SSM_HEAD_DIM, D_STATE), 0.1),
        "state_conv": nrm(ks[5], (DEPTH, DEC_BATCH, CONV_WIDTH - 1, CONV_DIM), 1.0),
        "page_table": jax.random.permutation(ks[6], n_pool)[:n_used].reshape(DEC_BATCH, n_pages).astype(jnp.int32),
        "norm_mix": gain(ks[7], (DEPTH, D_MODEL)),
        "w_in": nrm(ks[8], (DEPTH, D_MODEL, IN_DIM), D_MODEL ** -0.5),
        "b_gate": nrm(ks[9], (DEPTH, 2 * D_MODEL), 0.02),
        "conv_w": nrm(ks[11], (DEPTH, CONV_WIDTH, CONV_DIM), CONV_WIDTH ** -0.5),
        "conv_b": nrm(ks[12], (DEPTH, CONV_DIM), 0.02),
        "dt_bias": dt0 + jnp.log(-jnp.expm1(-dt0)),
        "a_log": jnp.log(jax.random.uniform(ks[13], (DEPTH, SSM_HEADS), f32, 1.0, 16.0)),
        "d_skip": gain(ks[14], (DEPTH, SSM_HEADS)),
        "ssm_norm": gain(ks[15], (DEPTH, D_INNER)),
        "w_ssm_proj": nrm(ks[16], (DEPTH, D_INNER, D_MODEL), D_INNER ** -0.5),
        "q_norm": gain(ks[17], (DEPTH, Q_LORA)),
        "w_qb": nrm(ks[18], (DEPTH, Q_LORA, MLA_HEADS * (QK_NOPE + QK_ROPE)), Q_LORA ** -0.5),
        "kv_norm": gain(ks[19], (DEPTH, KV_LORA)),
        "w_kvb": nrm(ks[20], (DEPTH, KV_LORA, MLA_HEADS * (QK_NOPE + V_HEAD)), KV_LORA ** -0.5),
        "w_mla_proj": nrm(ks[21], (DEPTH, MLA_HEADS * V_HEAD, D_MODEL), (MLA_HEADS * V_HEAD) ** -0.5),
        "w_out": nrm(ks[22], (DEPTH, D_MODEL, D_MODEL), D_MODEL ** -0.5),
        "norm_ffn": gain(ks[23], (DEPTH, D_MODEL)),
        "peer_wq": nrm(ks[24], (DEPTH, D_MODEL, PEER_HEADS * PEER_D_KEY), D_MODEL ** -0.5),
        "peer_keys": nrm(ks[25], (DEPTH, PEER_HEADS, 2, N_KEYS, PEER_D_KEY // 2), (PEER_D_KEY // 2) ** -0.5),
        "peer_u": nrm(ks[26], (DEPTH, N_EXPERTS, D_MODEL), D_MODEL ** -0.5),
        "peer_v": nrm(ks[27], (DEPTH, N_EXPERTS, D_MODEL), 0.1),
        "norm_final": gain(ks[28], (D_MODEL,)),
    }


def reference(x_prompt, x_sample, cache_latent, cache_krope, state_ssm, state_conv, page_table,
              norm_mix, w_in, b_gate, conv_w, conv_b, dt_bias, a_log, d_skip, ssm_norm, w_ssm_proj,
              q_norm, w_qb, kv_norm, w_kvb, w_mla_proj, w_out, norm_ffn, peer_wq, peer_keys,
              peer_u, peer_v, norm_final):
    bp, sp = x_prompt.shape[:2]
    bs, ss = x_sample.shape[:2]
    past_len = page_table.shape[1] * cache_latent.shape[2]
    pos_p = jnp.arange(sp, dtype=jnp.int32)
    pos_s = past_len + jnp.arange(ss, dtype=jnp.int32)
    h0_p = jnp.zeros((bp, SSM_GROUPS, SSM_HPG, SSM_HEAD_DIM, D_STATE), x_prompt.dtype)
    conv0_p = jnp.zeros((bp, CONV_WIDTH - 1, CONV_DIM), x_prompt.dtype)
    xp, xs = x_prompt, x_sample
    lat_p, kr_p, ssm_p, conv_p = [], [], [], []
    lat_s, kr_s, ssm_s, conv_s = [], [], [], []
    for l in range(DEPTH):
        p = (norm_mix[l], w_in[l], b_gate[l], conv_w[l], conv_b[l], dt_bias[l], a_log[l], d_skip[l],
             ssm_norm[l], w_ssm_proj[l], q_norm[l], w_qb[l], kv_norm[l], w_kvb[l], w_mla_proj[l],
             w_out[l], norm_ffn[l], peer_wq[l], peer_keys[l], peer_u[l], peer_v[l])
        past_lat = cache_latent[l][page_table].reshape(bs, past_len, KV_LORA)
        past_kr = cache_krope[l][page_table].reshape(bs, past_len, QK_ROPE)
        xp, hp, cp, lp, kp = trunk_layer(xp, pos_p, h0_p, conv0_p, None, None, p)
        xs, hs, cs, lsn, ksn = trunk_layer(xs, pos_s, state_ssm[l], state_conv[l], past_lat, past_kr, p)
        lat_p.append(lp); kr_p.append(kp); ssm_p.append(hp); conv_p.append(cp)
        lat_s.append(lsn); kr_s.append(ksn); ssm_s.append(hs); conv_s.append(cs)
    y_prompt = rmsnorm(xp, norm_final)
    y_sample = rmsnorm(xs, norm_final)
    return (y_prompt, y_sample,
            jnp.stack(lat_p), jnp.stack(kr_p), jnp.stack(ssm_p), jnp.stack(conv_p),
            jnp.stack(lat_s), jnp.stack(kr_s), jnp.stack(ssm_s), jnp.stack(conv_s))
```

```python
import functools

import numpy as np
import jax
import jax.numpy as jnp
from jax import lax
from jax.experimental import pallas as pl
from jax.experimental.pallas import tpu as pltpu

F32 = jnp.float32
BF16 = jnp.bfloat16
I32 = jnp.int32

D_MODEL = 1024
D_INNER = 1024
SSM_HEAD_DIM = 64
SSM_HEADS = 16
SSM_GROUPS = 2
SSM_HPG = 8
D_STATE = 128
CONV_WIDTH = 4
CONV_DIM = D_INNER + 2 * SSM_GROUPS * D_STATE
SSD_CHUNK = 128
MLA_HEADS = 8
QK_NOPE = 64
QK_ROPE = 32
V_HEAD = 64
Q_LORA = 384
KV_LORA = 256
ROPE_THETA = 10000.0
ATTN_SCALE = (QK_NOPE + QK_ROPE) ** -0.5
PEER_HEADS = 8
N_KEYS = 128
N_EXPERTS = N_KEYS * N_KEYS
PEER_D_KEY = 256
PEER_TOPK = 16
EPS = 1e-6

LANES = 128
QK_CAT = KV_LORA + LANES
DT_LANE = 64
VMEM_LIMIT = 56 * 1024 * 1024

_HI = lax.Precision.HIGHEST


def _cparams(*sem):
    return pltpu.CompilerParams(dimension_semantics=sem, vmem_limit_bytes=VMEM_LIMIT)


def _dot(a, b):
    return jnp.dot(a, b, preferred_element_type=F32)


def _dot_nt(a, b):
    return lax.dot_general(a, b, (((1,), (1,)), ((), ())), preferred_element_type=F32)


def _dot_tn(a, b):
    return lax.dot_general(a, b, (((0,), (0,)), ((), ())), preferred_element_type=F32)


def _rms(x, w):
    ms = jnp.mean(x * x, axis=-1, keepdims=True)
    return (x * lax.rsqrt(ms + EPS)) * w


def _sigmoid(x):
    return jax.nn.sigmoid(x)


def _softplus(x):
    return jnp.maximum(x, 0.0) + jnp.log1p(jnp.exp(-jnp.abs(x)))


def _row_block(t, cap):
    if t <= cap:
        return t
    b = cap
    while t % b:
        b //= 2
    return b


_IN_WIDTHS = (2 * D_MODEL, D_INNER, CONV_DIM, Q_LORA, KV_LORA, LANES, LANES)


def _in_proj_body(x_ref, nw_ref, w_ref, *out_refs):
    h = _rms(x_ref[...], nw_ref[...]).astype(BF16)
    off = 0
    for ref, wd in zip(out_refs, _IN_WIDTHS):
        ref[...] = _dot(h, w_ref[:, off:off + wd])
        off += wd


def _in_proj(x2d, nw, wcat):
    t = x2d.shape[0]
    tm = _row_block(t, 256)
    wtot = sum(_IN_WIDTHS)
    return pl.pallas_call(
        _in_proj_body,
        grid=(t // tm,),
        in_specs=[pl.BlockSpec((tm, D_MODEL), lambda i: (i, 0)),
                  pl.BlockSpec((1, D_MODEL), lambda i: (0, 0)),
                  pl.BlockSpec((D_MODEL, wtot), lambda i: (0, 0))],
        out_specs=[pl.BlockSpec((tm, wd), lambda i: (i, 0)) for wd in _IN_WIDTHS],
        out_shape=[jax.ShapeDtypeStruct((t, wd), F32) for wd in _IN_WIDTHS],
        compiler_params=_cparams("parallel"),
        name="in_proj",
    )(x2d, nw, wcat)


def _head_expand():
    r = lax.broadcasted_iota(I32, (LANES, D_INNER), 0)
    c = lax.broadcasted_iota(I32, (LANES, D_INNER), 1)
    return (jnp.right_shift(c, 6) == (r - DT_LANE)).astype(F32)


def _ssd_prompt_body(nc, xbc_ref, sm_ref, z_ref, cw_ref, cb_ref, dtb_ref, alog_ref, dsk_ref, nrm_ref,
                     yn_ref, hfin_ref, xpad_ref, st_ref):
    c = pl.program_id(1)
    cl = SSD_CHUNK

    @pl.when(c == 0)
    def _():
        xpad_ref[0:8, :] = jnp.zeros((8, CONV_DIM), F32)
        st_ref[...] = jnp.zeros_like(st_ref)

    xpad_ref[8:8 + cl, :] = xbc_ref[...]
    conv = cb_ref[...]
    for k in range(CONV_WIDTH):
        conv = conv + xpad_ref[5 + k:5 + k + cl, :] * cw_ref[k:k + 1, :]
    xpad_ref[0:8, :] = xpad_ref[cl:cl + 8, :]
    u = conv * _sigmoid(conv)
    xs = u[:, :D_INNER]
    bb = u[:, D_INNER:D_INNER + SSM_GROUPS * D_STATE].astype(BF16)
    cb = u[:, D_INNER + SSM_GROUPS * D_STATE:].astype(BF16)

    row = lax.broadcasted_iota(I32, (cl, cl), 0)
    col = lax.broadcasted_iota(I32, (cl, cl), 1)
    dt_ok = (col >= DT_LANE) & (col < DT_LANE + SSM_HEADS)
    dt = jnp.where(dt_ok, _softplus(sm_ref[...] + dtb_ref[...]), 0.0)
    a = dt * (-jnp.exp(alog_ref[...]))
    tri = (row >= col).astype(F32)
    a_cs = jnp.dot(tri, a, precision=_HI, preferred_element_type=F32)
    a_cs_t = a_cs.T
    ea = jnp.exp(a_cs)
    dte = jnp.exp(a_cs[cl - 1:cl, :] - a_cs)
    ex = _head_expand()
    dt_e = jnp.dot(dt, ex, precision=_HI, preferred_element_type=F32)
    ea_e = jnp.dot(ea, ex, precision=_HI, preferred_element_type=F32)
    dte_e = jnp.dot(dte, ex, precision=_HI, preferred_element_type=F32)

    xdt = xs * dt_e
    xdt_b = xdt.astype(BF16)
    causal = row >= col
    ys = []
    for g in range(SSM_GROUPS):
        cbm = _dot_nt(cb[:, g * D_STATE:(g + 1) * D_STATE], bb[:, g * D_STATE:(g + 1) * D_STATE])
        for e in range(SSM_HPG):
            h = g * SSM_HPG + e
            seg = a_cs[:, DT_LANE + h:DT_LANE + h + 1] - a_cs_t[DT_LANE + h:DT_LANE + h + 1, :]
            lm = jnp.exp(jnp.where(causal, seg, -jnp.inf))
            ys.append(_dot((cbm * lm).astype(BF16), xdt_b[:, h * SSM_HEAD_DIM:(h + 1) * SSM_HEAD_DIM]))
    y_diag = jnp.concatenate(ys, axis=1)

    xw = (xdt * dte_e).astype(BF16)
    dec = ea_e[cl - 1:cl, :]
    half = D_INNER // SSM_GROUPS
    yoffs = []
    for g in range(SSM_GROUPS):
        s_prev = st_ref[g]
        yoffs.append(_dot(cb[:, g * D_STATE:(g + 1) * D_STATE], s_prev.astype(BF16)))
        st_ref[g] = dec[:, g * half:(g + 1) * half] * s_prev + _dot_tn(
            bb[:, g * D_STATE:(g + 1) * D_STATE], xw[:, g * half:(g + 1) * half])
    y = y_diag + jnp.concatenate(yoffs, axis=1) * ea_e + dsk_ref[...] * xs
    zv = z_ref[...]
    y = y * (zv * _sigmoid(zv))
    outs = []
    for g in range(SSM_GROUPS):
        outs.append(_rms(y[:, g * half:(g + 1) * half], nrm_ref[:, g * half:(g + 1) * half]))
    yn_ref[...] = jnp.concatenate(outs, axis=1).astype(BF16)

    @pl.when(c == nc - 1)
    def _():
        for g in range(SSM_GROUPS):
            s_t = st_ref[g].T
            for e in range(SSM_HPG):
                hfin_ref[0, g, e] = s_t[e * SSM_HEAD_DIM:(e + 1) * SSM_HEAD_DIM, :]


def _ssd_prompt(xbc, sm2, z, lw, nb, seq):
    nc = seq // SSD_CHUNK
    cl = SSD_CHUNK
    rowmap = lambda b, c: (b * nc + c, 0)
    full = lambda b, c: (0, 0)
    return pl.pallas_call(
        functools.partial(_ssd_prompt_body, nc),
        grid=(nb, nc),
        in_specs=[pl.BlockSpec((cl, CONV_DIM), rowmap),
                  pl.BlockSpec((cl, LANES), rowmap),
                  pl.BlockSpec((cl, D_INNER), rowmap),
                  pl.BlockSpec((CONV_WIDTH, CONV_DIM), full),
                  pl.BlockSpec((1, CONV_DIM), full),
                  pl.BlockSpec((1, LANES), full),
                  pl.BlockSpec((1, LANES), full),
                  pl.BlockSpec((1, D_INNER), full),
                  pl.BlockSpec((1, D_INNER), full)],
        out_specs=[pl.BlockSpec((cl, D_INNER), rowmap),
                   pl.BlockSpec((1, SSM_GROUPS, SSM_HPG, SSM_HEAD_DIM, D_STATE), lambda b, c: (b, 0, 0, 0, 0))],
        out_shape=[jax.ShapeDtypeStruct((nb * seq, D_INNER), BF16),
                   jax.ShapeDtypeStruct((nb, SSM_GROUPS, SSM_HPG, SSM_HEAD_DIM, D_STATE), F32)],
        scratch_shapes=[pltpu.VMEM((cl + 8, CONV_DIM), F32),
                        pltpu.VMEM((SSM_GROUPS, D_STATE, D_INNER // SSM_GROUPS), F32)],
        compiler_params=_cparams("parallel", "arbitrary"),
        name="ssd_prompt",
    )(xbc, sm2, z, lw["conv_w"], lw["conv_b"], lw["dt_bias_p"], lw["a_log_p"], lw["d_skip_e"], lw["ssm_norm"])


def _ssd_step_pre_body(xbc_ref, prev_ref, sm_ref, cw_ref, cb_ref, dtb_ref, u_ref, dt_ref):
    conv = cb_ref[...]
    for k in range(CONV_WIDTH - 1):
        conv = conv + prev_ref[k] * cw_ref[k:k + 1, :]
    conv = conv + xbc_ref[...] * cw_ref[CONV_WIDTH - 1:CONV_WIDTH, :]
    u_ref[...] = conv * _sigmoid(conv)
    dt_ref[...] = _softplus(sm_ref[...] + dtb_ref[...])


def _ssd_step_pre(xbc, prev_t, sm2, lw):
    nb = xbc.shape[0]
    return pl.pallas_call(
        _ssd_step_pre_body,
        out_shape=[jax.ShapeDtypeStruct((nb, CONV_DIM), F32), jax.ShapeDtypeStruct((nb, LANES), F32)],
        compiler_params=pltpu.CompilerParams(vmem_limit_bytes=VMEM_LIMIT),
        name="ssd_step_pre",
    )(xbc, prev_t, sm2, lw["conv_w"], lw["conv_b"], lw["dt_bias_p"])


def _ssd_step_body(h0_ref, xt_ref, zt_ref, b_ref, c_ref, dt_ref, alog_ref, dsk_ref, nrm_ref, hn_ref, yt_ref):
    xt = xt_ref[0]
    dtv = dt_ref[0]
    av = -jnp.exp(alog_ref[...])
    lane = lax.broadcasted_iota(I32, (SSM_HEAD_DIM, SSM_HEADS), 1)
    y = jnp.zeros((SSM_HEAD_DIM, SSM_HEADS), F32)
    for g in range(SSM_GROUPS):
        bg = b_ref[0][:, g * D_STATE:(g + 1) * D_STATE]
        cg = c_ref[0][:, g * D_STATE:(g + 1) * D_STATE]
        for e in range(SSM_HPG):
            h = g * SSM_HPG + e
            dth = dtv[:, DT_LANE + h:DT_LANE + h + 1]
            dec = jnp.exp(dth * av[:, DT_LANE + h:DT_LANE + h + 1])
            hn = dec * h0_ref[0, g, e] + (xt[:, h:h + 1] * dth) * bg
            hn_ref[0, g, e] = hn
            ycol = jnp.sum(hn * cg, axis=1, keepdims=True)
            y = jnp.where(lane == h, ycol, y)
    y = y + dsk_ref[...] * xt
    zv = zt_ref[0]
    y = y * (zv * _sigmoid(zv))
    sq = y * y
    in_g0 = lane < SSM_HPG
    n_g = float(D_INNER // SSM_GROUPS)
    ms0 = jnp.sum(jnp.sum(jnp.where(in_g0, sq, 0.0), axis=1, keepdims=True), axis=0, keepdims=True) / n_g
    ms1 = jnp.sum(jnp.sum(jnp.where(in_g0, 0.0, sq), axis=1, keepdims=True), axis=0, keepdims=True) / n_g
    scale = jnp.where(in_g0, lax.rsqrt(ms0 + EPS), lax.rsqrt(ms1 + EPS))
    yt_ref[0] = (y * scale) * nrm_ref[...]


def _ssd_step(h0_all, layer, xt, zt, bm, cm, dt, lw):
    nb = xt.shape[0]
    st_block = (None, 1, SSM_GROUPS, SSM_HPG, SSM_HEAD_DIM, D_STATE)
    pt_block = (1, SSM_HEAD_DIM, SSM_HEADS)
    vec = lambda n: pl.BlockSpec((1, 1, n), lambda b: (b, 0, 0))
    full = lambda r, n: pl.BlockSpec((r, n), lambda b: (0, 0))
    return pl.pallas_call(
        _ssd_step_body,
        grid=(nb,),
        in_specs=[pl.BlockSpec(st_block, lambda b: (layer, b, 0, 0, 0, 0)),
                  pl.BlockSpec(pt_block, lambda b: (b, 0, 0)),
                  pl.BlockSpec(pt_block, lambda b: (b, 0, 0)),
                  vec(SSM_GROUPS * D_STATE), vec(SSM_GROUPS * D_STATE), vec(LANES),
                  full(1, LANES), full(1, SSM_HEADS), full(SSM_HEAD_DIM, SSM_HEADS)],
        out_specs=[pl.BlockSpec((1, SSM_GROUPS, SSM_HPG, SSM_HEAD_DIM, D_STATE), lambda b: (b, 0, 0, 0, 0)),
                   pl.BlockSpec(pt_block, lambda b: (b, 0, 0))],
        out_shape=[jax.ShapeDtypeStruct((nb, SSM_GROUPS, SSM_HPG, SSM_HEAD_DIM, D_STATE), F32),
                   jax.ShapeDtypeStruct((nb, SSM_HEAD_DIM, SSM_HEADS), F32)],
        compiler_params=_cparams("parallel"),
        name="ssd_step",
    )(h0_all, xt, zt, bm, cm, dt, lw["a_log_p"], lw["d_skip_row"], lw["ssm_norm_t"])


def _mla_prep_body(cq_ref, ckv_ref, sm1_ref, sm2_ref, qtab_ref, ktab_ref, qn_ref, kvn_ref,
                   wn_ref, wpe_ref, wpes_ref, wkb_ref, qcat_ref, kcat_ref, lat_ref, kpe_ref):
    hq = _rms(cq_ref[...], qn_ref[...]).astype(BF16)
    qn = _dot(hq, wn_ref[...]).astype(BF16)
    npe = MLA_HEADS * LANES
    qpe = (_dot(hq, wpe_ref[...]) * qtab_ref[:, :npe] + _dot(hq, wpes_ref[...]) * qtab_ref[:, npe:]) * ATTN_SCALE
    for h in range(MLA_HEADS):
        ql = _dot(qn[:, h * QK_NOPE:(h + 1) * QK_NOPE], wkb_ref[h]) * ATTN_SCALE
        qcat_ref[0, h] = jnp.concatenate([ql, qpe[:, h * LANES:(h + 1) * LANES]], axis=1).astype(BF16)
    lat = _rms(ckv_ref[...], kvn_ref[...])
    lat_ref[...] = lat
    kpe = sm1_ref[...] * ktab_ref[:, :LANES] + sm2_ref[...] * ktab_ref[:, LANES:]
    kpe_ref[...] = kpe[:, :QK_ROPE]
    kcat_ref[...] = jnp.concatenate([lat, kpe], axis=1).astype(BF16)


def _mla_prep(cq, ckv, sm1, sm2, qtab, ktab, lw, nb, seq):
    t = nb * seq
    tm = _row_block(seq, 256)
    nl = seq // tm
    rowmap = lambda b, i: (b * nl + i, 0)
    tabmap = lambda b, i: (i, 0)
    full2 = lambda b, i: (0, 0)
    npe = MLA_HEADS * LANES
    return pl.pallas_call(
        _mla_prep_body,
        grid=(nb, nl),
        in_specs=[pl.BlockSpec((tm, Q_LORA), rowmap),
                  pl.BlockSpec((tm, KV_LORA), rowmap),
                  pl.BlockSpec((tm, LANES), rowmap),
                  pl.BlockSpec((tm, LANES), rowmap),
                  pl.BlockSpec((tm, 2 * npe), tabmap),
                  pl.BlockSpec((tm, 2 * LANES), tabmap),
                  pl.BlockSpec((1, Q_LORA), full2),
                  pl.BlockSpec((1, KV_LORA), full2),
                  pl.BlockSpec((Q_LORA, MLA_HEADS * QK_NOPE), full2),
                  pl.BlockSpec((Q_LORA, npe), full2),
                  pl.BlockSpec((Q_LORA, npe), full2),
                  pl.BlockSpec((MLA_HEADS, QK_NOPE, KV_LORA), lambda b, i: (0, 0, 0))],
        out_specs=[pl.BlockSpec((1, MLA_HEADS, tm, QK_CAT), lambda b, i: (b, 0, i, 0)),
                   pl.BlockSpec((tm, QK_CAT), rowmap),
                   pl.BlockSpec((tm, KV_LORA), rowmap),
                   pl.BlockSpec((tm, QK_ROPE), rowmap)],
        out_shape=[jax.ShapeDtypeStruct((nb, MLA_HEADS, seq, QK_CAT), BF16),
                   jax.ShapeDtypeStruct((t, QK_CAT), BF16),
                   jax.ShapeDtypeStruct((t, KV_LORA), F32),
                   jax.ShapeDtypeStruct((t, QK_ROPE), F32)],
        compiler_params=_cparams("parallel", "parallel"),
        name="mla_prep",
    )(cq, ckv, sm1, sm2, qtab, ktab, lw["q_norm"], lw["kv_norm"], lw["wq_nope"], lw["wq_pe"], lw["wq_pe_sw"],
      lw["w_kb_t"])


def _attn_prompt_body(tq, tk, nk, q_ref, k_ref, wvb_ref, o_ref, m_ref, l_ref, acc_ref):
    i = pl.program_id(1)
    j = pl.program_id(2)
    rows = MLA_HEADS * tq

    @pl.when(j == 0)
    def _():
        m_ref[...] = jnp.full(m_ref.shape, -jnp.inf, F32)
        l_ref[...] = jnp.zeros(l_ref.shape, F32)
        acc_ref[...] = jnp.zeros(acc_ref.shape, F32)

    @pl.when(j * tk <= i * tq + tq - 1)
    def _():
        q = q_ref[0].reshape(rows, QK_CAT)
        k = k_ref[...]
        s = _dot_nt(q, k)
        qpos = i * tq + jnp.bitwise_and(lax.broadcasted_iota(I32, (rows, tk), 0), tq - 1)
        kpos = j * tk + lax.broadcasted_iota(I32, (rows, tk), 1)
        s = jnp.where(kpos <= qpos, s, -jnp.inf)
        m_prev = m_ref[...]
        m_new = jnp.maximum(m_prev, jnp.max(s, axis=1, keepdims=True))
        p = jnp.exp(s - m_new)
        alpha = jnp.exp(m_prev - m_new)
        l_ref[...] = alpha * l_ref[...] + jnp.sum(p, axis=1, keepdims=True)
        acc_ref[...] = alpha * acc_ref[...] + _dot(p.astype(BF16), k[:, :KV_LORA])
        m_ref[...] = m_new

    @pl.when(j == nk - 1)
    def _():
        o = (acc_ref[...] / l_ref[...]).astype(BF16)
        outs = [_dot(o[h * tq:(h + 1) * tq, :], wvb_ref[h]) for h in range(MLA_HEADS)]
        o_ref[...] = jnp.concatenate(outs, axis=1).astype(BF16)


def _attn_prompt(qcat, kcat, w_vb, nb, seq):
    tq = _row_block(seq, 256)
    tk = _row_block(seq, 512)
    assert tq & (tq - 1) == 0
    nq, nk = seq // tq, seq // tk

    def kmap(b, i, j):
        return (b * nk + jnp.minimum(j, (i * tq + tq - 1) // tk), 0)

    return pl.pallas_call(
        functools.partial(_attn_prompt_body, tq, tk, nk),
        grid=(nb, nq, nk),
        in_specs=[pl.BlockSpec((1, MLA_HEADS, tq, QK_CAT), lambda b, i, j: (b, 0, i, 0)),
                  pl.BlockSpec((tk, QK_CAT), kmap),
                  pl.BlockSpec((MLA_HEADS, KV_LORA, V_HEAD), lambda b, i, j: (0, 0, 0))],
        out_specs=pl.BlockSpec((tq, MLA_HEADS * V_HEAD), lambda b, i, j: (b * nq + i, 0)),
        out_shape=jax.ShapeDtypeStruct((nb * seq, MLA_HEADS * V_HEAD), BF16),
        scratch_shapes=[pltpu.VMEM((MLA_HEADS * tq, 1), F32), pltpu.VMEM((MLA_HEADS * tq, 1), F32),
                        pltpu.VMEM((MLA_HEADS * tq, KV_LORA), F32)],
        compiler_params=_cparams("parallel", "parallel", "arbitrary"),
        name="attn_prompt",
    )(qcat, kcat, w_vb)


def _attn_decode_body(pg, nj, pt_ref, q_ref, kc_ref, wvb_ref, *refs):
    lat_refs = refs[:pg]
    kr_refs = refs[pg:2 * pg]
    o_ref, m_ref, l_ref, acc_ref = refs[2 * pg:]
    j = pl.program_id(1)

    @pl.when(j == 0)
    def _():
        m_ref[...] = jnp.full(m_ref.shape, -jnp.inf, F32)
        l_ref[...] = jnp.zeros(l_ref.shape, F32)
        acc_ref[...] = jnp.zeros(acc_ref.shape, F32)

    q = q_ref[0]
    ql = q[:, :KV_LORA]
    qr = q[:, KV_LORA:KV_LORA + QK_ROPE]
    for i in range(pg):
        kv = lat_refs[i][...].astype(BF16)
        kr = kr_refs[i][...].astype(BF16)
        s = _dot_nt(ql, kv) + _dot_nt(qr, kr)
        m_prev = m_ref[...]
        m_new = jnp.maximum(m_prev, jnp.max(s, axis=1, keepdims=True))
        p = jnp.exp(s - m_new)
        alpha = jnp.exp(m_prev - m_new)
        l_ref[...] = alpha * l_ref[...] + jnp.sum(p, axis=1, keepdims=True)
        acc_ref[...] = alpha * acc_ref[...] + _dot(p.astype(BF16), kv)
        m_ref[...] = m_new

    @pl.when(j == nj - 1)
    def _():
        kc = kc_ref[0].astype(F32)
        s = jnp.sum(q.astype(F32) * kc, axis=1, keepdims=True)
        m_prev = m_ref[...]
        m_new = jnp.maximum(m_prev, s)
        p = jnp.exp(s - m_new)
        alpha = jnp.exp(m_prev - m_new)
        l_fin = alpha * l_ref[...] + p
        acc = alpha * acc_ref[...] + p.astype(BF16).astype(F32) * kc[:, :KV_LORA]
        o = (acc / l_fin).astype(BF16)
        full = _dot(o, wvb_ref[...])
        hrow = lax.broadcasted_iota(I32, full.shape, 0)
        hcol = jnp.right_shift(lax.broadcasted_iota(I32, full.shape, 1), 6)
        o_ref[0] = jnp.sum(jnp.where(hrow == hcol, full, 0.0), axis=0, keepdims=True).astype(BF16)


def _attn_decode(page_table, qdec, kcur, w_vb_all, cache_latent, cache_krope, layer):
    nb, n_pages = page_table.shape
    page = cache_latent.shape[2]
    pg = 8
    while n_pages % pg:
        pg //= 2
    nj = n_pages // pg

    def pmap(i):
        return lambda b, j, pt: (layer, pt[b, j * pg + i], 0, 0)

    in_specs = [pl.BlockSpec((1, MLA_HEADS, QK_CAT), lambda b, j, pt: (b, 0, 0)),
                pl.BlockSpec((1, 1, QK_CAT), lambda b, j, pt: (b, 0, 0)),
                pl.BlockSpec((KV_LORA, MLA_HEADS * V_HEAD), lambda b, j, pt: (0, 0))]
    in_specs += [pl.BlockSpec((None, None, page, KV_LORA), pmap(i)) for i in range(pg)]
    in_specs += [pl.BlockSpec((None, None, page, QK_ROPE), pmap(i)) for i in range(pg)]
    grid_spec = pltpu.PrefetchScalarGridSpec(
        num_scalar_prefetch=1,
        grid=(nb, nj),
        in_specs=in_specs,
        out_specs=pl.BlockSpec((1, 1, MLA_HEADS * V_HEAD), lambda b, j, pt: (b, 0, 0)),
        scratch_shapes=[pltpu.VMEM((MLA_HEADS, 1), F32), pltpu.VMEM((MLA_HEADS, 1), F32),
                        pltpu.VMEM((MLA_HEADS, KV_LORA), F32)])
    return pl.pallas_call(
        functools.partial(_attn_decode_body, pg, nj),
        grid_spec=grid_spec,
        out_shape=jax.ShapeDtypeStruct((nb, 1, MLA_HEADS * V_HEAD), BF16),
        compiler_params=_cparams("parallel", "arbitrary"),
        name="attn_decode",
    )(page_table, qdec, kcur, w_vb_all, *([cache_latent] * pg), *([cache_krope] * pg))


def _mix_body(x_ref, g_ref, bg_ref, yn_ref, o_ref, wssm_ref, wmla_ref, wout_ref, x1_ref):
    ya = _dot(yn_ref[...], wssm_ref[...])
    yb = _dot(o_ref[...], wmla_ref[...])
    gate = _sigmoid(g_ref[...] + bg_ref[...])
    mix = gate[:, :D_MODEL] * ya + gate[:, D_MODEL:] * yb
    x1_ref[...] = x_ref[...] + _dot(mix.astype(BF16), wout_ref[...])


def _mix(x2d, gates, yn, o, lw):
    t = x2d.shape[0]
    tm = _row_block(t, 256)
    row = lambda n: pl.BlockSpec((tm, n), lambda i: (i, 0))
    full = lambda r, n: pl.BlockSpec((r, n), lambda i: (0, 0))
    return pl.pallas_call(
        _mix_body,
        grid=(t // tm,),
        in_specs=[row(D_MODEL), row(2 * D_MODEL), full(1, 2 * D_MODEL), row(D_INNER), row(MLA_HEADS * V_HEAD),
                  full(D_INNER, D_MODEL), full(MLA_HEADS * V_HEAD, D_MODEL), full(D_MODEL, D_MODEL)],
        out_specs=row(D_MODEL),
        out_shape=jax.ShapeDtypeStruct((t, D_MODEL), F32),
        compiler_params=_cparams("parallel"),
        name="mix",
    )(x2d, gates, lw["b_gate"], yn, o, lw["w_ssm_proj"], lw["w_mla_proj"], lw["w_out"])


def _peer_scores_body(x_ref, nw_ref, wq_ref, keys_ref, hnt_ref, s1_ref, s2_ref):
    hn = _rms(x_ref[...], nw_ref[...])
    hnt_ref[...] = hn.T.astype(BF16)
    q = _dot(hn.astype(BF16), wq_ref[...]).astype(BF16)
    half = PEER_D_KEY // 2
    for h in range(PEER_HEADS):
        s1_ref[h] = _dot_nt(keys_ref[h, 0], q[:, h * PEER_D_KEY:h * PEER_D_KEY + half])
        s2_ref[h] = _dot_nt(keys_ref[h, 1], q[:, h * PEER_D_KEY + half:(h + 1) * PEER_D_KEY])


def _peer_scores(x1, lw):
    t = x1.shape[0]
    tm = _row_block(t, 256)
    sblock = pl.BlockSpec((PEER_HEADS, N_KEYS, tm), lambda i: (0, 0, i))
    return pl.pallas_call(
        _peer_scores_body,
        grid=(t // tm,),
        in_specs=[pl.BlockSpec((tm, D_MODEL), lambda i: (i, 0)),
                  pl.BlockSpec((1, D_MODEL), lambda i: (0, 0)),
                  pl.BlockSpec((D_MODEL, PEER_HEADS * PEER_D_KEY), lambda i: (0, 0)),
                  pl.BlockSpec((PEER_HEADS, 2, N_KEYS, PEER_D_KEY // 2), lambda i: (0, 0, 0, 0))],
        out_specs=[pl.BlockSpec((D_MODEL, tm), lambda i: (0, i)), sblock, sblock],
        out_shape=[jax.ShapeDtypeStruct((D_MODEL, t), BF16),
                   jax.ShapeDtypeStruct((PEER_HEADS, N_KEYS, t), F32),
                   jax.ShapeDtypeStruct((PEER_HEADS, N_KEYS, t), F32)],
        compiler_params=_cparams("parallel"),
        name="peer_scores",
    )(x1, lw["norm_ffn"], lw["peer_wq"], lw["peer_keys"])


def _top16_rows(s):
    n = s.shape[0]
    row = lax.broadcasted_iota(I32, s.shape, 0)
    r16 = lax.broadcasted_iota(I32, (PEER_TOPK, LANES), 0)

    def body(r, carry):
        rank, vals = carry
        alive = rank == PEER_TOPK
        m = jnp.max(jnp.where(alive, s, -jnp.inf), axis=0, keepdims=True)
        idx = jnp.min(jnp.where(alive & (s == m), row, n), axis=0, keepdims=True)
        rank = jnp.where(row == idx, r, rank)
        vals = jnp.where(r16 == r, m, vals)
        return rank, vals

    return lax.fori_loop(0, PEER_TOPK, body,
                         (jnp.full(s.shape, PEER_TOPK, I32), jnp.zeros((PEER_TOPK, LANES), F32)))


def _peer_gates_body(s1_ref, s2_ref, r2_ref, e2_ref, cnt_ref, e1_ref):
    k = PEER_TOPK
    r8 = lax.broadcasted_iota(I32, (8, LANES), 0)

    def head(h, _):
        s1 = s1_ref[h]
        s2 = s2_ref[h]
        rank1, sc1 = _top16_rows(s1)
        rank2, sc2 = _top16_rows(s2)
        pieces = [sc1[0:1] + sc2]
        pieces += [sc1[r:r + 1] + sc2[0:8] for r in range(1, 8)]
        pieces += [sc1[8:16] + sc2[0:1]]
        cand = jnp.concatenate(pieces, axis=0)
        rankc, _ = _top16_rows(cand)
        sel = rankc < k
        seli = sel.astype(I32)
        n_lo = jnp.zeros((8, LANES), I32)
        n_lo = jnp.where(r8 == 0, jnp.sum(seli[0:16], axis=0, keepdims=True), n_lo)
        for r in range(1, 8):
            n_lo = jnp.where(r8 == r, jnp.sum(seli[8 + 8 * r:16 + 8 * r], axis=0, keepdims=True), n_lo)
        ncnt = jnp.concatenate([n_lo, seli[72:80]], axis=0)
        z = jnp.sum(jnp.where(sel, jnp.exp(cand - cand[0:1]), 0.0), axis=0, keepdims=True)
        cnt = jnp.zeros((N_KEYS, LANES), I32)
        for r in range(k):
            cnt = jnp.where(rank1 == r, ncnt[r:r + 1], cnt)
        cnt_ref[h] = cnt.astype(F32)
        r2_ref[h] = rank2.astype(F32)
        e1_ref[h] = jnp.where(rank1 < k, jnp.exp(s1 - sc1[0:1]), 0.0)
        e2_ref[h] = jnp.where(rank2 < k, jnp.exp(s2 - sc2[0:1]) * (0.5 / z), 0.0)
        return 0

    lax.fori_loop(0, PEER_HEADS, head, 0)


def _peer_gates(s1t, s2t):
    t = s1t.shape[2]
    blk = pl.BlockSpec((PEER_HEADS, N_KEYS, LANES), lambda i: (0, 0, i))
    shp = jax.ShapeDtypeStruct((PEER_HEADS, N_KEYS, t), F32)
    return pl.pallas_call(
        _peer_gates_body,
        grid=(t // LANES,),
        in_specs=[blk, blk],
        out_specs=[blk, blk, blk, blk],
        out_shape=[shp, shp, shp, shp],
        compiler_params=_cparams("parallel"),
        name="peer_gates",
    )(s1t, s2t)


_SQRT_HALF = 0.7071067811865476


def _peer_experts_body(tm, eb, nj, final, hnt_ref, u_ref, vt_ref, r2_ref, e2_ref, cnt_ref, e1_ref, x_ref, nf_ref,
                       o_ref, acc_ref, at_ref, wt_ref):
    j = pl.program_id(1)

    @pl.when(j == 0)
    def _():
        acc_ref[...] = jnp.zeros(acc_ref.shape, F32)

    at_ref[...] = _dot(u_ref[...], hnt_ref[...])
    tiles = eb // N_KEYS
    a0 = pl.multiple_of(j * tiles, tiles)
    for s in range(tm // LANES):
        ls = slice(s * LANES, (s + 1) * LANES)
        cnts = [cnt_ref[h, pl.ds(a0, tiles), ls] for h in range(PEER_HEADS)]
        e1s = [e1_ref[h, pl.ds(a0, tiles), ls] for h in range(PEER_HEADS)]
        for al in range(tiles):
            g = jnp.zeros((N_KEYS, LANES), F32)
            for h in range(PEER_HEADS):
                g = g + jnp.where(r2_ref[h, :, ls] < cnts[h][al:al + 1], e2_ref[h, :, ls], 0.0) * e1s[h][al:al + 1]
            act = at_ref[al * N_KEYS:(al + 1) * N_KEYS, ls]
            wt_ref[al * N_KEYS:(al + 1) * N_KEYS, ls] = (g * (act * (1.0 + lax.erf(act * _SQRT_HALF)))).astype(BF16)
    acc_ref[...] += _dot(vt_ref[...], wt_ref[...])

    @pl.when(j == nj - 1)
    def _():
        x2 = x_ref[...] + acc_ref[...].T
        if final:
            o_ref[...] = _rms(x2, nf_ref[...])
        else:
            o_ref[...] = x2


def _peer_experts(hnt, gates, x1, lw, norm_final, final):
    t = x1.shape[0]
    tm = _row_block(t, 512)
    eb = 1024
    nj = N_EXPERTS // eb
    r2, e2, cnt, e1 = gates
    gblk = pl.BlockSpec((PEER_HEADS, N_KEYS, tm), lambda i, j: (0, 0, i))
    return pl.pallas_call(
        functools.partial(_peer_experts_body, tm, eb, nj, final),
        grid=(t // tm, nj),
        in_specs=[pl.BlockSpec((D_MODEL, tm), lambda i, j: (0, i)),
                  pl.BlockSpec((eb, D_MODEL), lambda i, j: (j, 0)),
                  pl.BlockSpec((D_MODEL, eb), lambda i, j: (0, j)),
                  gblk, gblk, gblk, gblk,
                  pl.BlockSpec((tm, D_MODEL), lambda i, j: (i, 0)),
                  pl.BlockSpec((1, D_MODEL), lambda i, j: (0, 0))],
        out_specs=pl.BlockSpec((tm, D_MODEL), lambda i, j: (i, 0)),
        out_shape=jax.ShapeDtypeStruct((t, D_MODEL), F32),
        scratch_shapes=[pltpu.VMEM((D_MODEL, tm), F32), pltpu.VMEM((eb, tm), F32), pltpu.VMEM((eb, tm), BF16)],
        compiler_params=_cparams("parallel", "arbitrary"),
        name="peer_experts",
    )(hnt, lw["peer_u"], lw["peer_v_t"], r2, e2, cnt, e1, x1, norm_final)


def _peer(x1, lw, norm_final, final):
    hnt, s1t, s2t = _peer_scores(x1, lw)
    gates = _peer_gates(s1t, s2t)
    return _peer_experts(hnt, gates, x1, lw, norm_final, final)


def _pad_lanes(v, start, total=LANES):
    return jnp.zeros((1, total), F32).at[0, start:start + v.shape[0]].set(v)


def _layer_weights(l, norm_mix, w_in, b_gate, conv_w, conv_b, dt_bias, a_log, d_skip, ssm_norm, w_ssm_proj,
                   q_norm, w_qb, kv_norm, w_kvb, w_mla_proj, w_out, norm_ffn, peer_wq, peer_keys, peer_u, peer_v):
    o_dt = 3 * D_MODEL + CONV_DIM
    o_cq = o_dt + SSM_HEADS
    o_ckv = o_cq + Q_LORA
    o_kr = o_ckv + KV_LORA
    w = w_in[l]
    half = QK_ROPE // 2
    zcols = lambda n: jnp.zeros((D_MODEL, n), F32)
    kr = w[:, o_kr:o_kr + QK_ROPE]
    kr_sw = jnp.concatenate([kr[:, half:], kr[:, :half]], axis=1)
    sm1 = jnp.concatenate([kr, zcols(LANES - QK_ROPE)], axis=1)
    sm2 = jnp.concatenate([kr_sw, zcols(DT_LANE - QK_ROPE), w[:, o_dt:o_cq], zcols(LANES - DT_LANE - SSM_HEADS)], axis=1)
    wcat = jnp.concatenate([w[:, :o_dt], w[:, o_cq:o_kr], sm1, sm2], axis=1).astype(BF16)

    wq = w_qb[l].reshape(Q_LORA, MLA_HEADS, QK_NOPE + QK_ROPE)
    wq_nope = wq[:, :, :QK_NOPE].reshape(Q_LORA, MLA_HEADS * QK_NOPE)
    pe = wq[:, :, QK_NOPE:]
    pe_sw = jnp.concatenate([pe[:, :, half:], pe[:, :, :half]], axis=2)
    padpe = lambda p: jnp.pad(p, ((0, 0), (0, 0), (0, LANES - QK_ROPE))).reshape(Q_LORA, MLA_HEADS * LANES)
    wkv = w_kvb[l].reshape(KV_LORA, MLA_HEADS, QK_NOPE + V_HEAD)
    w_kb_t = jnp.transpose(wkv[:, :, :QK_NOPE], (1, 2, 0))
    w_vb = jnp.transpose(wkv[:, :, QK_NOPE:], (1, 0, 2))
    w_vb_all = wkv[:, :, QK_NOPE:].reshape(KV_LORA, MLA_HEADS * V_HEAD)
    row = lambda v: v.reshape(1, -1)
    return dict(
        norm_mix=row(norm_mix[l]), wcat=wcat, b_gate=row(b_gate[l]),
        conv_w=conv_w[l], conv_b=row(conv_b[l]),
        dt_bias_p=_pad_lanes(dt_bias[l], DT_LANE), a_log_p=_pad_lanes(a_log[l], DT_LANE),
        d_skip_e=row(jnp.repeat(d_skip[l], SSM_HEAD_DIM)), d_skip_row=row(d_skip[l]),
        ssm_norm=row(ssm_norm[l]), ssm_norm_t=ssm_norm[l].reshape(SSM_HEADS, SSM_HEAD_DIM).T,
        w_ssm_proj=w_ssm_proj[l].astype(BF16),
        q_norm=row(q_norm[l]), kv_norm=row(kv_norm[l]),
        wq_nope=wq_nope.astype(BF16), wq_pe=padpe(pe).astype(BF16), wq_pe_sw=padpe(pe_sw).astype(BF16),
        w_kb_t=w_kb_t.astype(BF16), w_vb=w_vb.astype(BF16), w_vb_all=w_vb_all.astype(BF16),
        w_mla_proj=w_mla_proj[l].astype(BF16), w_out=w_out[l].astype(BF16),
        norm_ffn=row(norm_ffn[l]), peer_wq=peer_wq[l].astype(BF16), peer_keys=peer_keys[l].astype(BF16),
        peer_u=peer_u[l].astype(BF16), peer_v_t=peer_v[l].T.astype(BF16))


def _rope_tables(pos):
    half = QK_ROPE // 2
    inv = ROPE_THETA ** (-jnp.arange(half, dtype=F32) / half)
    ang = pos.astype(F32)[:, None] * inv[None, :]
    cos, sin = jnp.cos(ang), jnp.sin(ang)
    pad = jnp.zeros((pos.shape[0], LANES - QK_ROPE), F32)
    c128 = jnp.concatenate([cos, cos, pad], axis=1)
    s128 = jnp.concatenate([-sin, sin, pad], axis=1)
    ktab = jnp.concatenate([c128, s128], axis=1)
    qtab = jnp.concatenate([jnp.tile(c128, (1, MLA_HEADS)), jnp.tile(s128, (1, MLA_HEADS))], axis=1)
    return qtab, ktab


def _prompt_layer(x2d, nb, seq, lw, tabs, norm_final, final):
    gates, z, xbc, cq, ckv, sm1, sm2 = _in_proj(x2d, lw["norm_mix"], lw["wcat"])
    yn, h_new = _ssd_prompt(xbc, sm2, z, lw, nb, seq)
    qcat, kcat, lat, kpe = _mla_prep(cq, ckv, sm1, sm2, tabs[0], tabs[1], lw, nb, seq)
    o = _attn_prompt(qcat, kcat, lw["w_vb"], nb, seq)
    x1 = _mix(x2d, gates, yn, o, lw)
    x2 = _peer(x1, lw, norm_final, final)
    conv_new = xbc.reshape(nb, seq, CONV_DIM)[:, seq - (CONV_WIDTH - 1):, :]
    return x2, h_new, conv_new, lat.reshape(nb, seq, KV_LORA), kpe.reshape(nb, seq, QK_ROPE)


def _sample_layer(x2d, layer, lw, tabs, state_ssm, state_conv, cache_latent, cache_krope, page_table,
                  norm_final, final):
    nb = x2d.shape[0]
    gates, z, xbc, cq, ckv, sm1, sm2 = _in_proj(x2d, lw["norm_mix"], lw["wcat"])
    prev = state_conv[layer]
    u, dt = _ssd_step_pre(xbc, jnp.transpose(prev, (1, 0, 2)), sm2, lw)
    to_t = lambda v: jnp.transpose(v.reshape(nb, SSM_HEADS, SSM_HEAD_DIM), (0, 2, 1))
    n_bc = SSM_GROUPS * D_STATE
    h_new, yt = _ssd_step(state_ssm, layer, to_t(u[:, :D_INNER]), to_t(z),
                          u[:, D_INNER:D_INNER + n_bc].reshape(nb, 1, n_bc),
                          u[:, D_INNER + n_bc:].reshape(nb, 1, n_bc), dt.reshape(nb, 1, LANES), lw)
    yn = jnp.transpose(yt, (0, 2, 1)).reshape(nb, D_INNER).astype(BF16)
    conv_new = jnp.concatenate([prev[:, 1:], xbc[:, None, :]], axis=1)

    qcat, kcat, lat, kpe = _mla_prep(cq, ckv, sm1, sm2, tabs[0], tabs[1], lw, 1, nb)
    qdec = jnp.transpose(qcat[0], (1, 0, 2))
    o = _attn_decode(page_table, qdec, kcat.reshape(nb, 1, QK_CAT), lw["w_vb_all"], cache_latent, cache_krope, layer)
    x1 = _mix(x2d, gates, yn, o.reshape(nb, MLA_HEADS * V_HEAD), lw)
    pad = (-nb) % LANES
    x2 = _peer(jnp.pad(x1, ((0, pad), (0, 0))), lw, norm_final, final)[:nb]
    return x2, h_new, conv_new, lat.reshape(nb, 1, KV_LORA), kpe.reshape(nb, 1, QK_ROPE)


def kernel(x_prompt, x_sample, cache_latent, cache_krope, state_ssm, state_conv, page_table, norm_mix, w_in, b_gate,
           conv_w, conv_b, dt_bias, a_log, d_skip, ssm_norm, w_ssm_proj, q_norm, w_qb, kv_norm, w_kvb, w_mla_proj,
           w_out, norm_ffn, peer_wq, peer_keys, peer_u, peer_v, norm_final):
    bp, sp = x_prompt.shape[:2]
    bs, ss = x_sample.shape[:2]
    depth = w_in.shape[0]
    assert ss == 1 and sp % SSD_CHUNK == 0
    past_len = page_table.shape[1] * cache_latent.shape[2]
    tabs_p = _rope_tables(jnp.arange(sp, dtype=I32))
    tabs_s = _rope_tables(jnp.full((bs,), past_len, I32))
    nf = norm_final.reshape(1, D_MODEL)
    xp = x_prompt.reshape(bp * sp, D_MODEL)
    xs = x_sample.reshape(bs, D_MODEL)
    outs_p, outs_s = [], []
    for l in range(depth):
        lw = _layer_weights(l, norm_mix, w_in, b_gate, conv_w, conv_b, dt_bias, a_log, d_skip, ssm_norm, w_ssm_proj,
                            q_norm, w_qb, kv_norm, w_kvb, w_mla_proj, w_out, norm_ffn, peer_wq, peer_keys, peer_u,
                            peer_v)
        final = l == depth - 1
        xp, hp, cp, lp, kp = _prompt_layer(xp, bp, sp, lw, tabs_p, nf, final)
        xs, hs, cs, lsn, ksn = _sample_layer(xs, l, lw, tabs_s, state_ssm, state_conv, cache_latent, cache_krope,
                                             page_table, nf, final)
        outs_p.append((lp, kp, hp, cp))
        outs_s.append((lsn, ksn, hs, cs))
    stack = lambda outs, i: jnp.stack([o[i] for o in outs])
    return (xp.reshape(bp, sp, D_MODEL), xs.reshape(bs, ss, D_MODEL),
            stack(outs_p, 0), stack(outs_p, 1), stack(outs_p, 2), stack(outs_p, 3),
            stack(outs_s, 0), stack(outs_s, 1), stack(outs_s, 2), stack(outs_s, 3))
```

```python
import functools

import numpy as np
import jax
import jax.numpy as jnp
from jax import lax
from jax.experimental import pallas as pl
from jax.experimental.pallas import tpu as pltpu

F32 = jnp.float32
BF16 = jnp.bfloat16
I32 = jnp.int32

D_MODEL = 1024
D_INNER = 1024
SSM_HEAD_DIM = 64
SSM_HEADS = 16
SSM_GROUPS = 2
SSM_HPG = 8
D_STATE = 128
CONV_WIDTH = 4
CONV_DIM = D_INNER + 2 * SSM_GROUPS * D_STATE
SSD_CHUNK = 128
MLA_HEADS = 8
QK_NOPE = 64
QK_ROPE = 32
V_HEAD = 64
Q_LORA = 384
KV_LORA = 256
ROPE_THETA = 10000.0
ATTN_SCALE = (QK_NOPE + QK_ROPE) ** -0.5
PEER_HEADS = 8
N_KEYS = 128
N_EXPERTS = N_KEYS * N_KEYS
PEER_D_KEY = 256
PEER_TOPK = 16
EPS = 1e-6

LANES = 128
QK_CAT = KV_LORA + LANES
DT_LANE = 64
VMEM_LIMIT = 56 * 1024 * 1024

_HI = lax.Precision.HIGHEST


def _cparams(*sem):
    return pltpu.CompilerParams(dimension_semantics=sem, vmem_limit_bytes=VMEM_LIMIT)


def _dot(a, b):
    return jnp.dot(a, b, preferred_element_type=F32)


def _dot_nt(a, b):
    return lax.dot_general(a, b, (((1,), (1,)), ((), ())), preferred_element_type=F32)


def _dot_tn(a, b):
    return lax.dot_general(a, b, (((0,), (0,)), ((), ())), preferred_element_type=F32)


def _rms(x, w):
    ms = jnp.mean(x * x, axis=-1, keepdims=True)
    return (x * lax.rsqrt(ms + EPS)) * w


def _sigmoid(x):
    return jax.nn.sigmoid(x)


def _softplus(x):
    return jnp.maximum(x, 0.0) + jnp.log1p(jnp.exp(-jnp.abs(x)))


def _row_block(t, cap):
    if t <= cap:
        return t
    b = cap
    while t % b:
        b //= 2
    return b


_IN_WIDTHS = (2 * D_MODEL, D_INNER, CONV_DIM, Q_LORA, KV_LORA, LANES, LANES)


def _in_proj_body(x_ref, nw_ref, w_ref, *out_refs):
    h = _rms(x_ref[...], nw_ref[...]).astype(BF16)
    off = 0
    for ref, wd in zip(out_refs, _IN_WIDTHS):
        ref[...] = _dot(h, w_ref[:, off:off + wd])
        off += wd


def _in_proj(x2d, nw, wcat):
    t = x2d.shape[0]
    tm = _row_block(t, 256)
    wtot = sum(_IN_WIDTHS)
    return pl.pallas_call(
        _in_proj_body,
        grid=(t // tm,),
        in_specs=[pl.BlockSpec((tm, D_MODEL), lambda i: (i, 0)),
                  pl.BlockSpec((1, D_MODEL), lambda i: (0, 0)),
                  pl.BlockSpec((D_MODEL, wtot), lambda i: (0, 0))],
        out_specs=[pl.BlockSpec((tm, wd), lambda i: (i, 0)) for wd in _IN_WIDTHS],
        out_shape=[jax.ShapeDtypeStruct((t, wd), F32) for wd in _IN_WIDTHS],
        compiler_params=_cparams("parallel"),
        name="in_proj",
    )(x2d, nw, wcat)


def _head_expand():
    r = lax.broadcasted_iota(I32, (LANES, D_INNER), 0)
    c = lax.broadcasted_iota(I32, (LANES, D_INNER), 1)
    return (jnp.right_shift(c, 6) == (r - DT_LANE)).astype(F32)


def _ssd_prompt_body(nc, xbc_ref, sm_ref, z_ref, cw_ref, cb_ref, dtb_ref, alog_ref, dsk_ref, nrm_ref,
                     yn_ref, hfin_ref, xpad_ref, st_ref):
    c = pl.program_id(1)
    cl = SSD_CHUNK

    @pl.when(c == 0)
    def _():
        xpad_ref[0:8, :] = jnp.zeros((8, CONV_DIM), F32)
        st_ref[...] = jnp.zeros_like(st_ref)

    xpad_ref[8:8 + cl, :] = xbc_ref[...]
    conv = cb_ref[...]
    for k in range(CONV_WIDTH):
        conv = conv + xpad_ref[5 + k:5 + k + cl, :] * cw_ref[k:k + 1, :]
    xpad_ref[0:8, :] = xpad_ref[cl:cl + 8, :]
    u = conv * _sigmoid(conv)
    xs = u[:, :D_INNER]
    bb = u[:, D_INNER:D_INNER + SSM_GROUPS * D_STATE].astype(BF16)
    cb = u[:, D_INNER + SSM_GROUPS * D_STATE:].astype(BF16)

    row = lax.broadcasted_iota(I32, (cl, cl), 0)
    col = lax.broadcasted_iota(I32, (cl, cl), 1)
    dt_ok = (col >= DT_LANE) & (col < DT_LANE + SSM_HEADS)
    dt = jnp.where(dt_ok, _softplus(sm_ref[...] + dtb_ref[...]), 0.0)
    a = dt * (-jnp.exp(alog_ref[...]))
    tri = (row >= col).astype(F32)
    a_cs = jnp.dot(tri, a, precision=_HI, preferred_element_type=F32)
    a_cs_t = a_cs.T
    ea = jnp.exp(a_cs)
    dte = jnp.exp(a_cs[cl - 1:cl, :] - a_cs)
    ex = _head_expand()
    dt_e = jnp.dot(dt, ex, precision=_HI, preferred_element_type=F32)
    ea_e = jnp.dot(ea, ex, precision=_HI, preferred_element_type=F32)
    dte_e = jnp.dot(dte, ex, precision=_HI, preferred_element_type=F32)

    xdt = xs * dt_e
    xdt_b = xdt.astype(BF16)
    causal = row >= col
    ys = []
    for g in range(SSM_GROUPS):
        cbm = _dot_nt(cb[:, g * D_STATE:(g + 1) * D_STATE], bb[:, g * D_STATE:(g + 1) * D_STATE])
        for e in range(SSM_HPG):
            h = g * SSM_HPG + e
            seg = a_cs[:, DT_LANE + h:DT_LANE + h + 1] - a_cs_t[DT_LANE + h:DT_LANE + h + 1, :]
            lm = jnp.exp(jnp.where(causal, seg, -jnp.inf))
            ys.append(_dot((cbm * lm).astype(BF16), xdt_b[:, h * SSM_HEAD_DIM:(h + 1) * SSM_HEAD_DIM]))
    y_diag = jnp.concatenate(ys, axis=1)

    xw = (xdt * dte_e).astype(BF16)
    dec = ea_e[cl - 1:cl, :]
    half = D_INNER // SSM_GROUPS
    yoffs = []
    for g in range(SSM_GROUPS):
        s_prev = st_ref[g]
        yoffs.append(_dot(cb[:, g * D_STATE:(g + 1) * D_STATE], s_prev.astype(BF16)))
        st_ref[g] = dec[:, g * half:(g + 1) * half] * s_prev + _dot_tn(
            bb[:, g * D_STATE:(g + 1) * D_STATE], xw[:, g * half:(g + 1) * half])
    y = y_diag + jnp.concatenate(yoffs, axis=1) * ea_e + dsk_ref[...] * xs
    zv = z_ref[...]
    y = y * (zv * _sigmoid(zv))
    outs = []
    for g in range(SSM_GROUPS):
        outs.append(_rms(y[:, g * half:(g + 1) * half], nrm_ref[:, g * half:(g + 1) * half]))
    yn_ref[...] = jnp.concatenate(outs, axis=1).astype(BF16)

    @pl.when(c == nc - 1)
    def _():
        for g in range(SSM_GROUPS):
            s_t = st_ref[g].T
            for e in range(SSM_HPG):
                hfin_ref[0, g, e] = s_t[e * SSM_HEAD_DIM:(e + 1) * SSM_HEAD_DIM, :]


def _ssd_prompt(xbc, sm2, z, lw, nb, seq):
    nc = seq // SSD_CHUNK
    cl = SSD_CHUNK
    rowmap = lambda b, c: (b * nc + c, 0)
    full = lambda b, c: (0, 0)
    return pl.pallas_call(
        functools.partial(_ssd_prompt_body, nc),
        grid=(nb, nc),
        in_specs=[pl.BlockSpec((cl, CONV_DIM), rowmap),
                  pl.BlockSpec((cl, LANES), rowmap),
                  pl.BlockSpec((cl, D_INNER), rowmap),
                  pl.BlockSpec((CONV_WIDTH, CONV_DIM), full),
                  pl.BlockSpec((1, CONV_DIM), full),
                  pl.BlockSpec((1, LANES), full),
                  pl.BlockSpec((1, LANES), full),
                  pl.BlockSpec((1, D_INNER), full),
                  pl.BlockSpec((1, D_INNER), full)],
        out_specs=[pl.BlockSpec((cl, D_INNER), rowmap),
                   pl.BlockSpec((1, SSM_GROUPS, SSM_HPG, SSM_HEAD_DIM, D_STATE), lambda b, c: (b, 0, 0, 0, 0))],
        out_shape=[jax.ShapeDtypeStruct((nb * seq, D_INNER), BF16),
                   jax.ShapeDtypeStruct((nb, SSM_GROUPS, SSM_HPG, SSM_HEAD_DIM, D_STATE), F32)],
        scratch_shapes=[pltpu.VMEM((cl + 8, CONV_DIM), F32),
                        pltpu.VMEM((SSM_GROUPS, D_STATE, D_INNER // SSM_GROUPS), F32)],
        compiler_params=_cparams("parallel", "arbitrary"),
        name="ssd_prompt",
    )(xbc, sm2, z, lw["conv_w"], lw["conv_b"], lw["dt_bias_p"], lw["a_log_p"], lw["d_skip_e"], lw["ssm_norm"])


def _ssd_step_pre_body(xbc_ref, prev_ref, sm_ref, cw_ref, cb_ref, dtb_ref, u_ref, dt_ref):
    conv = cb_ref[...]
    for k in range(CONV_WIDTH - 1):
        conv = conv + prev_ref[k] * cw_ref[k:k + 1, :]
    conv = conv + xbc_ref[...] * cw_ref[CONV_WIDTH - 1:CONV_WIDTH, :]
    u_ref[...] = conv * _sigmoid(conv)
    dt_ref[...] = _softplus(sm_ref[...] + dtb_ref[...])


def _ssd_step_pre(xbc, prev_t, sm2, lw):
    nb = xbc.shape[0]
    return pl.pallas_call(
        _ssd_step_pre_body,
        out_shape=[jax.ShapeDtypeStruct((nb, CONV_DIM), F32), jax.ShapeDtypeStruct((nb, LANES), F32)],
        compiler_params=pltpu.CompilerParams(vmem_limit_bytes=VMEM_LIMIT),
        name="ssd_step_pre",
    )(xbc, prev_t, sm2, lw["conv_w"], lw["conv_b"], lw["dt_bias_p"])


def _ssd_step_body(h0_ref, xt_ref, zt_ref, b_ref, c_ref, dt_ref, alog_ref, dsk_ref, nrm_ref, hn_ref, yt_ref):
    xt = xt_ref[0]
    dtv = dt_ref[0]
    av = -jnp.exp(alog_ref[...])
    lane = lax.broadcasted_iota(I32, (SSM_HEAD_DIM, SSM_HEADS), 1)
    y = jnp.zeros((SSM_HEAD_DIM, SSM_HEADS), F32)
    for g in range(SSM_GROUPS):
        bg = b_ref[0][:, g * D_STATE:(g + 1) * D_STATE]
        cg = c_ref[0][:, g * D_STATE:(g + 1) * D_STATE]
        for e in range(SSM_HPG):
            h = g * SSM_HPG + e
            dth = dtv[:, DT_LANE + h:DT_LANE + h + 1]
            dec = jnp.exp(dth * av[:, DT_LANE + h:DT_LANE + h + 1])
            hn = dec * h0_ref[0, g, e] + (xt[:, h:h + 1] * dth) * bg
            hn_ref[0, g, e] = hn
            ycol = jnp.sum(hn * cg, axis=1, keepdims=True)
            y = jnp.where(lane == h, ycol, y)
    y = y + dsk_ref[...] * xt
    zv = zt_ref[0]
    y = y * (zv * _sigmoid(zv))
    sq = y * y
    in_g0 = lane < SSM_HPG
    n_g = float(D_INNER // SSM_GROUPS)
    ms0 = jnp.sum(jnp.sum(jnp.where(in_g0, sq, 0.0), axis=1, keepdims=True), axis=0, keepdims=True) / n_g
    ms1 = jnp.sum(jnp.sum(jnp.where(in_g0, 0.0, sq), axis=1, keepdims=True), axis=0, keepdims=True) / n_g
    scale = jnp.where(in_g0, lax.rsqrt(ms0 + EPS), lax.rsqrt(ms1 + EPS))
    yt_ref[0] = (y * scale) * nrm_ref[...]


def _ssd_step(h0_all, layer, xt, zt, bm, cm, dt, lw):
    nb = xt.shape[0]
    st_block = (None, 1, SSM_GROUPS, SSM_HPG, SSM_HEAD_DIM, D_STATE)
    pt_block = (1, SSM_HEAD_DIM, SSM_HEADS)
    vec = lambda n: pl.BlockSpec((1, 1, n), lambda b: (b, 0, 0))
    full = lambda r, n: pl.BlockSpec((r, n), lambda b: (0, 0))
    return pl.pallas_call(
        _ssd_step_body,
        grid=(nb,),
        in_specs=[pl.BlockSpec(st_block, lambda b: (layer, b, 0, 0, 0, 0)),
                  pl.BlockSpec(pt_block, lambda b: (b, 0, 0)),
                  pl.BlockSpec(pt_block, lambda b: (b, 0, 0)),
                  vec(SSM_GROUPS * D_STATE), vec(SSM_GROUPS * D_STATE), vec(LANES),
                  full(1, LANES), full(1, SSM_HEADS), full(SSM_HEAD_DIM, SSM_HEADS)],
        out_specs=[pl.BlockSpec((1, SSM_GROUPS, SSM_HPG, SSM_HEAD_DIM, D_STATE), lambda b: (b, 0, 0, 0, 0)),
                   pl.BlockSpec(pt_block, lambda b: (b, 0, 0))],
        out_shape=[jax.ShapeDtypeStruct((nb, SSM_GROUPS, SSM_HPG, SSM_HEAD_DIM, D_STATE), F32),
                   jax.ShapeDtypeStruct((nb, SSM_HEAD_DIM, SSM_HEADS), F32)],
        compiler_params=_cparams("parallel"),
        name="ssd_step",
    )(h0_all, xt, zt, bm, cm, dt, lw["a_log_p"], lw["d_skip_row"], lw["ssm_norm_t"])


def _mla_prep_body(cq_ref, ckv_ref, sm1_ref, sm2_ref, qtab_ref, ktab_ref, qn_ref, kvn_ref,
                   wn_ref, wpe_ref, wpes_ref, wkb_ref, qcat_ref, kcat_ref, lat_ref, kpe_ref):
    hq = _rms(cq_ref[...], qn_ref[...]).astype(BF16)
    qn = _dot(hq, wn_ref[...]).astype(BF16)
    npe = MLA_HEADS * LANES
    qpe = (_dot(hq, wpe_ref[...]) * qtab_ref[:, :npe] + _dot(hq, wpes_ref[...]) * qtab_ref[:, npe:]) * ATTN_SCALE
    for h in range(MLA_HEADS):
        ql = _dot(qn[:, h * QK_NOPE:(h + 1) * QK_NOPE], wkb_ref[h]) * ATTN_SCALE
        qcat_ref[0, h] = jnp.concatenate([ql, qpe[:, h * LANES:(h + 1) * LANES]], axis=1).astype(BF16)
    lat = _rms(ckv_ref[...], kvn_ref[...])
    lat_ref[...] = lat
    kpe = sm1_ref[...] * ktab_ref[:, :LANES] + sm2_ref[...] * ktab_ref[:, LANES:]
    kpe_ref[...] = kpe[:, :QK_ROPE]
    kcat_ref[...] = jnp.concatenate([lat, kpe], axis=1).astype(BF16)


def _mla_prep(cq, ckv, sm1, sm2, qtab, ktab, lw, nb, seq):
    t = nb * seq
    tm = _row_block(seq, 256)
    nl = seq // tm
    rowmap = lambda b, i: (b * nl + i, 0)
    tabmap = lambda b, i: (i, 0)
    full2 = lambda b, i: (0, 0)
    npe = MLA_HEADS * LANES
    return pl.pallas_call(
        _mla_prep_body,
        grid=(nb, nl),
        in_specs=[pl.BlockSpec((tm, Q_LORA), rowmap),
                  pl.BlockSpec((tm, KV_LORA), rowmap),
                  pl.BlockSpec((tm, LANES), rowmap),
                  pl.BlockSpec((tm, LANES), rowmap),
                  pl.BlockSpec((tm, 2 * npe), tabmap),
                  pl.BlockSpec((tm, 2 * LANES), tabmap),
                  pl.BlockSpec((1, Q_LORA), full2),
                  pl.BlockSpec((1, KV_LORA), full2),
                  pl.BlockSpec((Q_LORA, MLA_HEADS * QK_NOPE), full2),
                  pl.BlockSpec((Q_LORA, npe), full2),
                  pl.BlockSpec((Q_LORA, npe), full2),
                  pl.BlockSpec((MLA_HEADS, QK_NOPE, KV_LORA), lambda b, i: (0, 0, 0))],
        out_specs=[pl.BlockSpec((1, MLA_HEADS, tm, QK_CAT), lambda b, i: (b, 0, i, 0)),
                   pl.BlockSpec((tm, QK_CAT), rowmap),
                   pl.BlockSpec((tm, KV_LORA), rowmap),
                   pl.BlockSpec((tm, QK_ROPE), rowmap)],
        out_shape=[jax.ShapeDtypeStruct((nb, MLA_HEADS, seq, QK_CAT), BF16),
                   jax.ShapeDtypeStruct((t, QK_CAT), BF16),
                   jax.ShapeDtypeStruct((t, KV_LORA), F32),
                   jax.ShapeDtypeStruct((t, QK_ROPE), F32)],
        compiler_params=_cparams("parallel", "parallel"),
        name="mla_prep",
    )(cq, ckv, sm1, sm2, qtab, ktab, lw["q_norm"], lw["kv_norm"], lw["wq_nope"], lw["wq_pe"], lw["wq_pe_sw"],
      lw["w_kb_t"])


def _attn_prompt_body(tq, tk, nk, q_ref, k_ref, wvb_ref, o_ref, m_ref, l_ref, acc_ref):
    i = pl.program_id(1)
    j = pl.program_id(2)
    rows = MLA_HEADS * tq

    @pl.when(j == 0)
    def _():
        m_ref[...] = jnp.full(m_ref.shape, -jnp.inf, F32)
        l_ref[...] = jnp.zeros(l_ref.shape, F32)
        acc_ref[...] = jnp.zeros(acc_ref.shape, F32)

    @pl.when(j * tk <= i * tq + tq - 1)
    def _():
        k = k_ref[...]
        v = k[:, :KV_LORA]
        qpos = i * tq + lax.broadcasted_iota(I32, (tq, tk), 0)
        kpos = j * tk + lax.broadcasted_iota(I32, (tq, tk), 1)
        visible = kpos <= qpos
        ahead = 2
        s_q = [_dot_nt(q_ref[0, h], k) for h in range(ahead)]
        for h in range(MLA_HEADS):
            hs = slice(h * tq, (h + 1) * tq)
            s = jnp.where(visible, s_q.pop(0), -jnp.inf)
            if h + ahead < MLA_HEADS:
                s_q.append(_dot_nt(q_ref[0, h + ahead], k))
            m_prev = m_ref[hs]
            m_new = jnp.maximum(m_prev, jnp.max(s, axis=1, keepdims=True))
            p = jnp.exp(s - m_new)
            alpha = jnp.exp(m_prev - m_new)
            l_ref[hs] = alpha * l_ref[hs] + jnp.sum(p, axis=1, keepdims=True)
            acc_ref[hs] = alpha * acc_ref[hs] + _dot(p.astype(BF16), v)
            m_ref[hs] = m_new

    @pl.when(j == nk - 1)
    def _():
        outs = []
        for h in range(MLA_HEADS):
            hs = slice(h * tq, (h + 1) * tq)
            outs.append(_dot((acc_ref[hs] / l_ref[hs]).astype(BF16), wvb_ref[h]))
        o_ref[...] = jnp.concatenate(outs, axis=1).astype(BF16)


def _attn_prompt(qcat, kcat, w_vb, nb, seq):
    tq = _row_block(seq, 256)
    tk = _row_block(seq, 512)
    assert tq & (tq - 1) == 0
    nq, nk = seq // tq, seq // tk

    def kmap(b, i, j):
        return (b * nk + jnp.minimum(j, (i * tq + tq - 1) // tk), 0)

    return pl.pallas_call(
        functools.partial(_attn_prompt_body, tq, tk, nk),
        grid=(nb, nq, nk),
        in_specs=[pl.BlockSpec((1, MLA_HEADS, tq, QK_CAT), lambda b, i, j: (b, 0, i, 0)),
                  pl.BlockSpec((tk, QK_CAT), kmap),
                  pl.BlockSpec((MLA_HEADS, KV_LORA, V_HEAD), lambda b, i, j: (0, 0, 0))],
        out_specs=pl.BlockSpec((tq, MLA_HEADS * V_HEAD), lambda b, i, j: (b * nq + i, 0)),
        out_shape=jax.ShapeDtypeStruct((nb * seq, MLA_HEADS * V_HEAD), BF16),
        scratch_shapes=[pltpu.VMEM((MLA_HEADS * tq, 1), F32), pltpu.VMEM((MLA_HEADS * tq, 1), F32),
                        pltpu.VMEM((MLA_HEADS * tq, KV_LORA), F32)],
        compiler_params=_cparams("parallel", "parallel", "arbitrary"),
        name="attn_prompt",
    )(qcat, kcat, w_vb)


def _attn_decode_body(pg, nj, pt_ref, q_ref, kc_ref, wvb_ref, *refs):
    lat_refs = refs[:pg]
    kr_refs = refs[pg:2 * pg]
    o_ref, m_ref, l_ref, acc_ref = refs[2 * pg:]
    j = pl.program_id(1)

    @pl.when(j == 0)
    def _():
        m_ref[...] = jnp.full(m_ref.shape, -jnp.inf, F32)
        l_ref[...] = jnp.zeros(l_ref.shape, F32)
        acc_ref[...] = jnp.zeros(acc_ref.shape, F32)

    q = q_ref[0]
    ql = q[:, :KV_LORA]
    qr = q[:, KV_LORA:KV_LORA + QK_ROPE]
    kvs = [lat_refs[i][...].astype(BF16) for i in range(pg)]
    ss = [_dot_nt(ql, kvs[i]) + _dot_nt(qr, kr_refs[i][...].astype(BF16)) for i in range(pg)]
    m_prev = m_ref[...]
    m_blk = ss[0]
    for i in range(1, pg):
        m_blk = jnp.maximum(m_blk, ss[i])
    m_new = jnp.maximum(m_prev, jnp.max(m_blk, axis=1, keepdims=True))
    ps = [jnp.exp(ss[i] - m_new) for i in range(pg)]
    p_sum = ps[0]
    for i in range(1, pg):
        p_sum = p_sum + ps[i]
    pvs = [_dot(ps[i].astype(BF16), kvs[i]) for i in range(pg)]
    while len(pvs) > 1:
        pvs = [pvs[i] + pvs[i + 1] for i in range(0, len(pvs), 2)]
    alpha = jnp.exp(m_prev - m_new)
    l_ref[...] = alpha * l_ref[...] + jnp.sum(p_sum, axis=1, keepdims=True)
    acc_ref[...] = alpha * acc_ref[...] + pvs[0]
    m_ref[...] = m_new

    @pl.when(j == nj - 1)
    def _():
        kc = kc_ref[0].astype(F32)
        s = jnp.sum(q.astype(F32) * kc, axis=1, keepdims=True)
        m_prev = m_ref[...]
        m_new = jnp.maximum(m_prev, s)
        p = jnp.exp(s - m_new)
        alpha = jnp.exp(m_prev - m_new)
        l_fin = alpha * l_ref[...] + p
        acc = alpha * acc_ref[...] + p.astype(BF16).astype(F32) * kc[:, :KV_LORA]
        o = (acc / l_fin).astype(BF16)
        full = _dot(o, wvb_ref[...])
        hrow = lax.broadcasted_iota(I32, full.shape, 0)
        hcol = jnp.right_shift(lax.broadcasted_iota(I32, full.shape, 1), 6)
        o_ref[0] = jnp.sum(jnp.where(hrow == hcol, full, 0.0), axis=0, keepdims=True).astype(BF16)


def _attn_decode(page_table, qdec, kcur, w_vb_all, cache_latent, cache_krope, layer):
    nb, n_pages = page_table.shape
    page = cache_latent.shape[2]
    pg = 16
    while n_pages % pg:
        pg //= 2
    nj = n_pages // pg

    def pmap(i):
        return lambda b, j, pt: (layer, pt[b, j * pg + i], 0, 0)

    in_specs = [pl.BlockSpec((1, MLA_HEADS, QK_CAT), lambda b, j, pt: (b, 0, 0)),
                pl.BlockSpec((1, 1, QK_CAT), lambda b, j, pt: (b, 0, 0)),
                pl.BlockSpec((KV_LORA, MLA_HEADS * V_HEAD), lambda b, j, pt: (0, 0))]
    in_specs += [pl.BlockSpec((None, None, page, KV_LORA), pmap(i)) for i in range(pg)]
    in_specs += [pl.BlockSpec((None, None, page, QK_ROPE), pmap(i)) for i in range(pg)]
    grid_spec = pltpu.PrefetchScalarGridSpec(
        num_scalar_prefetch=1,
        grid=(nb, nj),
        in_specs=in_specs,
        out_specs=pl.BlockSpec((1, 1, MLA_HEADS * V_HEAD), lambda b, j, pt: (b, 0, 0)),
        scratch_shapes=[pltpu.VMEM((MLA_HEADS, 1), F32), pltpu.VMEM((MLA_HEADS, 1), F32),
                        pltpu.VMEM((MLA_HEADS, KV_LORA), F32)])
    return pl.pallas_call(
        functools.partial(_attn_decode_body, pg, nj),
        grid_spec=grid_spec,
        out_shape=jax.ShapeDtypeStruct((nb, 1, MLA_HEADS * V_HEAD), BF16),
        compiler_params=_cparams("parallel", "arbitrary"),
        name="attn_decode",
    )(page_table, qdec, kcur, w_vb_all, *([cache_latent] * pg), *([cache_krope] * pg))


def _mix_body(x_ref, g_ref, bg_ref, yn_ref, o_ref, wssm_ref, wmla_ref, wout_ref, x1_ref):
    ya = _dot(yn_ref[...], wssm_ref[...])
    yb = _dot(o_ref[...], wmla_ref[...])
    gate = _sigmoid(g_ref[...] + bg_ref[...])
    mix = gate[:, :D_MODEL] * ya + gate[:, D_MODEL:] * yb
    x1_ref[...] = x_ref[...] + _dot(mix.astype(BF16), wout_ref[...])


def _mix(x2d, gates, yn, o, lw):
    t = x2d.shape[0]
    tm = _row_block(t, 256)
    row = lambda n: pl.BlockSpec((tm, n), lambda i: (i, 0))
    full = lambda r, n: pl.BlockSpec((r, n), lambda i: (0, 0))
    return pl.pallas_call(
        _mix_body,
        grid=(t // tm,),
        in_specs=[row(D_MODEL), row(2 * D_MODEL), full(1, 2 * D_MODEL), row(D_INNER), row(MLA_HEADS * V_HEAD),
                  full(D_INNER, D_MODEL), full(MLA_HEADS * V_HEAD, D_MODEL), full(D_MODEL, D_MODEL)],
        out_specs=row(D_MODEL),
        out_shape=jax.ShapeDtypeStruct((t, D_MODEL), F32),
        compiler_params=_cparams("parallel"),
        name="mix",
    )(x2d, gates, lw["b_gate"], yn, o, lw["w_ssm_proj"], lw["w_mla_proj"], lw["w_out"])


def _peer_scores_body(x_ref, nw_ref, wq_ref, keys_ref, hnt_ref, s1_ref, s2_ref):
    hn = _rms(x_ref[...], nw_ref[...])
    hnt_ref[...] = hn.T.astype(BF16)
    q = _dot(hn.astype(BF16), wq_ref[...]).astype(BF16)
    half = PEER_D_KEY // 2
    for h in range(PEER_HEADS):
        s1_ref[h] = _dot_nt(keys_ref[h, 0], q[:, h * PEER_D_KEY:h * PEER_D_KEY + half])
        s2_ref[h] = _dot_nt(keys_ref[h, 1], q[:, h * PEER_D_KEY + half:(h + 1) * PEER_D_KEY])


def _peer_scores(x1, lw):
    t = x1.shape[0]
    tm = _row_block(t, 256)
    sblock = pl.BlockSpec((PEER_HEADS, N_KEYS, tm), lambda i: (0, 0, i))
    return pl.pallas_call(
        _peer_scores_body,
        grid=(t // tm,),
        in_specs=[pl.BlockSpec((tm, D_MODEL), lambda i: (i, 0)),
                  pl.BlockSpec((1, D_MODEL), lambda i: (0, 0)),
                  pl.BlockSpec((D_MODEL, PEER_HEADS * PEER_D_KEY), lambda i: (0, 0)),
                  pl.BlockSpec((PEER_HEADS, 2, N_KEYS, PEER_D_KEY // 2), lambda i: (0, 0, 0, 0))],
        out_specs=[pl.BlockSpec((D_MODEL, tm), lambda i: (0, i)), sblock, sblock],
        out_shape=[jax.ShapeDtypeStruct((D_MODEL, t), BF16),
                   jax.ShapeDtypeStruct((PEER_HEADS, N_KEYS, t), F32),
                   jax.ShapeDtypeStruct((PEER_HEADS, N_KEYS, t), F32)],
        compiler_params=_cparams("parallel"),
        name="peer_scores",
    )(x1, lw["norm_ffn"], lw["peer_wq"], lw["peer_keys"])


def _top16_rows(s):
    n = s.shape[0]
    row = lax.broadcasted_iota(I32, s.shape, 0)
    r16 = lax.broadcasted_iota(I32, (PEER_TOPK, LANES), 0)

    def body(r, carry):
        rank, vals = carry
        alive = rank == PEER_TOPK
        m = jnp.max(jnp.where(alive, s, -jnp.inf), axis=0, keepdims=True)
        idx = jnp.min(jnp.where(alive & (s == m), row, n), axis=0, keepdims=True)
        rank = jnp.where(row == idx, r, rank)
        vals = jnp.where(r16 == r, m, vals)
        return rank, vals

    return lax.fori_loop(0, PEER_TOPK, body,
                         (jnp.full(s.shape, PEER_TOPK, I32), jnp.zeros((PEER_TOPK, LANES), F32)))


def _batcher_net(n):
    def merge(lo, hi, r):
        step = r * 2
        if step < hi - lo:
            yield from merge(lo, hi, step)
            yield from merge(lo + r, hi, step)
            yield from [(i, i + r) for i in range(lo + r, hi - r, step)]
        else:
            yield (lo, lo + r)

    def sort(lo, hi):
        if hi - lo >= 1:
            mid = lo + (hi - lo) // 2
            yield from sort(lo, mid)
            yield from sort(mid + 1, hi)
            yield from merge(lo, hi, 1)

    return tuple(sort(0, n - 1))


_NET16 = _batcher_net(PEER_TOPK)
SUBLANES = 8


def _cmpx(v, i, j):
    a, b = v[i], v[j]
    if b is None:
        return
    if a is None:
        v[i], v[j] = b, None
        return
    v[i], v[j] = jnp.maximum(a, b), jnp.minimum(a, b)


def _top16_sorted(groups):
    v = list(groups) + [None] * (PEER_TOPK - len(groups))
    for i, j in _NET16:
        _cmpx(v, i, j)
    for shift in (4, 2, 1):
        b = [None if x is None else pltpu.roll(x, shift, 0) for x in v]
        c = []
        for i in range(PEER_TOPK):
            x, y = v[i], b[PEER_TOPK - 1 - i]
            c.append(y if x is None else x if y is None else jnp.maximum(x, y))
        for d in (8, 4, 2, 1):
            for i in range(PEER_TOPK):
                if not i & d:
                    _cmpx(c, i, i + d)
        v = c
    return v


def _colsum(x):
    x = x + pltpu.roll(x, 4, 0)
    x = x + pltpu.roll(x, 2, 0)
    return x + pltpu.roll(x, 1, 0)


def _gates_by_value(s1, s2):
    k = PEER_TOPK
    ng = N_KEYS // SUBLANES
    sub = lax.broadcasted_iota(I32, (SUBLANES, LANES), 0)
    c1 = [s1[SUBLANES * v:SUBLANES * (v + 1)] for v in range(ng)]
    c2 = [s2[SUBLANES * v:SUBLANES * (v + 1)] for v in range(ng)]
    t1 = _top16_sorted(c1)
    t2 = _top16_sorted(c2)

    def as_rows(t, base):
        out = t[base]
        for r in range(1, SUBLANES):
            out = jnp.where(sub == r, t[base + r], out)
        return out

    sc2_lo, sc2_hi, sc1_hi = as_rows(t2, 0), as_rows(t2, SUBLANES), as_rows(t1, SUBLANES)
    cand = [t1[0] + sc2_lo, t1[0] + sc2_hi] + [t1[r] + sc2_lo for r in range(1, SUBLANES)] + [sc1_hi + t2[0]]
    tau = _top16_sorted(cand)[k - 1]
    sel = [c >= tau for c in cand]
    one = lambda m: jnp.where(m, 1.0, 0.0)
    n_r = [_colsum(one(sel[0]) + one(sel[1]))] + [_colsum(one(sel[1 + r])) for r in range(1, SUBLANES)]
    n_hi = one(sel[9])
    n_r += [_colsum(jnp.where(sub == q, n_hi, 0.0)) for q in range(SUBLANES)]
    total = n_r[0]
    for r in range(1, k):
        total = total + n_r[r]
    cmax = t1[0] + t2[0]
    zsum = jnp.where(sel[0], jnp.exp(cand[0] - cmax), 0.0)
    for g in range(1, len(cand)):
        zsum = zsum + jnp.where(sel[g], jnp.exp(cand[g] - cmax), 0.0)
    z = _colsum(zsum)
    scale2 = 0.5 / z
    bad = total != float(k)
    for r in range(k - 1):
        bad = bad | (t1[r] == t1[r + 1]) | (t2[r] == t2[r + 1])
    in1 = [c >= t1[k - 1] for c in c1]
    in2 = [c >= t2[k - 1] for c in c2]
    n_in1, n_in2 = one(in1[0]), one(in2[0])
    for v in range(1, ng):
        n_in1, n_in2 = n_in1 + one(in1[v]), n_in2 + one(in2[v])
    bad = bad | (_colsum(n_in1) != float(k)) | (_colsum(n_in2) != float(k))
    key2, e2, thr, e1 = [], [], [], []
    for v in range(ng):
        kv = jnp.zeros((SUBLANES, LANES), F32)
        tv = jnp.full((SUBLANES, LANES), float(k + 1), F32)
        for r in range(k - 1, -1, -1):
            kv = jnp.where(c2[v] >= t2[r], float(k - r), kv)
            tv = jnp.where(c1[v] >= t1[r], float(k + 1) - n_r[r], tv)
        key2.append(kv)
        thr.append(tv)
        e1.append(jnp.where(in1[v], jnp.exp(c1[v] - t1[0]), 0.0))
        e2.append(jnp.where(in2[v], jnp.exp(c2[v] - t2[0]) * scale2, 0.0))
    return key2, e2, thr, e1, one(bad)


def _gates_by_index(s1, s2):
    k = PEER_TOPK
    r8 = lax.broadcasted_iota(I32, (SUBLANES, LANES), 0)
    rank1, sc1 = _top16_rows(s1)
    rank2, sc2 = _top16_rows(s2)
    pieces = [sc1[0:1] + sc2]
    pieces += [sc1[r:r + 1] + sc2[0:8] for r in range(1, 8)]
    pieces += [sc1[8:16] + sc2[0:1]]
    cand = jnp.concatenate(pieces, axis=0)
    rankc, _ = _top16_rows(cand)
    sel = rankc < k
    seli = sel.astype(I32)
    n_lo = jnp.zeros((8, LANES), I32)
    n_lo = jnp.where(r8 == 0, jnp.sum(seli[0:16], axis=0, keepdims=True), n_lo)
    for r in range(1, 8):
        n_lo = jnp.where(r8 == r, jnp.sum(seli[8 + 8 * r:16 + 8 * r], axis=0, keepdims=True), n_lo)
    ncnt = jnp.concatenate([n_lo, seli[72:80]], axis=0)
    z = jnp.sum(jnp.where(sel, jnp.exp(cand - cand[0:1]), 0.0), axis=0, keepdims=True)
    cnt = jnp.zeros((N_KEYS, LANES), I32)
    for r in range(k):
        cnt = jnp.where(rank1 == r, ncnt[r:r + 1], cnt)
    key2 = (k - rank2).astype(F32)
    thr = (k + 1 - cnt).astype(F32)
    e1 = jnp.where(rank1 < k, jnp.exp(s1 - sc1[0:1]), 0.0)
    e2 = jnp.where(rank2 < k, jnp.exp(s2 - sc2[0:1]) * (0.5 / z), 0.0)
    return key2, e2, thr, e1


def _peer_gates_body(s1_ref, s2_ref, k2_ref, e2_ref, thr_ref, e1_ref):
    def head(h, _):
        s1 = s1_ref[h]
        s2 = s2_ref[h]
        key2, e2, thr, e1, bad = _gates_by_value(s1, s2)
        k2_ref[h] = jnp.concatenate(key2, axis=0)
        e2_ref[h] = jnp.concatenate(e2, axis=0)
        thr_ref[h] = jnp.concatenate(thr, axis=0)
        e1_ref[h] = jnp.concatenate(e1, axis=0)

        @pl.when(jnp.max(bad) > 0.0)
        def _():
            k2_ref[h], e2_ref[h], thr_ref[h], e1_ref[h] = _gates_by_index(s1, s2)

        return 0

    lax.fori_loop(0, PEER_HEADS, head, 0)


def _peer_gates(s1t, s2t):
    t = s1t.shape[2]
    blk = pl.BlockSpec((PEER_HEADS, N_KEYS, LANES), lambda i: (0, 0, i))
    shp = jax.ShapeDtypeStruct((PEER_HEADS, N_KEYS, t), F32)
    return pl.pallas_call(
        _peer_gates_body,
        grid=(t // LANES,),
        in_specs=[blk, blk],
        out_specs=[blk, blk, blk, blk],
        out_shape=[shp, shp, shp, shp],
        compiler_params=_cparams("parallel"),
        name="peer_gates",
    )(s1t, s2t)


_SQRT_HALF = 0.7071067811865476


def _peer_experts_body(tm, eb, nj, final, hnt_ref, u_ref, vt_ref, k2_ref, e2_ref, thr_ref, e1_ref, x_ref, nf_ref,
                       o_ref, acc_ref, at_ref, wt_ref):
    j = pl.program_id(1)
    tiles = eb // N_KEYS
    pair = 2 * N_KEYS
    npairs = eb // pair
    new = lax.rem(j, 2)
    old = 1 - new

    def scores(p):
        rows = slice(p * pair, (p + 1) * pair)
        at_ref[new, rows, :] = _dot(u_ref[rows, :], hnt_ref[...])

    def weights(p):
        a0 = pl.multiple_of((j - 1) * tiles, tiles)
        rg = 2 * SUBLANES
        for s in range(tm // LANES):
            ls = slice(s * LANES, (s + 1) * LANES)
            thr_b, e1_b = {}, {}
            for h in range(PEER_HEADS):
                thr8 = thr_ref[h, pl.ds(a0, tiles), ls]
                e18 = e1_ref[h, pl.ds(a0, tiles), ls]
                for al in (2 * p, 2 * p + 1):
                    thr_b[h, al] = jnp.broadcast_to(thr8[al:al + 1], (rg, LANES))
                    e1_b[h, al] = jnp.broadcast_to(e18[al:al + 1], (rg, LANES))
            for v in range(N_KEYS // rg):
                b = slice(v * rg, (v + 1) * rg)
                g = {al: jnp.zeros((rg, LANES), F32) for al in (2 * p, 2 * p + 1)}
                for h in range(PEER_HEADS):
                    k2 = k2_ref[h, b, ls]
                    e2 = e2_ref[h, b, ls]
                    for al in (2 * p, 2 * p + 1):
                        g[al] = g[al] + jnp.where(k2 >= thr_b[h, al], e2, 0.0) * e1_b[h, al]
                for al in (2 * p, 2 * p + 1):
                    rows = slice(al * N_KEYS + v * rg, al * N_KEYS + (v + 1) * rg)
                    act = at_ref[old, rows, ls]
                    wt_ref[rows, ls] = (g[al] * (act * (1.0 + lax.erf(act * _SQRT_HALF)))).astype(BF16)

    def values(p):
        rows = slice(p * pair, (p + 1) * pair)
        acc_ref[...] += _dot(vt_ref[:, rows], wt_ref[rows, :])

    @pl.when(j == 0)
    def _():
        acc_ref[...] = jnp.zeros(acc_ref.shape, F32)
        for p in range(npairs):
            scores(p)

    @pl.when((j > 0) & (j < nj))
    def _():
        scores(0)
        for p in range(npairs):
            weights(p)
            if p + 1 < npairs:
                scores(p + 1)
            values(p)

    @pl.when(j == nj)
    def _():
        for p in range(npairs):
            weights(p)
            values(p)
        x2 = x_ref[...] + acc_ref[...].T
        if final:
            o_ref[...] = _rms(x2, nf_ref[...])
        else:
            o_ref[...] = x2


def _peer_experts(hnt, gates, x1, lw, norm_final, final):
    t = x1.shape[0]
    tm = _row_block(t, 512)
    eb = 8 * N_KEYS
    nj = N_EXPERTS // eb
    k2, e2, thr, e1 = gates
    gblk = pl.BlockSpec((PEER_HEADS, N_KEYS, tm), lambda i, j: (0, 0, i))
    return pl.pallas_call(
        functools.partial(_peer_experts_body, tm, eb, nj, final),
        grid=(t // tm, nj + 1),
        in_specs=[pl.BlockSpec((D_MODEL, tm), lambda i, j: (0, i)),
                  pl.BlockSpec((eb, D_MODEL), lambda i, j: (jnp.minimum(j, nj - 1), 0)),
                  pl.BlockSpec((D_MODEL, eb), lambda i, j: (0, jnp.maximum(j - 1, 0))),
                  gblk, gblk, gblk, gblk,
                  pl.BlockSpec((tm, D_MODEL), lambda i, j: (i, 0)),
                  pl.BlockSpec((1, D_MODEL), lambda i, j: (0, 0))],
        out_specs=pl.BlockSpec((tm, D_MODEL), lambda i, j: (i, 0)),
        out_shape=jax.ShapeDtypeStruct((t, D_MODEL), F32),
        scratch_shapes=[pltpu.VMEM((D_MODEL, tm), F32), pltpu.VMEM((2, eb, tm), F32), pltpu.VMEM((eb, tm), BF16)],
        compiler_params=_cparams("parallel", "arbitrary"),
        name="peer_experts",
    )(hnt, lw["peer_u"], lw["peer_v_t"], k2, e2, thr, e1, x1, norm_final)


def _peer(x1, lw, norm_final, final):
    hnt, s1t, s2t = _peer_scores(x1, lw)
    gates = _peer_gates(s1t, s2t)
    return _peer_experts(hnt, gates, x1, lw, norm_final, final)


def _pad_lanes(v, start, total=LANES):
    return jnp.zeros((1, total), F32).at[0, start:start + v.shape[0]].set(v)


def _layer_weights(l, norm_mix, w_in, b_gate, conv_w, conv_b, dt_bias, a_log, d_skip, ssm_norm, w_ssm_proj,
                   q_norm, w_qb, kv_norm, w_kvb, w_mla_proj, w_out, norm_ffn, peer_wq, peer_keys, peer_u, peer_v):
    o_dt = 3 * D_MODEL + CONV_DIM
    o_cq = o_dt + SSM_HEADS
    o_ckv = o_cq + Q_LORA
    o_kr = o_ckv + KV_LORA
    w = w_in[l]
    half = QK_ROPE // 2
    zcols = lambda n: jnp.zeros((D_MODEL, n), F32)
    kr = w[:, o_kr:o_kr + QK_ROPE]
    kr_sw = jnp.concatenate([kr[:, half:], kr[:, :half]], axis=1)
    sm1 = jnp.concatenate([kr, zcols(LANES - QK_ROPE)], axis=1)
    sm2 = jnp.concatenate([kr_sw, zcols(DT_LANE - QK_ROPE), w[:, o_dt:o_cq], zcols(LANES - DT_LANE - SSM_HEADS)], axis=1)
    wcat = jnp.concatenate([w[:, :o_dt], w[:, o_cq:o_kr], sm1, sm2], axis=1).astype(BF16)

    wq = w_qb[l].reshape(Q_LORA, MLA_HEADS, QK_NOPE + QK_ROPE)
    wq_nope = wq[:, :, :QK_NOPE].reshape(Q_LORA, MLA_HEADS * QK_NOPE)
    pe = wq[:, :, QK_NOPE:]
    pe_sw = jnp.concatenate([pe[:, :, half:], pe[:, :, :half]], axis=2)
    padpe = lambda p: jnp.pad(p, ((0, 0), (0, 0), (0, LANES - QK_ROPE))).reshape(Q_LORA, MLA_HEADS * LANES)
    wkv = w_kvb[l].reshape(KV_LORA, MLA_HEADS, QK_NOPE + V_HEAD)
    w_kb_t = jnp.transpose(wkv[:, :, :QK_NOPE], (1, 2, 0))
    w_vb = jnp.transpose(wkv[:, :, QK_NOPE:], (1, 0, 2))
    w_vb_all = wkv[:, :, QK_NOPE:].reshape(KV_LORA, MLA_HEADS * V_HEAD)
    row = lambda v: v.reshape(1, -1)
    return dict(
        norm_mix=row(norm_mix[l]), wcat=wcat, b_gate=row(b_gate[l]),
        conv_w=conv_w[l], conv_b=row(conv_b[l]),
        dt_bias_p=_pad_lanes(dt_bias[l], DT_LANE), a_log_p=_pad_lanes(a_log[l], DT_LANE),
        d_skip_e=row(jnp.repeat(d_skip[l], SSM_HEAD_DIM)), d_skip_row=row(d_skip[l]),
        ssm_norm=row(ssm_norm[l]), ssm_norm_t=ssm_norm[l].reshape(SSM_HEADS, SSM_HEAD_DIM).T,
        w_ssm_proj=w_ssm_proj[l].astype(BF16),
        q_norm=row(q_norm[l]), kv_norm=row(kv_norm[l]),
        wq_nope=wq_nope.astype(BF16), wq_pe=padpe(pe).astype(BF16), wq_pe_sw=padpe(pe_sw).astype(BF16),
        w_kb_t=w_kb_t.astype(BF16), w_vb=w_vb.astype(BF16), w_vb_all=w_vb_all.astype(BF16),
        w_mla_proj=w_mla_proj[l].astype(BF16), w_out=w_out[l].astype(BF16),
        norm_ffn=row(norm_ffn[l]), peer_wq=peer_wq[l].astype(BF16), peer_keys=peer_keys[l].astype(BF16),
        peer_u=peer_u[l].astype(BF16), peer_v_t=peer_v[l].T.astype(BF16))


def _rope_tables(pos):
    half = QK_ROPE // 2
    inv = ROPE_THETA ** (-jnp.arange(half, dtype=F32) / half)
    ang = pos.astype(F32)[:, None] * inv[None, :]
    cos, sin = jnp.cos(ang), jnp.sin(ang)
    pad = jnp.zeros((pos.shape[0], LANES - QK_ROPE), F32)
    c128 = jnp.concatenate([cos, cos, pad], axis=1)
    s128 = jnp.concatenate([-sin, sin, pad], axis=1)
    ktab = jnp.concatenate([c128, s128], axis=1)
    qtab = jnp.concatenate([jnp.tile(c128, (1, MLA_HEADS)), jnp.tile(s128, (1, MLA_HEADS))], axis=1)
    return qtab, ktab


def _prompt_layer(x2d, nb, seq, lw, tabs, norm_final, final):
    gates, z, xbc, cq, ckv, sm1, sm2 = _in_proj(x2d, lw["norm_mix"], lw["wcat"])
    yn, h_new = _ssd_prompt(xbc, sm2, z, lw, nb, seq)
    qcat, kcat, lat, kpe = _mla_prep(cq, ckv, sm1, sm2, tabs[0], tabs[1], lw, nb, seq)
    o = _attn_prompt(qcat, kcat, lw["w_vb"], nb, seq)
    x1 = _mix(x2d, gates, yn, o, lw)
    x2 = _peer(x1, lw, norm_final, final)
    conv_new = xbc.reshape(nb, seq, CONV_DIM)[:, seq - (CONV_WIDTH - 1):, :]
    return x2, h_new, conv_new, lat.reshape(nb, seq, KV_LORA), kpe.reshape(nb, seq, QK_ROPE)


def _sample_layer(x2d, layer, lw, tabs, state_ssm, state_conv, cache_latent, cache_krope, page_table,
                  norm_final, final):
    nb = x2d.shape[0]
    gates, z, xbc, cq, ckv, sm1, sm2 = _in_proj(x2d, lw["norm_mix"], lw["wcat"])
    prev = state_conv[layer]
    u, dt = _ssd_step_pre(xbc, jnp.transpose(prev, (1, 0, 2)), sm2, lw)
    to_t = lambda v: jnp.transpose(v.reshape(nb, SSM_HEADS, SSM_HEAD_DIM), (0, 2, 1))
    n_bc = SSM_GROUPS * D_STATE
    h_new, yt = _ssd_step(state_ssm, layer, to_t(u[:, :D_INNER]), to_t(z),
                          u[:, D_INNER:D_INNER + n_bc].reshape(nb, 1, n_bc),
                          u[:, D_INNER + n_bc:].reshape(nb, 1, n_bc), dt.reshape(nb, 1, LANES), lw)
    yn = jnp.transpose(yt, (0, 2, 1)).reshape(nb, D_INNER).astype(BF16)
    conv_new = jnp.concatenate([prev[:, 1:], xbc[:, None, :]], axis=1)

    qcat, kcat, lat, kpe = _mla_prep(cq, ckv, sm1, sm2, tabs[0], tabs[1], lw, 1, nb)
    qdec = jnp.transpose(qcat[0], (1, 0, 2))
    o = _attn_decode(page_table, qdec, kcat.reshape(nb, 1, QK_CAT), lw["w_vb_all"], cache_latent, cache_krope, layer)
    x1 = _mix(x2d, gates, yn, o.reshape(nb, MLA_HEADS * V_HEAD), lw)
    pad = (-nb) % LANES
    x2 = _peer(jnp.pad(x1, ((0, pad), (0, 0))), lw, norm_final, final)[:nb]
    return x2, h_new, conv_new, lat.reshape(nb, 1, KV_LORA), kpe.reshape(nb, 1, QK_ROPE)


def kernel(x_prompt, x_sample, cache_latent, cache_krope, state_ssm, state_conv, page_table, norm_mix, w_in, b_gate,
           conv_w, conv_b, dt_bias, a_log, d_skip, ssm_norm, w_ssm_proj, q_norm, w_qb, kv_norm, w_kvb, w_mla_proj,
           w_out, norm_ffn, peer_wq, peer_keys, peer_u, peer_v, norm_final):
    bp, sp = x_prompt.shape[:2]
    bs, ss = x_sample.shape[:2]
    depth = w_in.shape[0]
    assert ss == 1 and sp % SSD_CHUNK == 0
    past_len = page_table.shape[1] * cache_latent.shape[2]
    tabs_p = _rope_tables(jnp.arange(sp, dtype=I32))
    tabs_s = _rope_tables(jnp.full((bs,), past_len, I32))
    nf = norm_final.reshape(1, D_MODEL)
    xp = x_prompt.reshape(bp * sp, D_MODEL)
    xs = x_sample.reshape(bs, D_MODEL)
    outs_p, outs_s = [], []
    for l in range(depth):
        lw = _layer_weights(l, norm_mix, w_in, b_gate, conv_w, conv_b, dt_bias, a_log, d_skip, ssm_norm, w_ssm_proj,
                            q_norm, w_qb, kv_norm, w_kvb, w_mla_proj, w_out, norm_ffn, peer_wq, peer_keys, peer_u,
                            peer_v)
        final = l == depth - 1
        xp, hp, cp, lp, kp = _prompt_layer(xp, bp, sp, lw, tabs_p, nf, final)
        xs, hs, cs, lsn, ksn = _sample_layer(xs, l, lw, tabs_s, state_ssm, state_conv, cache_latent, cache_krope,
                                             page_table, nf, final)
        outs_p.append((lp, kp, hp, cp))
        outs_s.append((lsn, ksn, hs, cs))
    stack = lambda outs, i: jnp.stack([o[i] for o in outs])
    return (xp.reshape(bp, sp, D_MODEL), xs.reshape(bs, ss, D_MODEL),
            stack(outs_p, 0), stack(outs_p, 1), stack(outs_p, 2), stack(outs_p, 3),
            stack(outs_s, 0), stack(outs_s, 1), stack(outs_s, 2), stack(outs_s, 3))
```

```python
import functools

import numpy as np
import jax
import jax.numpy as jnp
from jax import lax
from jax.experimental import pallas as pl
from jax.experimental.pallas import tpu as pltpu

F32 = jnp.float32
BF16 = jnp.bfloat16
I32 = jnp.int32

D_MODEL = 1024
D_INNER = 1024
SSM_HEAD_DIM = 64
SSM_HEADS = 16
SSM_GROUPS = 2
SSM_HPG = 8
D_STATE = 128
CONV_WIDTH = 4
CONV_DIM = D_INNER + 2 * SSM_GROUPS * D_STATE
SSD_CHUNK = 128
MLA_HEADS = 8
QK_NOPE = 64
QK_ROPE = 32
V_HEAD = 64
Q_LORA = 384
KV_LORA = 256
ROPE_THETA = 10000.0
ATTN_SCALE = (QK_NOPE + QK_ROPE) ** -0.5
PEER_HEADS = 8
N_KEYS = 128
N_EXPERTS = N_KEYS * N_KEYS
PEER_D_KEY = 256
PEER_TOPK = 16
EPS = 1e-6

LANES = 128
QK_CAT = KV_LORA + LANES
DT_LANE = 64
VMEM_LIMIT = 56 * 1024 * 1024

_HI = lax.Precision.HIGHEST


def _cparams(*sem):
    return pltpu.CompilerParams(dimension_semantics=sem, vmem_limit_bytes=VMEM_LIMIT)


def _dot(a, b):
    return jnp.dot(a, b, preferred_element_type=F32)


def _dot_nt(a, b):
    return lax.dot_general(a, b, (((1,), (1,)), ((), ())), preferred_element_type=F32)


def _dot_tn(a, b):
    return lax.dot_general(a, b, (((0,), (0,)), ((), ())), preferred_element_type=F32)


def _rms(x, w):
    ms = jnp.mean(x * x, axis=-1, keepdims=True)
    return (x * lax.rsqrt(ms + EPS)) * w


def _sigmoid(x):
    return jax.nn.sigmoid(x)


def _softplus(x):
    return jnp.maximum(x, 0.0) + jnp.log1p(jnp.exp(-jnp.abs(x)))


def _row_block(t, cap):
    if t <= cap:
        return t
    b = cap
    while t % b:
        b //= 2
    return b


_IN_WIDTHS = (2 * D_MODEL, D_INNER, CONV_DIM, Q_LORA, KV_LORA, LANES, LANES)


def _in_proj_body(x_ref, nw_ref, w_ref, *out_refs):
    h = _rms(x_ref[...], nw_ref[...]).astype(BF16)
    off = 0
    for ref, wd in zip(out_refs, _IN_WIDTHS):
        ref[...] = _dot(h, w_ref[:, off:off + wd])
        off += wd


def _in_proj(x2d, nw, wcat):
    t = x2d.shape[0]
    tm = _row_block(t, 256)
    wtot = sum(_IN_WIDTHS)
    return pl.pallas_call(
        _in_proj_body,
        grid=(t // tm,),
        in_specs=[pl.BlockSpec((tm, D_MODEL), lambda i: (i, 0)),
                  pl.BlockSpec((1, D_MODEL), lambda i: (0, 0)),
                  pl.BlockSpec((D_MODEL, wtot), lambda i: (0, 0))],
        out_specs=[pl.BlockSpec((tm, wd), lambda i: (i, 0)) for wd in _IN_WIDTHS],
        out_shape=[jax.ShapeDtypeStruct((t, wd), F32) for wd in _IN_WIDTHS],
        compiler_params=_cparams("parallel"),
        name="in_proj",
    )(x2d, nw, wcat)


def _head_expand():
    r = lax.broadcasted_iota(I32, (LANES, D_INNER), 0)
    c = lax.broadcasted_iota(I32, (LANES, D_INNER), 1)
    return (jnp.right_shift(c, 6) == (r - DT_LANE)).astype(F32)


def _ssd_prompt_body(nc, xbc_ref, sm_ref, z_ref, cw_ref, cb_ref, dtb_ref, alog_ref, dsk_ref, nrm_ref,
                     yn_ref, hfin_ref, xpad_ref, st_ref):
    c = pl.program_id(1)
    cl = SSD_CHUNK

    @pl.when(c == 0)
    def _():
        xpad_ref[0:8, :] = jnp.zeros((8, CONV_DIM), F32)
        st_ref[...] = jnp.zeros_like(st_ref)

    xpad_ref[8:8 + cl, :] = xbc_ref[...]
    conv = cb_ref[...]
    for k in range(CONV_WIDTH):
        conv = conv + xpad_ref[5 + k:5 + k + cl, :] * cw_ref[k:k + 1, :]
    xpad_ref[0:8, :] = xpad_ref[cl:cl + 8, :]
    u = conv * _sigmoid(conv)
    xs = u[:, :D_INNER]
    bb = u[:, D_INNER:D_INNER + SSM_GROUPS * D_STATE].astype(BF16)
    cb = u[:, D_INNER + SSM_GROUPS * D_STATE:].astype(BF16)

    row = lax.broadcasted_iota(I32, (cl, cl), 0)
    col = lax.broadcasted_iota(I32, (cl, cl), 1)
    dt_ok = (col >= DT_LANE) & (col < DT_LANE + SSM_HEADS)
    dt = jnp.where(dt_ok, _softplus(sm_ref[...] + dtb_ref[...]), 0.0)
    a = dt * (-jnp.exp(alog_ref[...]))
    tri = (row >= col).astype(F32)
    a_cs = jnp.dot(tri, a, precision=_HI, preferred_element_type=F32)
    a_cs_t = a_cs.T
    ea = jnp.exp(a_cs)
    dte = jnp.exp(a_cs[cl - 1:cl, :] - a_cs)
    ex = _head_expand()
    dt_e = jnp.dot(dt, ex, precision=_HI, preferred_element_type=F32)
    ea_e = jnp.dot(ea, ex, precision=_HI, preferred_element_type=F32)
    dte_e = jnp.dot(dte, ex, precision=_HI, preferred_element_type=F32)

    xdt = xs * dt_e
    xdt_b = xdt.astype(BF16)
    causal = row >= col
    ys = []
    for g in range(SSM_GROUPS):
        cbm = _dot_nt(cb[:, g * D_STATE:(g + 1) * D_STATE], bb[:, g * D_STATE:(g + 1) * D_STATE])
        for e in range(SSM_HPG):
            h = g * SSM_HPG + e
            seg = a_cs[:, DT_LANE + h:DT_LANE + h + 1] - a_cs_t[DT_LANE + h:DT_LANE + h + 1, :]
            lm = jnp.exp(jnp.where(causal, seg, -jnp.inf))
            ys.append(_dot((cbm * lm).astype(BF16), xdt_b[:, h * SSM_HEAD_DIM:(h + 1) * SSM_HEAD_DIM]))
    y_diag = jnp.concatenate(ys, axis=1)

    xw = (xdt * dte_e).astype(BF16)
    dec = ea_e[cl - 1:cl, :]
    half = D_INNER // SSM_GROUPS
    yoffs = []
    for g in range(SSM_GROUPS):
        s_prev = st_ref[g]
        yoffs.append(_dot(cb[:, g * D_STATE:(g + 1) * D_STATE], s_prev.astype(BF16)))
        st_ref[g] = dec[:, g * half:(g + 1) * half] * s_prev + _dot_tn(
            bb[:, g * D_STATE:(g + 1) * D_STATE], xw[:, g * half:(g + 1) * half])
    y = y_diag + jnp.concatenate(yoffs, axis=1) * ea_e + dsk_ref[...] * xs
    zv = z_ref[...]
    y = y * (zv * _sigmoid(zv))
    outs = []
    for g in range(SSM_GROUPS):
        outs.append(_rms(y[:, g * half:(g + 1) * half], nrm_ref[:, g * half:(g + 1) * half]))
    yn_ref[...] = jnp.concatenate(outs, axis=1).astype(BF16)

    @pl.when(c == nc - 1)
    def _():
        for g in range(SSM_GROUPS):
            s_t = st_ref[g].T
            for e in range(SSM_HPG):
                hfin_ref[0, g, e] = s_t[e * SSM_HEAD_DIM:(e + 1) * SSM_HEAD_DIM, :]


def _ssd_prompt(xbc, sm2, z, lw, nb, seq):
    nc = seq // SSD_CHUNK
    cl = SSD_CHUNK
    rowmap = lambda b, c: (b * nc + c, 0)
    full = lambda b, c: (0, 0)
    return pl.pallas_call(
        functools.partial(_ssd_prompt_body, nc),
        grid=(nb, nc),
        in_specs=[pl.BlockSpec((cl, CONV_DIM), rowmap),
                  pl.BlockSpec((cl, LANES), rowmap),
                  pl.BlockSpec((cl, D_INNER), rowmap),
                  pl.BlockSpec((CONV_WIDTH, CONV_DIM), full),
                  pl.BlockSpec((1, CONV_DIM), full),
                  pl.BlockSpec((1, LANES), full),
                  pl.BlockSpec((1, LANES), full),
                  pl.BlockSpec((1, D_INNER), full),
                  pl.BlockSpec((1, D_INNER), full)],
        out_specs=[pl.BlockSpec((cl, D_INNER), rowmap),
                   pl.BlockSpec((1, SSM_GROUPS, SSM_HPG, SSM_HEAD_DIM, D_STATE), lambda b, c: (b, 0, 0, 0, 0))],
        out_shape=[jax.ShapeDtypeStruct((nb * seq, D_INNER), BF16),
                   jax.ShapeDtypeStruct((nb, SSM_GROUPS, SSM_HPG, SSM_HEAD_DIM, D_STATE), F32)],
        scratch_shapes=[pltpu.VMEM((cl + 8, CONV_DIM), F32),
                        pltpu.VMEM((SSM_GROUPS, D_STATE, D_INNER // SSM_GROUPS), F32)],
        compiler_params=_cparams("parallel", "arbitrary"),
        name="ssd_prompt",
    )(xbc, sm2, z, lw["conv_w"], lw["conv_b"], lw["dt_bias_p"], lw["a_log_p"], lw["d_skip_e"], lw["ssm_norm"])


def _ssd_step_pre_body(xbc_ref, prev_ref, sm_ref, cw_ref, cb_ref, dtb_ref, u_ref, dt_ref):
    conv = cb_ref[...]
    for k in range(CONV_WIDTH - 1):
        conv = conv + prev_ref[k] * cw_ref[k:k + 1, :]
    conv = conv + xbc_ref[...] * cw_ref[CONV_WIDTH - 1:CONV_WIDTH, :]
    u_ref[...] = conv * _sigmoid(conv)
    dt_ref[...] = _softplus(sm_ref[...] + dtb_ref[...])


def _ssd_step_pre(xbc, prev_t, sm2, lw):
    nb = xbc.shape[0]
    return pl.pallas_call(
        _ssd_step_pre_body,
        out_shape=[jax.ShapeDtypeStruct((nb, CONV_DIM), F32), jax.ShapeDtypeStruct((nb, LANES), F32)],
        compiler_params=pltpu.CompilerParams(vmem_limit_bytes=VMEM_LIMIT),
        name="ssd_step_pre",
    )(xbc, prev_t, sm2, lw["conv_w"], lw["conv_b"], lw["dt_bias_p"])


def _ssd_step_body(h0_ref, xt_ref, zt_ref, b_ref, c_ref, dt_ref, alog_ref, dsk_ref, nrm_ref, hn_ref, yt_ref):
    xt = xt_ref[0]
    dtv = dt_ref[0]
    av = -jnp.exp(alog_ref[...])
    lane = lax.broadcasted_iota(I32, (SSM_HEAD_DIM, SSM_HEADS), 1)
    y = jnp.zeros((SSM_HEAD_DIM, SSM_HEADS), F32)
    for g in range(SSM_GROUPS):
        bg = b_ref[0][:, g * D_STATE:(g + 1) * D_STATE]
        cg = c_ref[0][:, g * D_STATE:(g + 1) * D_STATE]
        for e in range(SSM_HPG):
            h = g * SSM_HPG + e
            dth = dtv[:, DT_LANE + h:DT_LANE + h + 1]
            dec = jnp.exp(dth * av[:, DT_LANE + h:DT_LANE + h + 1])
            hn = dec * h0_ref[0, g, e] + (xt[:, h:h + 1] * dth) * bg
            hn_ref[0, g, e] = hn
            ycol = jnp.sum(hn * cg, axis=1, keepdims=True)
            y = jnp.where(lane == h, ycol, y)
    y = y + dsk_ref[...] * xt
    zv = zt_ref[0]
    y = y * (zv * _sigmoid(zv))
    sq = y * y
    in_g0 = lane < SSM_HPG
    n_g = float(D_INNER // SSM_GROUPS)
    ms0 = jnp.sum(jnp.sum(jnp.where(in_g0, sq, 0.0), axis=1, keepdims=True), axis=0, keepdims=True) / n_g
    ms1 = jnp.sum(jnp.sum(jnp.where(in_g0, 0.0, sq), axis=1, keepdims=True), axis=0, keepdims=True) / n_g
    scale = jnp.where(in_g0, lax.rsqrt(ms0 + EPS), lax.rsqrt(ms1 + EPS))
    yt_ref[0] = (y * scale) * nrm_ref[...]


def _ssd_step(h0_all, layer, xt, zt, bm, cm, dt, lw):
    nb = xt.shape[0]
    st_block = (None, 1, SSM_GROUPS, SSM_HPG, SSM_HEAD_DIM, D_STATE)
    pt_block = (1, SSM_HEAD_DIM, SSM_HEADS)
    vec = lambda n: pl.BlockSpec((1, 1, n), lambda b: (b, 0, 0))
    full = lambda r, n: pl.BlockSpec((r, n), lambda b: (0, 0))
    return pl.pallas_call(
        _ssd_step_body,
        grid=(nb,),
        in_specs=[pl.BlockSpec(st_block, lambda b: (layer, b, 0, 0, 0, 0)),
                  pl.BlockSpec(pt_block, lambda b: (b, 0, 0)),
                  pl.BlockSpec(pt_block, lambda b: (b, 0, 0)),
                  vec(SSM_GROUPS * D_STATE), vec(SSM_GROUPS * D_STATE), vec(LANES),
                  full(1, LANES), full(1, SSM_HEADS), full(SSM_HEAD_DIM, SSM_HEADS)],
        out_specs=[pl.BlockSpec((1, SSM_GROUPS, SSM_HPG, SSM_HEAD_DIM, D_STATE), lambda b: (b, 0, 0, 0, 0)),
                   pl.BlockSpec(pt_block, lambda b: (b, 0, 0))],
        out_shape=[jax.ShapeDtypeStruct((nb, SSM_GROUPS, SSM_HPG, SSM_HEAD_DIM, D_STATE), F32),
                   jax.ShapeDtypeStruct((nb, SSM_HEAD_DIM, SSM_HEADS), F32)],
        compiler_params=_cparams("parallel"),
        name="ssd_step",
    )(h0_all, xt, zt, bm, cm, dt, lw["a_log_p"], lw["d_skip_row"], lw["ssm_norm_t"])


def _mla_prep_body(cq_ref, ckv_ref, sm1_ref, sm2_ref, qtab_ref, ktab_ref, qn_ref, kvn_ref,
                   wn_ref, wpe_ref, wpes_ref, wkb_ref, qcat_ref, kcat_ref, lat_ref, kpe_ref):
    hq = _rms(cq_ref[...], qn_ref[...]).astype(BF16)
    qn = _dot(hq, wn_ref[...]).astype(BF16)
    npe = MLA_HEADS * LANES
    qpe = (_dot(hq, wpe_ref[...]) * qtab_ref[:, :npe] + _dot(hq, wpes_ref[...]) * qtab_ref[:, npe:]) * ATTN_SCALE
    for h in range(MLA_HEADS):
        ql = _dot(qn[:, h * QK_NOPE:(h + 1) * QK_NOPE], wkb_ref[h]) * ATTN_SCALE
        qcat_ref[0, h] = jnp.concatenate([ql, qpe[:, h * LANES:(h + 1) * LANES]], axis=1).astype(BF16)
    lat = _rms(ckv_ref[...], kvn_ref[...])
    lat_ref[...] = lat
    kpe = sm1_ref[...] * ktab_ref[:, :LANES] + sm2_ref[...] * ktab_ref[:, LANES:]
    kpe_ref[...] = kpe[:, :QK_ROPE]
    kcat_ref[...] = jnp.concatenate([lat, kpe], axis=1).astype(BF16)


def _mla_prep(cq, ckv, sm1, sm2, qtab, ktab, lw, nb, seq):
    t = nb * seq
    tm = _row_block(seq, 256)
    nl = seq // tm
    rowmap = lambda b, i: (b * nl + i, 0)
    tabmap = lambda b, i: (i, 0)
    full2 = lambda b, i: (0, 0)
    npe = MLA_HEADS * LANES
    return pl.pallas_call(
        _mla_prep_body,
        grid=(nb, nl),
        in_specs=[pl.BlockSpec((tm, Q_LORA), rowmap),
                  pl.BlockSpec((tm, KV_LORA), rowmap),
                  pl.BlockSpec((tm, LANES), rowmap),
                  pl.BlockSpec((tm, LANES), rowmap),
                  pl.BlockSpec((tm, 2 * npe), tabmap),
                  pl.BlockSpec((tm, 2 * LANES), tabmap),
                  pl.BlockSpec((1, Q_LORA), full2),
                  pl.BlockSpec((1, KV_LORA), full2),
                  pl.BlockSpec((Q_LORA, MLA_HEADS * QK_NOPE), full2),
                  pl.BlockSpec((Q_LORA, npe), full2),
                  pl.BlockSpec((Q_LORA, npe), full2),
                  pl.BlockSpec((MLA_HEADS, QK_NOPE, KV_LORA), lambda b, i: (0, 0, 0))],
        out_specs=[pl.BlockSpec((1, MLA_HEADS, tm, QK_CAT), lambda b, i: (b, 0, i, 0)),
                   pl.BlockSpec((tm, QK_CAT), rowmap),
                   pl.BlockSpec((tm, KV_LORA), rowmap),
                   pl.BlockSpec((tm, QK_ROPE), rowmap)],
        out_shape=[jax.ShapeDtypeStruct((nb, MLA_HEADS, seq, QK_CAT), BF16),
                   jax.ShapeDtypeStruct((t, QK_CAT), BF16),
                   jax.ShapeDtypeStruct((t, KV_LORA), F32),
                   jax.ShapeDtypeStruct((t, QK_ROPE), F32)],
        compiler_params=_cparams("parallel", "parallel"),
        name="mla_prep",
    )(cq, ckv, sm1, sm2, qtab, ktab, lw["q_norm"], lw["kv_norm"], lw["wq_nope"], lw["wq_pe"], lw["wq_pe_sw"],
      lw["w_kb_t"])


def _attn_prompt_body(tq, tk, nk, q_ref, k_ref, wvb_ref, o_ref, m_ref, l_ref, acc_ref):
    i = pl.program_id(1)
    j = pl.program_id(2)

    @pl.when(j == 0)
    def _():
        m_ref[...] = jnp.full(m_ref.shape, -jnp.inf, F32)
        l_ref[...] = jnp.zeros(l_ref.shape, F32)
        acc_ref[...] = jnp.zeros(acc_ref.shape, F32)

    @pl.when(j * tk <= i * tq + tq - 1)
    def _():
        k = k_ref[...]
        v = k[:, :KV_LORA]
        qpos = i * tq + lax.broadcasted_iota(I32, (tq, tk), 0)
        kpos = j * tk + lax.broadcasted_iota(I32, (tq, tk), 1)
        visible = kpos <= qpos
        ahead = 2
        s_q = [_dot_nt(q_ref[0, h], k) for h in range(ahead)]
        for h in range(MLA_HEADS):
            hs = slice(h * tq, (h + 1) * tq)
            s = jnp.where(visible, s_q.pop(0), -jnp.inf)
            if h + ahead < MLA_HEADS:
                s_q.append(_dot_nt(q_ref[0, h + ahead], k))
            m_prev = m_ref[hs]
            m_new = jnp.maximum(m_prev, jnp.max(s, axis=1, keepdims=True))
            p = jnp.exp(s - m_new)
            alpha = jnp.exp(m_prev - m_new)
            l_ref[hs] = alpha * l_ref[hs] + jnp.sum(p, axis=1, keepdims=True)
            acc_ref[hs] = alpha * acc_ref[hs] + _dot(p.astype(BF16), v)
            m_ref[hs] = m_new

    @pl.when(j == nk - 1)
    def _():
        outs = []
        for h in range(MLA_HEADS):
            hs = slice(h * tq, (h + 1) * tq)
            outs.append(_dot((acc_ref[hs] / l_ref[hs]).astype(BF16), wvb_ref[h]))
        o_ref[...] = jnp.concatenate(outs, axis=1).astype(BF16)


def _attn_prompt(qcat, kcat, w_vb, nb, seq):
    tq = _row_block(seq, 256)
    tk = _row_block(seq, 512)
    nq, nk = seq // tq, seq // tk

    def kmap(b, i, j):
        return (b * nk + jnp.minimum(j, (i * tq + tq - 1) // tk), 0)

    return pl.pallas_call(
        functools.partial(_attn_prompt_body, tq, tk, nk),
        grid=(nb, nq, nk),
        in_specs=[pl.BlockSpec((1, MLA_HEADS, tq, QK_CAT), lambda b, i, j: (b, 0, i, 0)),
                  pl.BlockSpec((tk, QK_CAT), kmap),
                  pl.BlockSpec((MLA_HEADS, KV_LORA, V_HEAD), lambda b, i, j: (0, 0, 0))],
        out_specs=pl.BlockSpec((tq, MLA_HEADS * V_HEAD), lambda b, i, j: (b * nq + i, 0)),
        out_shape=jax.ShapeDtypeStruct((nb * seq, MLA_HEADS * V_HEAD), BF16),
        scratch_shapes=[pltpu.VMEM((MLA_HEADS * tq, 1), F32), pltpu.VMEM((MLA_HEADS * tq, 1), F32),
                        pltpu.VMEM((MLA_HEADS * tq, KV_LORA), F32)],
        compiler_params=_cparams("parallel", "parallel", "arbitrary"),
        name="attn_prompt",
    )(qcat, kcat, w_vb)


def _attn_decode_body(pg, nj, pt_ref, q_ref, kc_ref, wvb_ref, *refs):
    lat_refs = refs[:pg]
    kr_refs = refs[pg:2 * pg]
    o_ref, m_ref, l_ref, acc_ref = refs[2 * pg:]
    j = pl.program_id(1)

    @pl.when(j == 0)
    def _():
        m_ref[...] = jnp.full(m_ref.shape, -jnp.inf, F32)
        l_ref[...] = jnp.zeros(l_ref.shape, F32)
        acc_ref[...] = jnp.zeros(acc_ref.shape, F32)

    q = q_ref[0]
    ql = q[:, :KV_LORA]
    qr = q[:, KV_LORA:KV_LORA + QK_ROPE]
    kvs = [lat_refs[i][...].astype(BF16) for i in range(pg)]
    ss = [_dot_nt(ql, kvs[i]) + _dot_nt(qr, kr_refs[i][...].astype(BF16)) for i in range(pg)]
    m_prev = m_ref[...]
    m_blk = ss[0]
    for i in range(1, pg):
        m_blk = jnp.maximum(m_blk, ss[i])
    m_new = jnp.maximum(m_prev, jnp.max(m_blk, axis=1, keepdims=True))
    ps = [jnp.exp(ss[i] - m_new) for i in range(pg)]
    p_sum = ps[0]
    for i in range(1, pg):
        p_sum = p_sum + ps[i]
    pvs = [_dot(ps[i].astype(BF16), kvs[i]) for i in range(pg)]
    while len(pvs) > 1:
        pvs = [pvs[i] + pvs[i + 1] for i in range(0, len(pvs), 2)]
    alpha = jnp.exp(m_prev - m_new)
    l_ref[...] = alpha * l_ref[...] + jnp.sum(p_sum, axis=1, keepdims=True)
    acc_ref[...] = alpha * acc_ref[...] + pvs[0]
    m_ref[...] = m_new

    @pl.when(j == nj - 1)
    def _():
        kc = kc_ref[0].astype(F32)
        s = jnp.sum(q.astype(F32) * kc, axis=1, keepdims=True)
        m_prev = m_ref[...]
        m_new = jnp.maximum(m_prev, s)
        p = jnp.exp(s - m_new)
        alpha = jnp.exp(m_prev - m_new)
        l_fin = alpha * l_ref[...] + p
        acc = alpha * acc_ref[...] + p.astype(BF16).astype(F32) * kc[:, :KV_LORA]
        o = (acc / l_fin).astype(BF16)
        full = _dot(o, wvb_ref[...])
        hrow = lax.broadcasted_iota(I32, full.shape, 0)
        hcol = jnp.right_shift(lax.broadcasted_iota(I32, full.shape, 1), 6)
        o_ref[0] = jnp.sum(jnp.where(hrow == hcol, full, 0.0), axis=0, keepdims=True).astype(BF16)


def _attn_decode(page_table, qdec, kcur, w_vb_all, cache_latent, cache_krope, layer):
    nb, n_pages = page_table.shape
    page = cache_latent.shape[2]
    pg = 16
    while n_pages % pg:
        pg //= 2
    nj = n_pages // pg

    def pmap(i):
        return lambda b, j, pt: (layer, pt[b, j * pg + i], 0, 0)

    in_specs = [pl.BlockSpec((1, MLA_HEADS, QK_CAT), lambda b, j, pt: (b, 0, 0)),
                pl.BlockSpec((1, 1, QK_CAT), lambda b, j, pt: (b, 0, 0)),
                pl.BlockSpec((KV_LORA, MLA_HEADS * V_HEAD), lambda b, j, pt: (0, 0))]
    in_specs += [pl.BlockSpec((None, None, page, KV_LORA), pmap(i)) for i in range(pg)]
    in_specs += [pl.BlockSpec((None, None, page, QK_ROPE), pmap(i)) for i in range(pg)]
    grid_spec = pltpu.PrefetchScalarGridSpec(
        num_scalar_prefetch=1,
        grid=(nb, nj),
        in_specs=in_specs,
        out_specs=pl.BlockSpec((1, 1, MLA_HEADS * V_HEAD), lambda b, j, pt: (b, 0, 0)),
        scratch_shapes=[pltpu.VMEM((MLA_HEADS, 1), F32), pltpu.VMEM((MLA_HEADS, 1), F32),
                        pltpu.VMEM((MLA_HEADS, KV_LORA), F32)])
    return pl.pallas_call(
        functools.partial(_attn_decode_body, pg, nj),
        grid_spec=grid_spec,
        out_shape=jax.ShapeDtypeStruct((nb, 1, MLA_HEADS * V_HEAD), BF16),
        compiler_params=_cparams("parallel", "arbitrary"),
        name="attn_decode",
    )(page_table, qdec, kcur, w_vb_all, *([cache_latent] * pg), *([cache_krope] * pg))


def _mix_body(x_ref, g_ref, bg_ref, yn_ref, o_ref, wssm_ref, wmla_ref, wout_ref, x1_ref):
    ya = _dot(yn_ref[...], wssm_ref[...])
    yb = _dot(o_ref[...], wmla_ref[...])
    gate = _sigmoid(g_ref[...] + bg_ref[...])
    mix = gate[:, :D_MODEL] * ya + gate[:, D_MODEL:] * yb
    x1_ref[...] = x_ref[...] + _dot(mix.astype(BF16), wout_ref[...])


def _mix(x2d, gates, yn, o, lw):
    t = x2d.shape[0]
    tm = _row_block(t, 256)
    row = lambda n: pl.BlockSpec((tm, n), lambda i: (i, 0))
    full = lambda r, n: pl.BlockSpec((r, n), lambda i: (0, 0))
    return pl.pallas_call(
        _mix_body,
        grid=(t // tm,),
        in_specs=[row(D_MODEL), row(2 * D_MODEL), full(1, 2 * D_MODEL), row(D_INNER), row(MLA_HEADS * V_HEAD),
                  full(D_INNER, D_MODEL), full(MLA_HEADS * V_HEAD, D_MODEL), full(D_MODEL, D_MODEL)],
        out_specs=row(D_MODEL),
        out_shape=jax.ShapeDtypeStruct((t, D_MODEL), F32),
        compiler_params=_cparams("parallel"),
        name="mix",
    )(x2d, gates, lw["b_gate"], yn, o, lw["w_ssm_proj"], lw["w_mla_proj"], lw["w_out"])


def _peer_scores_body(x_ref, nw_ref, wq_ref, keys_ref, hn_ref, s1_ref, s2_ref):
    hn = _rms(x_ref[...], nw_ref[...]).astype(BF16)
    hn_ref[...] = hn
    q = _dot(hn, wq_ref[...]).astype(BF16)
    half = PEER_D_KEY // 2
    for h in range(PEER_HEADS):
        s1_ref[h] = _dot_nt(keys_ref[h, 0], q[:, h * PEER_D_KEY:h * PEER_D_KEY + half])
        s2_ref[h] = _dot_nt(keys_ref[h, 1], q[:, h * PEER_D_KEY + half:(h + 1) * PEER_D_KEY])


def _peer_scores(x1, lw):
    t = x1.shape[0]
    tm = _row_block(t, 256)
    sblock = pl.BlockSpec((PEER_HEADS, N_KEYS, tm), lambda i: (0, 0, i))
    return pl.pallas_call(
        _peer_scores_body,
        grid=(t // tm,),
        in_specs=[pl.BlockSpec((tm, D_MODEL), lambda i: (i, 0)),
                  pl.BlockSpec((1, D_MODEL), lambda i: (0, 0)),
                  pl.BlockSpec((D_MODEL, PEER_HEADS * PEER_D_KEY), lambda i: (0, 0)),
                  pl.BlockSpec((PEER_HEADS, 2, N_KEYS, PEER_D_KEY // 2), lambda i: (0, 0, 0, 0))],
        out_specs=[pl.BlockSpec((tm, D_MODEL), lambda i: (i, 0)), sblock, sblock],
        out_shape=[jax.ShapeDtypeStruct((t, D_MODEL), BF16),
                   jax.ShapeDtypeStruct((PEER_HEADS, N_KEYS, t), F32),
                   jax.ShapeDtypeStruct((PEER_HEADS, N_KEYS, t), F32)],
        compiler_params=_cparams("parallel"),
        name="peer_scores",
    )(x1, lw["norm_ffn"], lw["peer_wq"], lw["peer_keys"])


SUBLANES = 8


def _top16_rows(s):
    n = s.shape[0]
    row = lax.broadcasted_iota(I32, s.shape, 0)
    r16 = lax.broadcasted_iota(I32, (PEER_TOPK, LANES), 0)

    def body(r, carry):
        rank, vals = carry
        alive = rank == PEER_TOPK
        m = jnp.max(jnp.where(alive, s, -jnp.inf), axis=0, keepdims=True)
        idx = jnp.min(jnp.where(alive & (s == m), row, n), axis=0, keepdims=True)
        rank = jnp.where(row == idx, r, rank)
        vals = jnp.where(r16 == r, m, vals)
        return rank, vals

    return lax.fori_loop(0, PEER_TOPK, body,
                         (jnp.full(s.shape, PEER_TOPK, I32), jnp.zeros((PEER_TOPK, LANES), F32)))


def _batcher_net(n):
    def merge(lo, hi, r):
        step = r * 2
        if step < hi - lo:
            yield from merge(lo, hi, step)
            yield from merge(lo + r, hi, step)
            yield from [(i, i + r) for i in range(lo + r, hi - r, step)]
        else:
            yield (lo, lo + r)

    def sort(lo, hi):
        if hi - lo >= 1:
            mid = lo + (hi - lo) // 2
            yield from sort(lo, mid)
            yield from sort(mid + 1, hi)
            yield from merge(lo, hi, 1)

    return tuple(sort(0, n - 1))


_NET16 = _batcher_net(PEER_TOPK)


def _cmpx(v, x, i, j):
    a, b = v[i], v[j]
    if b is None:
        return
    if a is None:
        v[i], v[j] = b, None
        if x is not None:
            x[i], x[j] = x[j], x[i]
        return
    v[i], v[j] = jnp.maximum(a, b), jnp.minimum(a, b)
    if x is not None:
        first = a >= b
        x[i], x[j] = jnp.where(first, x[i], x[j]), jnp.where(first, x[j], x[i])


def _top16_sorted(groups, payload=None):
    pad = PEER_TOPK - len(groups)
    v = list(groups) + [None] * pad
    x = None if payload is None else list(payload) + [None] * pad
    for i, j in _NET16:
        _cmpx(v, x, i, j)
    for shift in (4, 2, 1):
        roll = lambda y: None if y is None else pltpu.roll(y, shift, 0)
        b = [roll(y) for y in v]
        bx = None if x is None else [roll(y) for y in x]
        c, cx = [], (None if x is None else [])
        for i in range(PEER_TOPK):
            p, q = v[i], b[PEER_TOPK - 1 - i]
            if p is None or q is None:
                c.append(q if p is None else p)
                if x is not None:
                    cx.append(bx[PEER_TOPK - 1 - i] if p is None else x[i])
            else:
                c.append(jnp.maximum(p, q))
                if x is not None:
                    cx.append(jnp.where(p >= q, x[i], bx[PEER_TOPK - 1 - i]))
        for d in (8, 4, 2, 1):
            for i in range(PEER_TOPK):
                if not i & d:
                    _cmpx(c, cx, i, i + d)
        v, x = c, cx
    return (v, x) if payload is not None else v


def _colsum(x):
    x = x + pltpu.roll(x, 4, 0)
    x = x + pltpu.roll(x, 2, 0)
    return x + pltpu.roll(x, 1, 0)


def _picks_by_value(s1, s2):
    k = PEER_TOPK
    ng = N_KEYS // SUBLANES
    sub = lax.broadcasted_iota(I32, (SUBLANES, LANES), 0)
    subf = sub.astype(F32)
    one = lambda m: jnp.where(m, 1.0, 0.0)
    c1 = [s1[SUBLANES * v:SUBLANES * (v + 1)] for v in range(ng)]
    c2 = [s2[SUBLANES * v:SUBLANES * (v + 1)] for v in range(ng)]
    ids = [subf + float(SUBLANES * v) for v in range(ng)]
    t1, i1 = _top16_sorted(c1, ids)
    t2, i2 = _top16_sorted(c2, ids)

    def as_rows(t, base):
        out = t[base]
        for r in range(1, SUBLANES):
            out = jnp.where(sub == r, t[base + r], out)
        return out

    sc2_lo, sc2_hi, sc1_hi = as_rows(t2, 0), as_rows(t2, SUBLANES), as_rows(t1, SUBLANES)
    cand = [t1[0] + sc2_lo, t1[0] + sc2_hi] + [t1[r] + sc2_lo for r in range(1, SUBLANES)] + [sc1_hi + t2[0]]
    tau = _top16_sorted(cand)[k - 1]
    sel = [c >= tau for c in cand]
    n_r = [_colsum(one(sel[0]) + one(sel[1]))] + [_colsum(one(sel[1 + r])) for r in range(1, SUBLANES)]
    n_hi = one(sel[9])
    n_r += [_colsum(jnp.where(sub == q, n_hi, 0.0)) for q in range(SUBLANES)]
    cmax = t1[0] + t2[0]
    zsum = jnp.where(sel[0], jnp.exp(cand[0] - cmax), 0.0)
    for g in range(1, len(cand)):
        zsum = zsum + jnp.where(sel[g], jnp.exp(cand[g] - cmax), 0.0)
    scale = 0.5 / _colsum(zsum)
    pre = [jnp.zeros((SUBLANES, LANES), F32)]
    for r in range(k):
        pre.append(pre[r] + n_r[r])
    bad = pre[k] != float(k)
    for r in range(k - 1):
        bad = bad | (t1[r] == t1[r + 1]) | (t2[r] == t2[r + 1])
    n_in1, n_in2 = one(c1[0] >= t1[k - 1]), one(c2[0] >= t2[k - 1])
    for v in range(1, ng):
        n_in1, n_in2 = n_in1 + one(c1[v] >= t1[k - 1]), n_in2 + one(c2[v] >= t2[k - 1])
    bad = bad | (_colsum(n_in1) != float(k)) | (_colsum(n_in2) != float(k))
    outs = ([], [], [])
    for half in range(k // SUBLANES):
        pick = subf + float(SUBLANES * half)
        r1p = jnp.zeros((SUBLANES, LANES), F32)
        base = jnp.zeros((SUBLANES, LANES), F32)
        for r in range(1, k):
            started = pre[r] <= pick
            r1p = jnp.where(started, float(r), r1p)
            base = jnp.where(started, pre[r], base)
        r2p = pick - base
        a_id, b_id, v1, v2 = i1[0], i2[0], t1[0], t2[0]
        for r in range(1, k):
            is1, is2 = r1p == float(r), r2p == float(r)
            a_id, v1 = jnp.where(is1, i1[r], a_id), jnp.where(is1, t1[r], v1)
            b_id, v2 = jnp.where(is2, i2[r], b_id), jnp.where(is2, t2[r], v2)
        outs[0].append(a_id)
        outs[1].append(b_id)
        outs[2].append(jnp.exp((v1 + v2) - cmax) * scale)
    return tuple(jnp.concatenate(o, axis=0) for o in outs) + (one(bad),)


def _picks_by_index(s1, s2):
    k = PEER_TOPK
    r16 = lax.broadcasted_iota(I32, (k, LANES), 0)
    rowf = lax.broadcasted_iota(I32, (N_KEYS, LANES), 0).astype(F32)
    rank1, sc1 = _top16_rows(s1)
    rank2, sc2 = _top16_rows(s2)

    def index_of_rank(rank):
        out = jnp.zeros((k, LANES), F32)
        for r in range(k):
            out = jnp.where(r16 == r, jnp.sum(jnp.where(rank == r, rowf, 0.0), axis=0, keepdims=True), out)
        return out

    i1, i2 = index_of_rank(rank1), index_of_rank(rank2)
    rep = lambda row, n: jnp.broadcast_to(row, (n, LANES))
    cand = jnp.concatenate([sc1[0:1] + sc2] + [sc1[r:r + 1] + sc2[0:8] for r in range(1, 8)] + [sc1[8:16] + sc2[0:1]],
                           axis=0)
    a_c = jnp.concatenate([rep(i1[0:1], k)] + [rep(i1[r:r + 1], 8) for r in range(1, 8)] + [i1[8:16]], axis=0)
    b_c = jnp.concatenate([i2] + [i2[0:8]] * 7 + [rep(i2[0:1], 8)], axis=0)
    rankc, _ = _top16_rows(cand)
    z = jnp.sum(jnp.where(rankc < k, jnp.exp(cand - cand[0:1]), 0.0), axis=0, keepdims=True)
    a_id = jnp.zeros((k, LANES), F32)
    b_id = jnp.zeros((k, LANES), F32)
    val = jnp.zeros((k, LANES), F32)
    for p in range(k):
        m = rankc == p
        pick = lambda c: jnp.sum(jnp.where(m, c, 0.0), axis=0, keepdims=True)
        a_id = jnp.where(r16 == p, pick(a_c), a_id)
        b_id = jnp.where(r16 == p, pick(b_c), b_id)
        val = jnp.where(r16 == p, pick(cand), val)
    return a_id, b_id, jnp.exp(val - cand[0:1]) * (0.5 / z)


def _peer_gates_body(s1_ref, s2_ref, a_ref, b_ref, g_ref, at_ref, bt_ref, gt_ref):
    k = PEER_TOPK

    def head(h, _):
        s1 = s1_ref[h]
        s2 = s2_ref[h]
        rows = pl.ds(pl.multiple_of(h * k, k), k)
        a_id, b_id, gate, bad = _picks_by_value(s1, s2)
        at_ref[rows, :] = a_id
        bt_ref[rows, :] = b_id
        gt_ref[rows, :] = gate

        @pl.when(jnp.max(bad) > 0.0)
        def _():
            at_ref[rows, :], bt_ref[rows, :], gt_ref[rows, :] = _picks_by_index(s1, s2)

        return 0

    lax.fori_loop(0, PEER_HEADS, head, 0)
    a_ref[...] = at_ref[...].T
    b_ref[...] = bt_ref[...].T
    g_ref[...] = gt_ref[...].T


def _peer_gates(s1t, s2t):
    t = s1t.shape[2]
    nslot = PEER_HEADS * PEER_TOPK
    blk = pl.BlockSpec((PEER_HEADS, N_KEYS, LANES), lambda i: (0, 0, i))
    oblk = pl.BlockSpec((LANES, nslot), lambda i: (i, 0))
    shp = jax.ShapeDtypeStruct((t, nslot), F32)
    return pl.pallas_call(
        _peer_gates_body,
        grid=(t // LANES,),
        in_specs=[blk, blk],
        out_specs=[oblk, oblk, oblk],
        out_shape=[shp, shp, shp],
        scratch_shapes=[pltpu.VMEM((nslot, LANES), F32)] * 3,
        compiler_params=_cparams("parallel"),
        name="peer_gates",
    )(s1t, s2t)


_SQRT_HALF = 0.7071067811865476
_HI16 = -65536
_GPAD = 8


def _peer_experts_body(tm, eb, nj, final, hn_ref, u_ref, v_ref, a_ref, b_ref, g_ref, x_ref, nf_ref,
                       o_ref, acc_ref, gw_ref):
    j = pl.program_id(1)
    pitch = tm + _GPAD
    half_a = N_KEYS // 2

    @pl.when(j == 0)
    def _():
        acc_ref[...] = jnp.zeros(acc_ref.shape, F32)
        key = lax.broadcasted_iota(I32, (N_KEYS, PEER_HEADS * PEER_TOPK), 0).astype(F32)

        def group(i, _):
            rows = pl.ds(pl.multiple_of(i * SUBLANES, SUBLANES), SUBLANES)
            a8, b8, g8 = a_ref[rows, :], b_ref[rows, :], g_ref[rows, :]
            for q in range(SUBLANES):
                pt = jnp.where(key == a8[q:q + 1], g8[q:q + 1], 0.0).astype(BF16)
                qt = jnp.where(key == b8[q:q + 1], 1.0, 0.0).astype(BF16)
                gt = _dot_nt(pt, qt).astype(BF16).astype(F32)
                hi = jnp.bitwise_and(pltpu.bitcast(gt[:half_a], I32), _HI16)
                lo = lax.shift_right_logical(pltpu.bitcast(gt[half_a:], I32), 16)
                gw_ref[pl.ds(i * SUBLANES + q, half_a, stride=pitch), :] = jnp.bitwise_or(hi, lo)
            return 0

        lax.fori_loop(0, tm // SUBLANES, group, 0)

    tiles = eb // N_KEYS
    a_lo = lax.rem(j * tiles, half_a)
    shift = jnp.where(j * tiles >= half_a, 16, 0)
    act = _dot_nt(hn_ref[...], u_ref[...])
    ws = []
    for al in range(tiles):
        word = gw_ref[pl.ds(pl.multiple_of((a_lo + al) * pitch, SUBLANES), tm), :]
        gate = pltpu.bitcast(jnp.bitwise_and(lax.shift_left(word, shift), _HI16), F32)
        x = act[:, al * N_KEYS:(al + 1) * N_KEYS]
        ws.append((gate * (x * (1.0 + lax.erf(x * _SQRT_HALF)))).astype(BF16))
    acc_ref[...] += _dot(jnp.concatenate(ws, axis=1), v_ref[...])

    @pl.when(j == nj - 1)
    def _():
        x2 = x_ref[...] + acc_ref[...]
        if final:
            o_ref[...] = _rms(x2, nf_ref[...])
        else:
            o_ref[...] = x2


def _peer_experts(hn, picks, x1, lw, norm_final, final):
    t = x1.shape[0]
    tm = _row_block(t, 512)
    eb = 8 * N_KEYS
    nj = N_EXPERTS // eb
    nslot = PEER_HEADS * PEER_TOPK
    a_id, b_id, gate = picks
    row = lambda n: pl.BlockSpec((tm, n), lambda i, j: (i, 0))
    return pl.pallas_call(
        functools.partial(_peer_experts_body, tm, eb, nj, final),
        grid=(t // tm, nj),
        in_specs=[row(D_MODEL),
                  pl.BlockSpec((eb, D_MODEL), lambda i, j: (j, 0)),
                  pl.BlockSpec((eb, D_MODEL), lambda i, j: (j, 0)),
                  row(nslot), row(nslot), row(nslot), row(D_MODEL),
                  pl.BlockSpec((1, D_MODEL), lambda i, j: (0, 0))],
        out_specs=row(D_MODEL),
        out_shape=jax.ShapeDtypeStruct((t, D_MODEL), F32),
        scratch_shapes=[pltpu.VMEM((tm, D_MODEL), F32),
                        pltpu.VMEM((N_KEYS // 2 * (tm + _GPAD), N_KEYS), I32)],
        compiler_params=_cparams("parallel", "arbitrary"),
        name="peer_experts",
    )(hn, lw["peer_u"], lw["peer_v"], a_id, b_id, gate, x1, norm_final)


def _peer(x1, lw, norm_final, final):
    hn, s1t, s2t = _peer_scores(x1, lw)
    picks = _peer_gates(s1t, s2t)
    return _peer_experts(hn, picks, x1, lw, norm_final, final)


def _pad_lanes(v, start, total=LANES):
    return jnp.zeros((1, total), F32).at[0, start:start + v.shape[0]].set(v)


def _layer_weights(l, norm_mix, w_in, b_gate, conv_w, conv_b, dt_bias, a_log, d_skip, ssm_norm, w_ssm_proj,
                   q_norm, w_qb, kv_norm, w_kvb, w_mla_proj, w_out, norm_ffn, peer_wq, peer_keys, peer_u, peer_v):
    o_dt = 3 * D_MODEL + CONV_DIM
    o_cq = o_dt + SSM_HEADS
    o_ckv = o_cq + Q_LORA
    o_kr = o_ckv + KV_LORA
    w = w_in[l]
    half = QK_ROPE // 2
    zcols = lambda n: jnp.zeros((D_MODEL, n), F32)
    kr = w[:, o_kr:o_kr + QK_ROPE]
    kr_sw = jnp.concatenate([kr[:, half:], kr[:, :half]], axis=1)
    sm1 = jnp.concatenate([kr, zcols(LANES - QK_ROPE)], axis=1)
    sm2 = jnp.concatenate([kr_sw, zcols(DT_LANE - QK_ROPE), w[:, o_dt:o_cq], zcols(LANES - DT_LANE - SSM_HEADS)], axis=1)
    wcat = jnp.concatenate([w[:, :o_dt], w[:, o_cq:o_kr], sm1, sm2], axis=1).astype(BF16)

    wq = w_qb[l].reshape(Q_LORA, MLA_HEADS, QK_NOPE + QK_ROPE)
    wq_nope = wq[:, :, :QK_NOPE].reshape(Q_LORA, MLA_HEADS * QK_NOPE)
    pe = wq[:, :, QK_NOPE:]
    pe_sw = jnp.concatenate([pe[:, :, half:], pe[:, :, :half]], axis=2)
    padpe = lambda p: jnp.pad(p, ((0, 0), (0, 0), (0, LANES - QK_ROPE))).reshape(Q_LORA, MLA_HEADS * LANES)
    wkv = w_kvb[l].reshape(KV_LORA, MLA_HEADS, QK_NOPE + V_HEAD)
    w_kb_t = jnp.transpose(wkv[:, :, :QK_NOPE], (1, 2, 0))
    w_vb = jnp.transpose(wkv[:, :, QK_NOPE:], (1, 0, 2))
    w_vb_all = wkv[:, :, QK_NOPE:].reshape(KV_LORA, MLA_HEADS * V_HEAD)
    row = lambda v: v.reshape(1, -1)
    return dict(
        norm_mix=row(norm_mix[l]), wcat=wcat, b_gate=row(b_gate[l]),
        conv_w=conv_w[l], conv_b=row(conv_b[l]),
        dt_bias_p=_pad_lanes(dt_bias[l], DT_LANE), a_log_p=_pad_lanes(a_log[l], DT_LANE),
        d_skip_e=row(jnp.repeat(d_skip[l], SSM_HEAD_DIM)), d_skip_row=row(d_skip[l]),
        ssm_norm=row(ssm_norm[l]), ssm_norm_t=ssm_norm[l].reshape(SSM_HEADS, SSM_HEAD_DIM).T,
        w_ssm_proj=w_ssm_proj[l].astype(BF16),
        q_norm=row(q_norm[l]), kv_norm=row(kv_norm[l]),
        wq_nope=wq_nope.astype(BF16), wq_pe=padpe(pe).astype(BF16), wq_pe_sw=padpe(pe_sw).astype(BF16),
        w_kb_t=w_kb_t.astype(BF16), w_vb=w_vb.astype(BF16), w_vb_all=w_vb_all.astype(BF16),
        w_mla_proj=w_mla_proj[l].astype(BF16), w_out=w_out[l].astype(BF16),
        norm_ffn=row(norm_ffn[l]), peer_wq=peer_wq[l].astype(BF16), peer_keys=peer_keys[l].astype(BF16),
        peer_u=peer_u[l].astype(BF16), peer_v=peer_v[l].astype(BF16))


def _rope_tables(pos):
    half = QK_ROPE // 2
    inv = ROPE_THETA ** (-jnp.arange(half, dtype=F32) / half)
    ang = pos.astype(F32)[:, None] * inv[None, :]
    cos, sin = jnp.cos(ang), jnp.sin(ang)
    pad = jnp.zeros((pos.shape[0], LANES - QK_ROPE), F32)
    c128 = jnp.concatenate([cos, cos, pad], axis=1)
    s128 = jnp.concatenate([-sin, sin, pad], axis=1)
    ktab = jnp.concatenate([c128, s128], axis=1)
    qtab = jnp.concatenate([jnp.tile(c128, (1, MLA_HEADS)), jnp.tile(s128, (1, MLA_HEADS))], axis=1)
    return qtab, ktab


def _prompt_layer(x2d, nb, seq, lw, tabs, norm_final, final):
    gates, z, xbc, cq, ckv, sm1, sm2 = _in_proj(x2d, lw["norm_mix"], lw["wcat"])
    yn, h_new = _ssd_prompt(xbc, sm2, z, lw, nb, seq)
    qcat, kcat, lat, kpe = _mla_prep(cq, ckv, sm1, sm2, tabs[0], tabs[1], lw, nb, seq)
    o = _attn_prompt(qcat, kcat, lw["w_vb"], nb, seq)
    x1 = _mix(x2d, gates, yn, o, lw)
    x2 = _peer(x1, lw, norm_final, final)
    conv_new = xbc.reshape(nb, seq, CONV_DIM)[:, seq - (CONV_WIDTH - 1):, :]
    return x2, h_new, conv_new, lat.reshape(nb, seq, KV_LORA), kpe.reshape(nb, seq, QK_ROPE)


def _sample_layer(x2d, layer, lw, tabs, state_ssm, state_conv, cache_latent, cache_krope, page_table,
                  norm_final, final):
    nb = x2d.shape[0]
    gates, z, xbc, cq, ckv, sm1, sm2 = _in_proj(x2d, lw["norm_mix"], lw["wcat"])
    prev = state_conv[layer]
    u, dt = _ssd_step_pre(xbc, jnp.transpose(prev, (1, 0, 2)), sm2, lw)
    to_t = lambda v: jnp.transpose(v.reshape(nb, SSM_HEADS, SSM_HEAD_DIM), (0, 2, 1))
    n_bc = SSM_GROUPS * D_STATE
    h_new, yt = _ssd_step(state_ssm, layer, to_t(u[:, :D_INNER]), to_t(z),
                          u[:, D_INNER:D_INNER + n_bc].reshape(nb, 1, n_bc),
                          u[:, D_INNER + n_bc:].reshape(nb, 1, n_bc), dt.reshape(nb, 1, LANES), lw)
    yn = jnp.transpose(yt, (0, 2, 1)).reshape(nb, D_INNER).astype(BF16)
    conv_new = jnp.concatenate([prev[:, 1:], xbc[:, None, :]], axis=1)

    qcat, kcat, lat, kpe = _mla_prep(cq, ckv, sm1, sm2, tabs[0], tabs[1], lw, 1, nb)
    qdec = jnp.transpose(qcat[0], (1, 0, 2))
    o = _attn_decode(page_table, qdec, kcat.reshape(nb, 1, QK_CAT), lw["w_vb_all"], cache_latent, cache_krope, layer)
    x1 = _mix(x2d, gates, yn, o.reshape(nb, MLA_HEADS * V_HEAD), lw)
    pad = (-nb) % LANES
    x2 = _peer(jnp.pad(x1, ((0, pad), (0, 0))), lw, norm_final, final)[:nb]
    return x2, h_new, conv_new, lat.reshape(nb, 1, KV_LORA), kpe.reshape(nb, 1, QK_ROPE)


def kernel(x_prompt, x_sample, cache_latent, cache_krope, state_ssm, state_conv, page_table, norm_mix, w_in, b_gate,
           conv_w, conv_b, dt_bias, a_log, d_skip, ssm_norm, w_ssm_proj, q_norm, w_qb, kv_norm, w_kvb, w_mla_proj,
           w_out, norm_ffn, peer_wq, peer_keys, peer_u, peer_v, norm_final):
    bp, sp = x_prompt.shape[:2]
    bs, ss = x_sample.shape[:2]
    depth = w_in.shape[0]
    assert ss == 1 and sp % SSD_CHUNK == 0
    past_len = page_table.shape[1] * cache_latent.shape[2]
    tabs_p = _rope_tables(jnp.arange(sp, dtype=I32))
    tabs_s = _rope_tables(jnp.full((bs,), past_len, I32))
    nf = norm_final.reshape(1, D_MODEL)
    xp = x_prompt.reshape(bp * sp, D_MODEL)
    xs = x_sample.reshape(bs, D_MODEL)
    outs_p, outs_s = [], []
    for l in range(depth):
        lw = _layer_weights(l, norm_mix, w_in, b_gate, conv_w, conv_b, dt_bias, a_log, d_skip, ssm_norm, w_ssm_proj,
                            q_norm, w_qb, kv_norm, w_kvb, w_mla_proj, w_out, norm_ffn, peer_wq, peer_keys, peer_u,
                            peer_v)
        final = l == depth - 1
        xp, hp, cp, lp, kp = _prompt_layer(xp, bp, sp, lw, tabs_p, nf, final)
        xs, hs, cs, lsn, ksn = _sample_layer(xs, l, lw, tabs_s, state_ssm, state_conv, cache_latent, cache_krope,
                                             page_table, nf, final)
        outs_p.append((lp, kp, hp, cp))
        outs_s.append((lsn, ksn, hs, cs))
    stack = lambda outs, i: jnp.stack([o[i] for o in outs])
    return (xp.reshape(bp, sp, D_MODEL), xs.reshape(bs, ss, D_MODEL),
            stack(outs_p, 0), stack(outs_p, 1), stack(outs_p, 2), stack(outs_p, 3),
            stack(outs_s, 0), stack(outs_s, 1), stack(outs_s, 2), stack(outs_s, 3))
```

```python
import functools

import numpy as np
import jax
import jax.numpy as jnp
from jax import lax
from jax.experimental import pallas as pl
from jax.experimental.pallas import tpu as pltpu

F32 = jnp.float32
BF16 = jnp.bfloat16
I32 = jnp.int32

D_MODEL = 1024
D_INNER = 1024
SSM_HEAD_DIM = 64
SSM_HEADS = 16
SSM_GROUPS = 2
SSM_HPG = 8
D_STATE = 128
CONV_WIDTH = 4
CONV_DIM = D_INNER + 2 * SSM_GROUPS * D_STATE
SSD_CHUNK = 128
MLA_HEADS = 8
QK_NOPE = 64
QK_ROPE = 32
V_HEAD = 64
Q_LORA = 384
KV_LORA = 256
ROPE_THETA = 10000.0
ATTN_SCALE = (QK_NOPE + QK_ROPE) ** -0.5
PEER_HEADS = 8
N_KEYS = 128
N_EXPERTS = N_KEYS * N_KEYS
PEER_D_KEY = 256
PEER_TOPK = 16
EPS = 1e-6

LANES = 128
QK_CAT = KV_LORA + LANES
DT_LANE = 64
VMEM_LIMIT = 56 * 1024 * 1024

_HI = lax.Precision.HIGHEST


def _cparams(*sem):
    return pltpu.CompilerParams(dimension_semantics=sem, vmem_limit_bytes=VMEM_LIMIT)


def _dot(a, b):
    return jnp.dot(a, b, preferred_element_type=F32)


def _dot_nt(a, b):
    return lax.dot_general(a, b, (((1,), (1,)), ((), ())), preferred_element_type=F32)


def _dot_tn(a, b):
    return lax.dot_general(a, b, (((0,), (0,)), ((), ())), preferred_element_type=F32)


def _rms(x, w):
    ms = jnp.mean(x * x, axis=-1, keepdims=True)
    return (x * lax.rsqrt(ms + EPS)) * w


def _sigmoid(x):
    return jax.nn.sigmoid(x)


def _softplus(x):
    return jnp.maximum(x, 0.0) + jnp.log1p(jnp.exp(-jnp.abs(x)))


def _row_block(t, cap):
    if t <= cap:
        return t
    b = cap
    while t % b:
        b //= 2
    return b


_IN_WIDTHS = (2 * D_MODEL, D_INNER, CONV_DIM, Q_LORA, KV_LORA, LANES, LANES)


def _in_proj_body(x_ref, nw_ref, w_ref, *out_refs):
    h = _rms(x_ref[...], nw_ref[...]).astype(BF16)
    off = 0
    for ref, wd in zip(out_refs, _IN_WIDTHS):
        ref[...] = _dot(h, w_ref[:, off:off + wd])
        off += wd


def _in_proj(x2d, nw, wcat):
    t = x2d.shape[0]
    tm = _row_block(t, 256)
    wtot = sum(_IN_WIDTHS)
    return pl.pallas_call(
        _in_proj_body,
        grid=(t // tm,),
        in_specs=[pl.BlockSpec((tm, D_MODEL), lambda i: (i, 0)),
                  pl.BlockSpec((1, D_MODEL), lambda i: (0, 0)),
                  pl.BlockSpec((D_MODEL, wtot), lambda i: (0, 0))],
        out_specs=[pl.BlockSpec((tm, wd), lambda i: (i, 0)) for wd in _IN_WIDTHS],
        out_shape=[jax.ShapeDtypeStruct((t, wd), F32) for wd in _IN_WIDTHS],
        compiler_params=_cparams("parallel"),
        name="in_proj",
    )(x2d, nw, wcat)


def _head_expand():
    r = lax.broadcasted_iota(I32, (LANES, D_INNER), 0)
    c = lax.broadcasted_iota(I32, (LANES, D_INNER), 1)
    return (jnp.right_shift(c, 6) == (r - DT_LANE)).astype(F32)


def _ssd_prompt_body(nc, xbc_ref, sm_ref, z_ref, cw_ref, cb_ref, dtb_ref, alog_ref, dsk_ref, nrm_ref,
                     yn_ref, hfin_ref, xpad_ref, st_ref):
    c = pl.program_id(1)
    cl = SSD_CHUNK

    @pl.when(c == 0)
    def _():
        xpad_ref[0:8, :] = jnp.zeros((8, CONV_DIM), F32)
        st_ref[...] = jnp.zeros_like(st_ref)

    xpad_ref[8:8 + cl, :] = xbc_ref[...]
    conv = cb_ref[...]
    for k in range(CONV_WIDTH):
        conv = conv + xpad_ref[5 + k:5 + k + cl, :] * cw_ref[k:k + 1, :]
    xpad_ref[0:8, :] = xpad_ref[cl:cl + 8, :]
    u = conv * _sigmoid(conv)
    xs = u[:, :D_INNER]
    bb = u[:, D_INNER:D_INNER + SSM_GROUPS * D_STATE].astype(BF16)
    cb = u[:, D_INNER + SSM_GROUPS * D_STATE:].astype(BF16)

    row = lax.broadcasted_iota(I32, (cl, cl), 0)
    col = lax.broadcasted_iota(I32, (cl, cl), 1)
    dt_ok = (col >= DT_LANE) & (col < DT_LANE + SSM_HEADS)
    dt = jnp.where(dt_ok, _softplus(sm_ref[...] + dtb_ref[...]), 0.0)
    a = dt * (-jnp.exp(alog_ref[...]))
    tri = (row >= col).astype(F32)
    a_cs = jnp.dot(tri, a, precision=_HI, preferred_element_type=F32)
    a_cs_t = a_cs.T
    ea = jnp.exp(a_cs)
    dte = jnp.exp(a_cs[cl - 1:cl, :] - a_cs)
    ex = _head_expand()
    dt_e = jnp.dot(dt, ex, precision=_HI, preferred_element_type=F32)
    ea_e = jnp.dot(ea, ex, precision=_HI, preferred_element_type=F32)
    dte_e = jnp.dot(dte, ex, precision=_HI, preferred_element_type=F32)

    xdt = xs * dt_e
    xdt_b = xdt.astype(BF16)
    causal = row >= col
    ys = []
    for g in range(SSM_GROUPS):
        cbm = _dot_nt(cb[:, g * D_STATE:(g + 1) * D_STATE], bb[:, g * D_STATE:(g + 1) * D_STATE])
        for e in range(SSM_HPG):
            h = g * SSM_HPG + e
            seg = a_cs[:, DT_LANE + h:DT_LANE + h + 1] - a_cs_t[DT_LANE + h:DT_LANE + h + 1, :]
            lm = jnp.exp(jnp.where(causal, seg, -jnp.inf))
            ys.append(_dot((cbm * lm).astype(BF16), xdt_b[:, h * SSM_HEAD_DIM:(h + 1) * SSM_HEAD_DIM]))
    y_diag = jnp.concatenate(ys, axis=1)

    xw = (xdt * dte_e).astype(BF16)
    dec = ea_e[cl - 1:cl, :]
    half = D_INNER // SSM_GROUPS
    yoffs = []
    for g in range(SSM_GROUPS):
        s_prev = st_ref[g]
        yoffs.append(_dot(cb[:, g * D_STATE:(g + 1) * D_STATE], s_prev.astype(BF16)))
        st_ref[g] = dec[:, g * half:(g + 1) * half] * s_prev + _dot_tn(
            bb[:, g * D_STATE:(g + 1) * D_STATE], xw[:, g * half:(g + 1) * half])
    y = y_diag + jnp.concatenate(yoffs, axis=1) * ea_e + dsk_ref[...] * xs
    zv = z_ref[...]
    y = y * (zv * _sigmoid(zv))
    outs = []
    for g in range(SSM_GROUPS):
        outs.append(_rms(y[:, g * half:(g + 1) * half], nrm_ref[:, g * half:(g + 1) * half]))
    yn_ref[...] = jnp.concatenate(outs, axis=1).astype(BF16)

    @pl.when(c == nc - 1)
    def _():
        for g in range(SSM_GROUPS):
            s_t = st_ref[g].T
            for e in range(SSM_HPG):
                hfin_ref[0, g, e] = s_t[e * SSM_HEAD_DIM:(e + 1) * SSM_HEAD_DIM, :]


def _ssd_prompt(xbc, sm2, z, lw, nb, seq):
    nc = seq // SSD_CHUNK
    cl = SSD_CHUNK
    rowmap = lambda b, c: (b * nc + c, 0)
    full = lambda b, c: (0, 0)
    return pl.pallas_call(
        functools.partial(_ssd_prompt_body, nc),
        grid=(nb, nc),
        in_specs=[pl.BlockSpec((cl, CONV_DIM), rowmap),
                  pl.BlockSpec((cl, LANES), rowmap),
                  pl.BlockSpec((cl, D_INNER), rowmap),
                  pl.BlockSpec((CONV_WIDTH, CONV_DIM), full),
                  pl.BlockSpec((1, CONV_DIM), full),
                  pl.BlockSpec((1, LANES), full),
                  pl.BlockSpec((1, LANES), full),
                  pl.BlockSpec((1, D_INNER), full),
                  pl.BlockSpec((1, D_INNER), full)],
        out_specs=[pl.BlockSpec((cl, D_INNER), rowmap),
                   pl.BlockSpec((1, SSM_GROUPS, SSM_HPG, SSM_HEAD_DIM, D_STATE), lambda b, c: (b, 0, 0, 0, 0))],
        out_shape=[jax.ShapeDtypeStruct((nb * seq, D_INNER), BF16),
                   jax.ShapeDtypeStruct((nb, SSM_GROUPS, SSM_HPG, SSM_HEAD_DIM, D_STATE), F32)],
        scratch_shapes=[pltpu.VMEM((cl + 8, CONV_DIM), F32),
                        pltpu.VMEM((SSM_GROUPS, D_STATE, D_INNER // SSM_GROUPS), F32)],
        compiler_params=_cparams("parallel", "arbitrary"),
        name="ssd_prompt",
    )(xbc, sm2, z, lw["conv_w"], lw["conv_b"], lw["dt_bias_p"], lw["a_log_p"], lw["d_skip_e"], lw["ssm_norm"])


def _ssd_step_pre_body(xbc_ref, prev_ref, sm_ref, cw_ref, cb_ref, dtb_ref, u_ref, dt_ref):
    conv = cb_ref[...]
    for k in range(CONV_WIDTH - 1):
        conv = conv + prev_ref[k] * cw_ref[k:k + 1, :]
    conv = conv + xbc_ref[...] * cw_ref[CONV_WIDTH - 1:CONV_WIDTH, :]
    u_ref[...] = conv * _sigmoid(conv)
    dt_ref[...] = _softplus(sm_ref[...] + dtb_ref[...])


def _ssd_step_pre(xbc, prev_t, sm2, lw):
    nb = xbc.shape[0]
    return pl.pallas_call(
        _ssd_step_pre_body,
        out_shape=[jax.ShapeDtypeStruct((nb, CONV_DIM), F32), jax.ShapeDtypeStruct((nb, LANES), F32)],
        compiler_params=pltpu.CompilerParams(vmem_limit_bytes=VMEM_LIMIT),
        name="ssd_step_pre",
    )(xbc, prev_t, sm2, lw["conv_w"], lw["conv_b"], lw["dt_bias_p"])


def _ssd_step_body(h0_ref, xt_ref, zt_ref, b_ref, c_ref, dt_ref, alog_ref, dsk_ref, nrm_ref, hn_ref, yt_ref):
    xt = xt_ref[0]
    dtv = dt_ref[0]
    av = -jnp.exp(alog_ref[...])
    lane = lax.broadcasted_iota(I32, (SSM_HEAD_DIM, SSM_HEADS), 1)
    y = jnp.zeros((SSM_HEAD_DIM, SSM_HEADS), F32)
    for g in range(SSM_GROUPS):
        bg = b_ref[0][:, g * D_STATE:(g + 1) * D_STATE]
        cg = c_ref[0][:, g * D_STATE:(g + 1) * D_STATE]
        for e in range(SSM_HPG):
            h = g * SSM_HPG + e
            dth = dtv[:, DT_LANE + h:DT_LANE + h + 1]
            dec = jnp.exp(dth * av[:, DT_LANE + h:DT_LANE + h + 1])
            hn = dec * h0_ref[0, g, e] + (xt[:, h:h + 1] * dth) * bg
            hn_ref[0, g, e] = hn
            ycol = jnp.sum(hn * cg, axis=1, keepdims=True)
            y = jnp.where(lane == h, ycol, y)
    y = y + dsk_ref[...] * xt
    zv = zt_ref[0]
    y = y * (zv * _sigmoid(zv))
    sq = y * y
    in_g0 = lane < SSM_HPG
    n_g = float(D_INNER // SSM_GROUPS)
    ms0 = jnp.sum(jnp.sum(jnp.where(in_g0, sq, 0.0), axis=1, keepdims=True), axis=0, keepdims=True) / n_g
    ms1 = jnp.sum(jnp.sum(jnp.where(in_g0, 0.0, sq), axis=1, keepdims=True), axis=0, keepdims=True) / n_g
    scale = jnp.where(in_g0, lax.rsqrt(ms0 + EPS), lax.rsqrt(ms1 + EPS))
    yt_ref[0] = (y * scale) * nrm_ref[...]


def _ssd_step(h0_all, layer, xt, zt, bm, cm, dt, lw):
    nb = xt.shape[0]
    st_block = (None, 1, SSM_GROUPS, SSM_HPG, SSM_HEAD_DIM, D_STATE)
    pt_block = (1, SSM_HEAD_DIM, SSM_HEADS)
    vec = lambda n: pl.BlockSpec((1, 1, n), lambda b: (b, 0, 0))
    full = lambda r, n: pl.BlockSpec((r, n), lambda b: (0, 0))
    return pl.pallas_call(
        _ssd_step_body,
        grid=(nb,),
        in_specs=[pl.BlockSpec(st_block, lambda b: (layer, b, 0, 0, 0, 0)),
                  pl.BlockSpec(pt_block, lambda b: (b, 0, 0)),
                  pl.BlockSpec(pt_block, lambda b: (b, 0, 0)),
                  vec(SSM_GROUPS * D_STATE), vec(SSM_GROUPS * D_STATE), vec(LANES),
                  full(1, LANES), full(1, SSM_HEADS), full(SSM_HEAD_DIM, SSM_HEADS)],
        out_specs=[pl.BlockSpec((1, SSM_GROUPS, SSM_HPG, SSM_HEAD_DIM, D_STATE), lambda b: (b, 0, 0, 0, 0)),
                   pl.BlockSpec(pt_block, lambda b: (b, 0, 0))],
        out_shape=[jax.ShapeDtypeStruct((nb, SSM_GROUPS, SSM_HPG, SSM_HEAD_DIM, D_STATE), F32),
                   jax.ShapeDtypeStruct((nb, SSM_HEAD_DIM, SSM_HEADS), F32)],
        compiler_params=_cparams("parallel"),
        name="ssd_step",
    )(h0_all, xt, zt, bm, cm, dt, lw["a_log_p"], lw["d_skip_row"], lw["ssm_norm_t"])


def _mla_prep_body(cq_ref, ckv_ref, sm1_ref, sm2_ref, qtab_ref, ktab_ref, qn_ref, kvn_ref,
                   wn_ref, wpe_ref, wpes_ref, wkb_ref, qcat_ref, kcat_ref, lat_ref, kpe_ref):
    hq = _rms(cq_ref[...], qn_ref[...]).astype(BF16)
    qn = _dot(hq, wn_ref[...]).astype(BF16)
    npe = MLA_HEADS * LANES
    qpe = (_dot(hq, wpe_ref[...]) * qtab_ref[:, :npe] + _dot(hq, wpes_ref[...]) * qtab_ref[:, npe:]) * ATTN_SCALE
    for h in range(MLA_HEADS):
        ql = _dot(qn[:, h * QK_NOPE:(h + 1) * QK_NOPE], wkb_ref[h]) * ATTN_SCALE
        qcat_ref[0, h] = jnp.concatenate([ql, qpe[:, h * LANES:(h + 1) * LANES]], axis=1).astype(BF16)
    lat = _rms(ckv_ref[...], kvn_ref[...])
    lat_ref[...] = lat
    kpe = sm1_ref[...] * ktab_ref[:, :LANES] + sm2_ref[...] * ktab_ref[:, LANES:]
    kpe_ref[...] = kpe[:, :QK_ROPE]
    kcat_ref[...] = jnp.concatenate([lat, kpe], axis=1).astype(BF16)


def _mla_prep(cq, ckv, sm1, sm2, qtab, ktab, lw, nb, seq):
    t = nb * seq
    tm = _row_block(seq, 256)
    nl = seq // tm
    rowmap = lambda b, i: (b * nl + i, 0)
    tabmap = lambda b, i: (i, 0)
    full2 = lambda b, i: (0, 0)
    npe = MLA_HEADS * LANES
    return pl.pallas_call(
        _mla_prep_body,
        grid=(nb, nl),
        in_specs=[pl.BlockSpec((tm, Q_LORA), rowmap),
                  pl.BlockSpec((tm, KV_LORA), rowmap),
                  pl.BlockSpec((tm, LANES), rowmap),
                  pl.BlockSpec((tm, LANES), rowmap),
                  pl.BlockSpec((tm, 2 * npe), tabmap),
                  pl.BlockSpec((tm, 2 * LANES), tabmap),
                  pl.BlockSpec((1, Q_LORA), full2),
                  pl.BlockSpec((1, KV_LORA), full2),
                  pl.BlockSpec((Q_LORA, MLA_HEADS * QK_NOPE), full2),
                  pl.BlockSpec((Q_LORA, npe), full2),
                  pl.BlockSpec((Q_LORA, npe), full2),
                  pl.BlockSpec((MLA_HEADS, QK_NOPE, KV_LORA), lambda b, i: (0, 0, 0))],
        out_specs=[pl.BlockSpec((1, MLA_HEADS, tm, QK_CAT), lambda b, i: (b, 0, i, 0)),
                   pl.BlockSpec((tm, QK_CAT), rowmap),
                   pl.BlockSpec((tm, KV_LORA), rowmap),
                   pl.BlockSpec((tm, QK_ROPE), rowmap)],
        out_shape=[jax.ShapeDtypeStruct((nb, MLA_HEADS, seq, QK_CAT), BF16),
                   jax.ShapeDtypeStruct((t, QK_CAT), BF16),
                   jax.ShapeDtypeStruct((t, KV_LORA), F32),
                   jax.ShapeDtypeStruct((t, QK_ROPE), F32)],
        compiler_params=_cparams("parallel", "parallel"),
        name="mla_prep",
    )(cq, ckv, sm1, sm2, qtab, ktab, lw["q_norm"], lw["kv_norm"], lw["wq_nope"], lw["wq_pe"], lw["wq_pe_sw"],
      lw["w_kb_t"])


def _attn_prompt_body(tq, tk, nk, q_ref, k_ref, wvb_ref, o_ref, m_ref, l_ref, acc_ref):
    i = pl.program_id(1)
    j = pl.program_id(2)

    @pl.when(j == 0)
    def _():
        m_ref[...] = jnp.full(m_ref.shape, -jnp.inf, F32)
        l_ref[...] = jnp.zeros(l_ref.shape, F32)
        acc_ref[...] = jnp.zeros(acc_ref.shape, F32)

    @pl.when(j * tk <= i * tq + tq - 1)
    def _():
        k = k_ref[...]
        v = k[:, :KV_LORA]
        qpos = i * tq + lax.broadcasted_iota(I32, (tq, tk), 0)
        kpos = j * tk + lax.broadcasted_iota(I32, (tq, tk), 1)
        visible = kpos <= qpos
        ahead = 2
        s_q = [_dot_nt(q_ref[0, h], k) for h in range(ahead)]
        for h in range(MLA_HEADS):
            hs = slice(h * tq, (h + 1) * tq)
            s = jnp.where(visible, s_q.pop(0), -jnp.inf)
            if h + ahead < MLA_HEADS:
                s_q.append(_dot_nt(q_ref[0, h + ahead], k))
            m_prev = m_ref[hs]
            m_new = jnp.maximum(m_prev, jnp.max(s, axis=1, keepdims=True))
            p = jnp.exp(s - m_new)
            alpha = jnp.exp(m_prev - m_new)
            l_ref[hs] = alpha * l_ref[hs] + jnp.sum(p, axis=1, keepdims=True)
            acc_ref[hs] = alpha * acc_ref[hs] + _dot(p.astype(BF16), v)
            m_ref[hs] = m_new

    @pl.when(j == nk - 1)
    def _():
        outs = []
        for h in range(MLA_HEADS):
            hs = slice(h * tq, (h + 1) * tq)
            outs.append(_dot((acc_ref[hs] / l_ref[hs]).astype(BF16), wvb_ref[h]))
        o_ref[...] = jnp.concatenate(outs, axis=1).astype(BF16)


def _attn_prompt(qcat, kcat, w_vb, nb, seq):
    tq = _row_block(seq, 256)
    tk = _row_block(seq, 512)
    nq, nk = seq // tq, seq // tk

    def kmap(b, i, j):
        return (b * nk + jnp.minimum(j, (i * tq + tq - 1) // tk), 0)

    return pl.pallas_call(
        functools.partial(_attn_prompt_body, tq, tk, nk),
        grid=(nb, nq, nk),
        in_specs=[pl.BlockSpec((1, MLA_HEADS, tq, QK_CAT), lambda b, i, j: (b, 0, i, 0)),
                  pl.BlockSpec((tk, QK_CAT), kmap),
                  pl.BlockSpec((MLA_HEADS, KV_LORA, V_HEAD), lambda b, i, j: (0, 0, 0))],
        out_specs=pl.BlockSpec((tq, MLA_HEADS * V_HEAD), lambda b, i, j: (b * nq + i, 0)),
        out_shape=jax.ShapeDtypeStruct((nb * seq, MLA_HEADS * V_HEAD), BF16),
        scratch_shapes=[pltpu.VMEM((MLA_HEADS * tq, 1), F32), pltpu.VMEM((MLA_HEADS * tq, 1), F32),
                        pltpu.VMEM((MLA_HEADS * tq, KV_LORA), F32)],
        compiler_params=_cparams("parallel", "parallel", "arbitrary"),
        name="attn_prompt",
    )(qcat, kcat, w_vb)


def _attn_decode_body(pg, nj, pt_ref, q_ref, kc_ref, wvb_ref, *refs):
    lat_refs = refs[:pg]
    kr_refs = refs[pg:2 * pg]
    o_ref, m_ref, l_ref, acc_ref = refs[2 * pg:]
    j = pl.program_id(1)

    @pl.when(j == 0)
    def _():
        m_ref[...] = jnp.full(m_ref.shape, -jnp.inf, F32)
        l_ref[...] = jnp.zeros(l_ref.shape, F32)
        acc_ref[...] = jnp.zeros(acc_ref.shape, F32)

    q = q_ref[0]
    ql = q[:, :KV_LORA]
    qr = q[:, KV_LORA:KV_LORA + QK_ROPE]
    kvs = [lat_refs[i][...].astype(BF16) for i in range(pg)]
    ss = [_dot_nt(ql, kvs[i]) + _dot_nt(qr, kr_refs[i][...].astype(BF16)) for i in range(pg)]
    m_prev = m_ref[...]
    m_blk = ss[0]
    for i in range(1, pg):
        m_blk = jnp.maximum(m_blk, ss[i])
    m_new = jnp.maximum(m_prev, jnp.max(m_blk, axis=1, keepdims=True))
    ps = [jnp.exp(ss[i] - m_new) for i in range(pg)]
    p_sum = ps[0]
    for i in range(1, pg):
        p_sum = p_sum + ps[i]
    pvs = [_dot(ps[i].astype(BF16), kvs[i]) for i in range(pg)]
    while len(pvs) > 1:
        pvs = [pvs[i] + pvs[i + 1] for i in range(0, len(pvs), 2)]
    alpha = jnp.exp(m_prev - m_new)
    l_ref[...] = alpha * l_ref[...] + jnp.sum(p_sum, axis=1, keepdims=True)
    acc_ref[...] = alpha * acc_ref[...] + pvs[0]
    m_ref[...] = m_new

    @pl.when(j == nj - 1)
    def _():
        kc = kc_ref[0].astype(F32)
        s = jnp.sum(q.astype(F32) * kc, axis=1, keepdims=True)
        m_prev = m_ref[...]
        m_new = jnp.maximum(m_prev, s)
        p = jnp.exp(s - m_new)
        alpha = jnp.exp(m_prev - m_new)
        l_fin = alpha * l_ref[...] + p
        acc = alpha * acc_ref[...] + p.astype(BF16).astype(F32) * kc[:, :KV_LORA]
        o = (acc / l_fin).astype(BF16)
        full = _dot(o, wvb_ref[...])
        hrow = lax.broadcasted_iota(I32, full.shape, 0)
        hcol = jnp.right_shift(lax.broadcasted_iota(I32, full.shape, 1), 6)
        o_ref[0] = jnp.sum(jnp.where(hrow == hcol, full, 0.0), axis=0, keepdims=True).astype(BF16)


def _attn_decode(page_table, qdec, kcur, w_vb_all, cache_latent, cache_krope, layer):
    nb, n_pages = page_table.shape
    page = cache_latent.shape[2]
    pg = 16
    while n_pages % pg:
        pg //= 2
    nj = n_pages // pg

    def pmap(i):
        return lambda b, j, pt: (layer, pt[b, j * pg + i], 0, 0)

    in_specs = [pl.BlockSpec((1, MLA_HEADS, QK_CAT), lambda b, j, pt: (b, 0, 0)),
                pl.BlockSpec((1, 1, QK_CAT), lambda b, j, pt: (b, 0, 0)),
                pl.BlockSpec((KV_LORA, MLA_HEADS * V_HEAD), lambda b, j, pt: (0, 0))]
    in_specs += [pl.BlockSpec((None, None, page, KV_LORA), pmap(i)) for i in range(pg)]
    in_specs += [pl.BlockSpec((None, None, page, QK_ROPE), pmap(i)) for i in range(pg)]
    grid_spec = pltpu.PrefetchScalarGridSpec(
        num_scalar_prefetch=1,
        grid=(nb, nj),
        in_specs=in_specs,
        out_specs=pl.BlockSpec((1, 1, MLA_HEADS * V_HEAD), lambda b, j, pt: (b, 0, 0)),
        scratch_shapes=[pltpu.VMEM((MLA_HEADS, 1), F32), pltpu.VMEM((MLA_HEADS, 1), F32),
                        pltpu.VMEM((MLA_HEADS, KV_LORA), F32)])
    return pl.pallas_call(
        functools.partial(_attn_decode_body, pg, nj),
        grid_spec=grid_spec,
        out_shape=jax.ShapeDtypeStruct((nb, 1, MLA_HEADS * V_HEAD), BF16),
        compiler_params=_cparams("parallel", "arbitrary"),
        name="attn_decode",
    )(page_table, qdec, kcur, w_vb_all, *([cache_latent] * pg), *([cache_krope] * pg))


def _mix_body(x_ref, g_ref, bg_ref, yn_ref, o_ref, wssm_ref, wmla_ref, wout_ref, x1_ref):
    ya = _dot(yn_ref[...], wssm_ref[...])
    yb = _dot(o_ref[...], wmla_ref[...])
    gate = _sigmoid(g_ref[...] + bg_ref[...])
    mix = gate[:, :D_MODEL] * ya + gate[:, D_MODEL:] * yb
    x1_ref[...] = x_ref[...] + _dot(mix.astype(BF16), wout_ref[...])


def _mix(x2d, gates, yn, o, lw):
    t = x2d.shape[0]
    tm = _row_block(t, 256)
    row = lambda n: pl.BlockSpec((tm, n), lambda i: (i, 0))
    full = lambda r, n: pl.BlockSpec((r, n), lambda i: (0, 0))
    return pl.pallas_call(
        _mix_body,
        grid=(t // tm,),
        in_specs=[row(D_MODEL), row(2 * D_MODEL), full(1, 2 * D_MODEL), row(D_INNER), row(MLA_HEADS * V_HEAD),
                  full(D_INNER, D_MODEL), full(MLA_HEADS * V_HEAD, D_MODEL), full(D_MODEL, D_MODEL)],
        out_specs=row(D_MODEL),
        out_shape=jax.ShapeDtypeStruct((t, D_MODEL), F32),
        compiler_params=_cparams("parallel"),
        name="mix",
    )(x2d, gates, lw["b_gate"], yn, o, lw["w_ssm_proj"], lw["w_mla_proj"], lw["w_out"])


def _peer_scores_body(x_ref, nw_ref, wq_ref, keys_ref, hn_ref, s1_ref, s2_ref):
    hn = _rms(x_ref[...], nw_ref[...]).astype(BF16)
    hn_ref[...] = hn
    q = _dot(hn, wq_ref[...]).astype(BF16)
    half = PEER_D_KEY // 2
    for h in range(PEER_HEADS):
        s1_ref[h] = _dot_nt(keys_ref[h, 0], q[:, h * PEER_D_KEY:h * PEER_D_KEY + half])
        s2_ref[h] = _dot_nt(keys_ref[h, 1], q[:, h * PEER_D_KEY + half:(h + 1) * PEER_D_KEY])


def _peer_scores(x1, lw):
    t = x1.shape[0]
    tm = _row_block(t, 256)
    sblock = pl.BlockSpec((PEER_HEADS, N_KEYS, tm), lambda i: (0, 0, i))
    return pl.pallas_call(
        _peer_scores_body,
        grid=(t // tm,),
        in_specs=[pl.BlockSpec((tm, D_MODEL), lambda i: (i, 0)),
                  pl.BlockSpec((1, D_MODEL), lambda i: (0, 0)),
                  pl.BlockSpec((D_MODEL, PEER_HEADS * PEER_D_KEY), lambda i: (0, 0)),
                  pl.BlockSpec((PEER_HEADS, 2, N_KEYS, PEER_D_KEY // 2), lambda i: (0, 0, 0, 0))],
        out_specs=[pl.BlockSpec((tm, D_MODEL), lambda i: (i, 0)), sblock, sblock],
        out_shape=[jax.ShapeDtypeStruct((t, D_MODEL), BF16),
                   jax.ShapeDtypeStruct((PEER_HEADS, N_KEYS, t), F32),
                   jax.ShapeDtypeStruct((PEER_HEADS, N_KEYS, t), F32)],
        compiler_params=_cparams("parallel"),
        name="peer_scores",
    )(x1, lw["norm_ffn"], lw["peer_wq"], lw["peer_keys"])


SUBLANES = 8


def _top16_rows(s):
    n = s.shape[0]
    row = lax.broadcasted_iota(I32, s.shape, 0)
    r16 = lax.broadcasted_iota(I32, (PEER_TOPK, LANES), 0)

    def body(r, carry):
        rank, vals = carry
        alive = rank == PEER_TOPK
        m = jnp.max(jnp.where(alive, s, -jnp.inf), axis=0, keepdims=True)
        idx = jnp.min(jnp.where(alive & (s == m), row, n), axis=0, keepdims=True)
        rank = jnp.where(row == idx, r, rank)
        vals = jnp.where(r16 == r, m, vals)
        return rank, vals

    return lax.fori_loop(0, PEER_TOPK, body,
                         (jnp.full(s.shape, PEER_TOPK, I32), jnp.zeros((PEER_TOPK, LANES), F32)))


def _batcher_net(n):
    def merge(lo, hi, r):
        step = r * 2
        if step < hi - lo:
            yield from merge(lo, hi, step)
            yield from merge(lo + r, hi, step)
            yield from [(i, i + r) for i in range(lo + r, hi - r, step)]
        else:
            yield (lo, lo + r)

    def sort(lo, hi):
        if hi - lo >= 1:
            mid = lo + (hi - lo) // 2
            yield from sort(lo, mid)
            yield from sort(mid + 1, hi)
            yield from merge(lo, hi, 1)

    return tuple(sort(0, n - 1))


_NET16 = _batcher_net(PEER_TOPK)


def _cmpx(v, x, i, j):
    a, b = v[i], v[j]
    if b is None:
        return
    if a is None:
        v[i], v[j] = b, None
        if x is not None:
            x[i], x[j] = x[j], x[i]
        return
    v[i], v[j] = jnp.maximum(a, b), jnp.minimum(a, b)
    if x is not None:
        first = a >= b
        x[i], x[j] = jnp.where(first, x[i], x[j]), jnp.where(first, x[j], x[i])


def _top16_sorted(groups, payload=None):
    pad = PEER_TOPK - len(groups)
    v = list(groups) + [None] * pad
    x = None if payload is None else list(payload) + [None] * pad
    for i, j in _NET16:
        _cmpx(v, x, i, j)
    for shift in (4, 2, 1):
        roll = lambda y: None if y is None else pltpu.roll(y, shift, 0)
        b = [roll(y) for y in v]
        bx = None if x is None else [roll(y) for y in x]
        c, cx = [], (None if x is None else [])
        for i in range(PEER_TOPK):
            p, q = v[i], b[PEER_TOPK - 1 - i]
            if p is None or q is None:
                c.append(q if p is None else p)
                if x is not None:
                    cx.append(bx[PEER_TOPK - 1 - i] if p is None else x[i])
            else:
                c.append(jnp.maximum(p, q))
                if x is not None:
                    cx.append(jnp.where(p >= q, x[i], bx[PEER_TOPK - 1 - i]))
        for d in (8, 4, 2, 1):
            for i in range(PEER_TOPK):
                if not i & d:
                    _cmpx(c, cx, i, i + d)
        v, x = c, cx
    return (v, x) if payload is not None else v


def _colsum(x):
    x = x + pltpu.roll(x, 4, 0)
    x = x + pltpu.roll(x, 2, 0)
    return x + pltpu.roll(x, 1, 0)


def _picks_by_value(s1, s2):
    k = PEER_TOPK
    ng = N_KEYS // SUBLANES
    sub = lax.broadcasted_iota(I32, (SUBLANES, LANES), 0)
    subf = sub.astype(F32)
    one = lambda m: jnp.where(m, 1.0, 0.0)
    c1 = [s1[SUBLANES * v:SUBLANES * (v + 1)] for v in range(ng)]
    c2 = [s2[SUBLANES * v:SUBLANES * (v + 1)] for v in range(ng)]
    ids = [subf + float(SUBLANES * v) for v in range(ng)]
    t1, i1 = _top16_sorted(c1, ids)
    t2, i2 = _top16_sorted(c2, ids)

    def as_rows(t, base):
        out = t[base]
        for r in range(1, SUBLANES):
            out = jnp.where(sub == r, t[base + r], out)
        return out

    sc2_lo, sc2_hi, sc1_hi = as_rows(t2, 0), as_rows(t2, SUBLANES), as_rows(t1, SUBLANES)
    cand = [t1[0] + sc2_lo, t1[0] + sc2_hi] + [t1[r] + sc2_lo for r in range(1, SUBLANES)] + [sc1_hi + t2[0]]
    tau = _top16_sorted(cand)[k - 1]
    sel = [c >= tau for c in cand]
    n_r = [_colsum(one(sel[0]) + one(sel[1]))] + [_colsum(one(sel[1 + r])) for r in range(1, SUBLANES)]
    n_hi = one(sel[9])
    n_r += [_colsum(jnp.where(sub == q, n_hi, 0.0)) for q in range(SUBLANES)]
    cmax = t1[0] + t2[0]
    zsum = jnp.where(sel[0], jnp.exp(cand[0] - cmax), 0.0)
    for g in range(1, len(cand)):
        zsum = zsum + jnp.where(sel[g], jnp.exp(cand[g] - cmax), 0.0)
    scale = 0.5 / _colsum(zsum)
    pre = [jnp.zeros((SUBLANES, LANES), F32)]
    for r in range(k):
        pre.append(pre[r] + n_r[r])
    bad = pre[k] != float(k)
    for r in range(k - 1):
        bad = bad | (t1[r] == t1[r + 1]) | (t2[r] == t2[r + 1])
    n_in1, n_in2 = one(c1[0] >= t1[k - 1]), one(c2[0] >= t2[k - 1])
    for v in range(1, ng):
        n_in1, n_in2 = n_in1 + one(c1[v] >= t1[k - 1]), n_in2 + one(c2[v] >= t2[k - 1])
    bad = bad | (_colsum(n_in1) != float(k)) | (_colsum(n_in2) != float(k))
    outs = ([], [], [])
    for half in range(k // SUBLANES):
        pick = subf + float(SUBLANES * half)
        r1p = jnp.zeros((SUBLANES, LANES), F32)
        base = jnp.zeros((SUBLANES, LANES), F32)
        for r in range(1, k):
            started = pre[r] <= pick
            r1p = jnp.where(started, float(r), r1p)
            base = jnp.where(started, pre[r], base)
        r2p = pick - base
        a_id, b_id, v1, v2 = i1[0], i2[0], t1[0], t2[0]
        for r in range(1, k):
            is1, is2 = r1p == float(r), r2p == float(r)
            a_id, v1 = jnp.where(is1, i1[r], a_id), jnp.where(is1, t1[r], v1)
            b_id, v2 = jnp.where(is2, i2[r], b_id), jnp.where(is2, t2[r], v2)
        outs[0].append(a_id)
        outs[1].append(b_id)
        outs[2].append(jnp.exp((v1 + v2) - cmax) * scale)
    return tuple(jnp.concatenate(o, axis=0) for o in outs) + (one(bad),)


def _picks_by_index(s1, s2):
    k = PEER_TOPK
    r16 = lax.broadcasted_iota(I32, (k, LANES), 0)
    rowf = lax.broadcasted_iota(I32, (N_KEYS, LANES), 0).astype(F32)
    rank1, sc1 = _top16_rows(s1)
    rank2, sc2 = _top16_rows(s2)

    def index_of_rank(rank):
        out = jnp.zeros((k, LANES), F32)
        for r in range(k):
            out = jnp.where(r16 == r, jnp.sum(jnp.where(rank == r, rowf, 0.0), axis=0, keepdims=True), out)
        return out

    i1, i2 = index_of_rank(rank1), index_of_rank(rank2)
    rep = lambda row, n: jnp.broadcast_to(row, (n, LANES))
    cand = jnp.concatenate([sc1[0:1] + sc2] + [sc1[r:r + 1] + sc2[0:8] for r in range(1, 8)] + [sc1[8:16] + sc2[0:1]],
                           axis=0)
    a_c = jnp.concatenate([rep(i1[0:1], k)] + [rep(i1[r:r + 1], 8) for r in range(1, 8)] + [i1[8:16]], axis=0)
    b_c = jnp.concatenate([i2] + [i2[0:8]] * 7 + [rep(i2[0:1], 8)], axis=0)
    rankc, _ = _top16_rows(cand)
    z = jnp.sum(jnp.where(rankc < k, jnp.exp(cand - cand[0:1]), 0.0), axis=0, keepdims=True)
    a_id = jnp.zeros((k, LANES), F32)
    b_id = jnp.zeros((k, LANES), F32)
    val = jnp.zeros((k, LANES), F32)
    for p in range(k):
        m = rankc == p
        pick = lambda c: jnp.sum(jnp.where(m, c, 0.0), axis=0, keepdims=True)
        a_id = jnp.where(r16 == p, pick(a_c), a_id)
        b_id = jnp.where(r16 == p, pick(b_c), b_id)
        val = jnp.where(r16 == p, pick(cand), val)
    return a_id, b_id, jnp.exp(val - cand[0:1]) * (0.5 / z)


def _peer_gates_body(s1_ref, s2_ref, a_ref, b_ref, g_ref, at_ref, bt_ref, gt_ref):
    k = PEER_TOPK

    def head(h, _):
        s1 = s1_ref[h]
        s2 = s2_ref[h]
        rows = pl.ds(pl.multiple_of(h * k, k), k)
        a_id, b_id, gate, bad = _picks_by_value(s1, s2)
        at_ref[rows, :] = a_id
        bt_ref[rows, :] = b_id
        gt_ref[rows, :] = gate

        @pl.when(jnp.max(bad) > 0.0)
        def _():
            at_ref[rows, :], bt_ref[rows, :], gt_ref[rows, :] = _picks_by_index(s1, s2)

        return 0

    lax.fori_loop(0, PEER_HEADS, head, 0)
    a_ref[...] = at_ref[...].T
    b_ref[...] = bt_ref[...].T
    g_ref[...] = gt_ref[...].T


def _peer_gates(s1t, s2t):
    t = s1t.shape[2]
    nslot = PEER_HEADS * PEER_TOPK
    blk = pl.BlockSpec((PEER_HEADS, N_KEYS, LANES), lambda i: (0, 0, i))
    oblk = pl.BlockSpec((LANES, nslot), lambda i: (i, 0))
    shp = jax.ShapeDtypeStruct((t, nslot), F32)
    return pl.pallas_call(
        _peer_gates_body,
        grid=(t // LANES,),
        in_specs=[blk, blk],
        out_specs=[oblk, oblk, oblk],
        out_shape=[shp, shp, shp],
        scratch_shapes=[pltpu.VMEM((nslot, LANES), F32)] * 3,
        compiler_params=_cparams("parallel"),
        name="peer_gates",
    )(s1t, s2t)


_SQRT_HALF = 0.7071067811865476
_HI16 = -65536
_GPAD = 8


def _peer_experts_body(tm, eb, nj, final, hn_ref, u_ref, v_ref, a_ref, b_ref, g_ref, x_ref, nf_ref,
                       o_ref, acc_ref, gw_ref):
    j = pl.program_id(1)
    pitch = tm + _GPAD
    half_a = N_KEYS // 2

    @pl.when(j == 0)
    def _():
        acc_ref[...] = jnp.zeros(acc_ref.shape, F32)
        key = lax.broadcasted_iota(I32, (N_KEYS, PEER_HEADS * PEER_TOPK), 0).astype(F32)

        def group(i, _):
            rows = pl.ds(pl.multiple_of(i * SUBLANES, SUBLANES), SUBLANES)
            a8, b8, g8 = a_ref[rows, :], b_ref[rows, :], g_ref[rows, :]
            for q in range(SUBLANES):
                pt = jnp.where(key == a8[q:q + 1], g8[q:q + 1], 0.0).astype(BF16)
                qt = jnp.where(key == b8[q:q + 1], 1.0, 0.0).astype(BF16)
                gt = _dot_nt(pt, qt).astype(BF16).astype(F32)
                hi = jnp.bitwise_and(pltpu.bitcast(gt[:half_a], I32), _HI16)
                lo = lax.shift_right_logical(pltpu.bitcast(gt[half_a:], I32), 16)
                gw_ref[pl.ds(i * SUBLANES + q, half_a, stride=pitch), :] = jnp.bitwise_or(hi, lo)
            return 0

        lax.fori_loop(0, tm // SUBLANES, group, 0, unroll=4)

    tiles = eb // N_KEYS
    a_lo = lax.rem(j * tiles, half_a)
    shift = jnp.where(j * tiles >= half_a, 16, 0)
    act = _dot_nt(hn_ref[...], u_ref[...])
    ws = []
    for al in range(tiles):
        word = gw_ref[pl.ds(pl.multiple_of((a_lo + al) * pitch, SUBLANES), tm), :]
        gate = pltpu.bitcast(jnp.bitwise_and(lax.shift_left(word, shift), _HI16), F32)
        x = act[:, al * N_KEYS:(al + 1) * N_KEYS]
        ws.append((gate * (x * (1.0 + lax.erf(x * _SQRT_HALF)))).astype(BF16))
    acc_ref[...] += _dot(jnp.concatenate(ws, axis=1), v_ref[...])

    @pl.when(j == nj - 1)
    def _():
        x2 = x_ref[...] + acc_ref[...]
        if final:
            o_ref[...] = _rms(x2, nf_ref[...])
        else:
            o_ref[...] = x2


def _peer_experts(hn, picks, x1, lw, norm_final, final):
    t = x1.shape[0]
    tm = _row_block(t, 512)
    eb = 8 * N_KEYS
    nj = N_EXPERTS // eb
    nslot = PEER_HEADS * PEER_TOPK
    a_id, b_id, gate = picks
    row = lambda n: pl.BlockSpec((tm, n), lambda i, j: (i, 0))
    return pl.pallas_call(
        functools.partial(_peer_experts_body, tm, eb, nj, final),
        grid=(t // tm, nj),
        in_specs=[row(D_MODEL),
                  pl.BlockSpec((eb, D_MODEL), lambda i, j: (j, 0)),
                  pl.BlockSpec((eb, D_MODEL), lambda i, j: (j, 0)),
                  row(nslot), row(nslot), row(nslot), row(D_MODEL),
                  pl.BlockSpec((1, D_MODEL), lambda i, j: (0, 0))],
        out_specs=row(D_MODEL),
        out_shape=jax.ShapeDtypeStruct((t, D_MODEL), F32),
        scratch_shapes=[pltpu.VMEM((tm, D_MODEL), F32),
                        pltpu.VMEM((N_KEYS // 2 * (tm + _GPAD), N_KEYS), I32)],
        compiler_params=_cparams("parallel", "arbitrary"),
        name="peer_experts",
    )(hn, lw["peer_u"], lw["peer_v"], a_id, b_id, gate, x1, norm_final)


def _peer(x1, lw, norm_final, final):
    hn, s1t, s2t = _peer_scores(x1, lw)
    picks = _peer_gates(s1t, s2t)
    return _peer_experts(hn, picks, x1, lw, norm_final, final)


def _pad_lanes(v, start, total=LANES):
    return jnp.zeros((1, total), F32).at[0, start:start + v.shape[0]].set(v)


def _layer_weights(l, norm_mix, w_in, b_gate, conv_w, conv_b, dt_bias, a_log, d_skip, ssm_norm, w_ssm_proj,
                   q_norm, w_qb, kv_norm, w_kvb, w_mla_proj, w_out, norm_ffn, peer_wq, peer_keys, peer_u, peer_v):
    o_dt = 3 * D_MODEL + CONV_DIM
    o_cq = o_dt + SSM_HEADS
    o_ckv = o_cq + Q_LORA
    o_kr = o_ckv + KV_LORA
    w = w_in[l]
    half = QK_ROPE // 2
    zcols = lambda n: jnp.zeros((D_MODEL, n), F32)
    kr = w[:, o_kr:o_kr + QK_ROPE]
    kr_sw = jnp.concatenate([kr[:, half:], kr[:, :half]], axis=1)
    sm1 = jnp.concatenate([kr, zcols(LANES - QK_ROPE)], axis=1)
    sm2 = jnp.concatenate([kr_sw, zcols(DT_LANE - QK_ROPE), w[:, o_dt:o_cq], zcols(LANES - DT_LANE - SSM_HEADS)], axis=1)
    wcat = jnp.concatenate([w[:, :o_dt], w[:, o_cq:o_kr], sm1, sm2], axis=1).astype(BF16)

    wq = w_qb[l].reshape(Q_LORA, MLA_HEADS, QK_NOPE + QK_ROPE)
    wq_nope = wq[:, :, :QK_NOPE].reshape(Q_LORA, MLA_HEADS * QK_NOPE)
    pe = wq[:, :, QK_NOPE:]
    pe_sw = jnp.concatenate([pe[:, :, half:], pe[:, :, :half]], axis=2)
    padpe = lambda p: jnp.pad(p, ((0, 0), (0, 0), (0, LANES - QK_ROPE))).reshape(Q_LORA, MLA_HEADS * LANES)
    wkv = w_kvb[l].reshape(KV_LORA, MLA_HEADS, QK_NOPE + V_HEAD)
    w_kb_t = jnp.transpose(wkv[:, :, :QK_NOPE], (1, 2, 0))
    w_vb = jnp.transpose(wkv[:, :, QK_NOPE:], (1, 0, 2))
    w_vb_all = wkv[:, :, QK_NOPE:].reshape(KV_LORA, MLA_HEADS * V_HEAD)
    row = lambda v: v.reshape(1, -1)
    return dict(
        norm_mix=row(norm_mix[l]), wcat=wcat, b_gate=row(b_gate[l]),
        conv_w=conv_w[l], conv_b=row(conv_b[l]),
        dt_bias_p=_pad_lanes(dt_bias[l], DT_LANE), a_log_p=_pad_lanes(a_log[l], DT_LANE),
        d_skip_e=row(jnp.repeat(d_skip[l], SSM_HEAD_DIM)), d_skip_row=row(d_skip[l]),
        ssm_norm=row(ssm_norm[l]), ssm_norm_t=ssm_norm[l].reshape(SSM_HEADS, SSM_HEAD_DIM).T,
        w_ssm_proj=w_ssm_proj[l].astype(BF16),
        q_norm=row(q_norm[l]), kv_norm=row(kv_norm[l]),
        wq_nope=wq_nope.astype(BF16), wq_pe=padpe(pe).astype(BF16), wq_pe_sw=padpe(pe_sw).astype(BF16),
        w_kb_t=w_kb_t.astype(BF16), w_vb=w_vb.astype(BF16), w_vb_all=w_vb_all.astype(BF16),
        w_mla_proj=w_mla_proj[l].astype(BF16), w_out=w_out[l].astype(BF16),
        norm_ffn=row(norm_ffn[l]), peer_wq=peer_wq[l].astype(BF16), peer_keys=peer_keys[l].astype(BF16),
        peer_u=peer_u[l].astype(BF16), peer_v=peer_v[l].astype(BF16))


def _rope_tables(pos):
    half = QK_ROPE // 2
    inv = ROPE_THETA ** (-jnp.arange(half, dtype=F32) / half)
    ang = pos.astype(F32)[:, None] * inv[None, :]
    cos, sin = jnp.cos(ang), jnp.sin(ang)
    pad = jnp.zeros((pos.shape[0], LANES - QK_ROPE), F32)
    c128 = jnp.concatenate([cos, cos, pad], axis=1)
    s128 = jnp.concatenate([-sin, sin, pad], axis=1)
    ktab = jnp.concatenate([c128, s128], axis=1)
    qtab = jnp.concatenate([jnp.tile(c128, (1, MLA_HEADS)), jnp.tile(s128, (1, MLA_HEADS))], axis=1)
    return qtab, ktab


def _prompt_layer(x2d, nb, seq, lw, tabs, norm_final, final):
    gates, z, xbc, cq, ckv, sm1, sm2 = _in_proj(x2d, lw["norm_mix"], lw["wcat"])
    yn, h_new = _ssd_prompt(xbc, sm2, z, lw, nb, seq)
    qcat, kcat, lat, kpe = _mla_prep(cq, ckv, sm1, sm2, tabs[0], tabs[1], lw, nb, seq)
    o = _attn_prompt(qcat, kcat, lw["w_vb"], nb, seq)
    x1 = _mix(x2d, gates, yn, o, lw)
    x2 = _peer(x1, lw, norm_final, final)
    conv_new = xbc.reshape(nb, seq, CONV_DIM)[:, seq - (CONV_WIDTH - 1):, :]
    return x2, h_new, conv_new, lat.reshape(nb, seq, KV_LORA), kpe.reshape(nb, seq, QK_ROPE)


def _sample_layer(x2d, layer, lw, tabs, state_ssm, state_conv, cache_latent, cache_krope, page_table,
                  norm_final, final):
    nb = x2d.shape[0]
    gates, z, xbc, cq, ckv, sm1, sm2 = _in_proj(x2d, lw["norm_mix"], lw["wcat"])
    prev = state_conv[layer]
    u, dt = _ssd_step_pre(xbc, jnp.transpose(prev, (1, 0, 2)), sm2, lw)
    to_t = lambda v: jnp.transpose(v.reshape(nb, SSM_HEADS, SSM_HEAD_DIM), (0, 2, 1))
    n_bc = SSM_GROUPS * D_STATE
    h_new, yt = _ssd_step(state_ssm, layer, to_t(u[:, :D_INNER]), to_t(z),
                          u[:, D_INNER:D_INNER + n_bc].reshape(nb, 1, n_bc),
                          u[:, D_INNER + n_bc:].reshape(nb, 1, n_bc), dt.reshape(nb, 1, LANES), lw)
    yn = jnp.transpose(yt, (0, 2, 1)).reshape(nb, D_INNER).astype(BF16)
    conv_new = jnp.concatenate([prev[:, 1:], xbc[:, None, :]], axis=1)

    qcat, kcat, lat, kpe = _mla_prep(cq, ckv, sm1, sm2, tabs[0], tabs[1], lw, 1, nb)
    qdec = jnp.transpose(qcat[0], (1, 0, 2))
    o = _attn_decode(page_table, qdec, kcat.reshape(nb, 1, QK_CAT), lw["w_vb_all"], cache_latent, cache_krope, layer)
    x1 = _mix(x2d, gates, yn, o.reshape(nb, MLA_HEADS * V_HEAD), lw)
    pad = (-nb) % LANES
    x2 = _peer(jnp.pad(x1, ((0, pad), (0, 0))), lw, norm_final, final)[:nb]
    return x2, h_new, conv_new, lat.reshape(nb, 1, KV_LORA), kpe.reshape(nb, 1, QK_ROPE)


def kernel(x_prompt, x_sample, cache_latent, cache_krope, state_ssm, state_conv, page_table, norm_mix, w_in, b_gate,
           conv_w, conv_b, dt_bias, a_log, d_skip, ssm_norm, w_ssm_proj, q_norm, w_qb, kv_norm, w_kvb, w_mla_proj,
           w_out, norm_ffn, peer_wq, peer_keys, peer_u, peer_v, norm_final):
    bp, sp = x_prompt.shape[:2]
    bs, ss = x_sample.shape[:2]
    depth = w_in.shape[0]
    assert ss == 1 and sp % SSD_CHUNK == 0
    past_len = page_table.shape[1] * cache_latent.shape[2]
    tabs_p = _rope_tables(jnp.arange(sp, dtype=I32))
    tabs_s = _rope_tables(jnp.full((bs,), past_len, I32))
    nf = norm_final.reshape(1, D_MODEL)
    cache_krope = cache_krope.astype(BF16)
    xp = x_prompt.reshape(bp * sp, D_MODEL)
    xs = x_sample.reshape(bs, D_MODEL)
    outs_p, outs_s = [], []
    for l in range(depth):
        lw = _layer_weights(l, norm_mix, w_in, b_gate, conv_w, conv_b, dt_bias, a_log, d_skip, ssm_norm, w_ssm_proj,
                            q_norm, w_qb, kv_norm, w_kvb, w_mla_proj, w_out, norm_ffn, peer_wq, peer_keys, peer_u,
                            peer_v)
        final = l == depth - 1
        xp, hp, cp, lp, kp = _prompt_layer(xp, bp, sp, lw, tabs_p, nf, final)
        xs, hs, cs, lsn, ksn = _sample_layer(xs, l, lw, tabs_s, state_ssm, state_conv, cache_latent, cache_krope,
                                             page_table, nf, final)
        outs_p.append((lp, kp, hp, cp))
        outs_s.append((lsn, ksn, hs, cs))
    stack = lambda outs, i: jnp.stack([o[i] for o in outs])
    return (xp.reshape(bp, sp, D_MODEL), xs.reshape(bs, ss, D_MODEL),
            stack(outs_p, 0), stack(outs_p, 1), stack(outs_p, 2), stack(outs_p, 3),
            stack(outs_s, 0), stack(outs_s, 1), stack(outs_s, 2), stack(outs_s, 3))
```

```python
import functools

import numpy as np
import jax
import jax.numpy as jnp
from jax import lax
from jax.experimental import pallas as pl
from jax.experimental.pallas import tpu as pltpu

F32 = jnp.float32
BF16 = jnp.bfloat16
I32 = jnp.int32

D_MODEL = 1024
D_INNER = 1024
SSM_HEAD_DIM = 64
SSM_HEADS = 16
SSM_GROUPS = 2
SSM_HPG = 8
D_STATE = 128
CONV_WIDTH = 4
CONV_DIM = D_INNER + 2 * SSM_GROUPS * D_STATE
SSD_CHUNK = 128
MLA_HEADS = 8
QK_NOPE = 64
QK_ROPE = 32
V_HEAD = 64
Q_LORA = 384
KV_LORA = 256
ROPE_THETA = 10000.0
ATTN_SCALE = (QK_NOPE + QK_ROPE) ** -0.5
PEER_HEADS = 8
N_KEYS = 128
N_EXPERTS = N_KEYS * N_KEYS
PEER_D_KEY = 256
PEER_TOPK = 16
EPS = 1e-6

LANES = 128
QK_CAT = KV_LORA + LANES
DT_LANE = 64
VMEM_LIMIT = 56 * 1024 * 1024

_HI = lax.Precision.HIGHEST


def _cparams(*sem):
    return pltpu.CompilerParams(dimension_semantics=sem, vmem_limit_bytes=VMEM_LIMIT)


def _dot(a, b):
    return jnp.dot(a, b, preferred_element_type=F32)


def _dot_nt(a, b):
    return lax.dot_general(a, b, (((1,), (1,)), ((), ())), preferred_element_type=F32)


def _dot_tn(a, b):
    return lax.dot_general(a, b, (((0,), (0,)), ((), ())), preferred_element_type=F32)


def _rms(x, w):
    ms = jnp.mean(x * x, axis=-1, keepdims=True)
    return (x * lax.rsqrt(ms + EPS)) * w


def _sigmoid(x):
    return jax.nn.sigmoid(x)


def _softplus(x):
    return jnp.maximum(x, 0.0) + jnp.log1p(jnp.exp(-jnp.abs(x)))


def _row_block(t, cap):
    if t <= cap:
        return t
    b = cap
    while t % b:
        b //= 2
    return b


_IN_WIDTHS = (2 * D_MODEL, D_INNER, CONV_DIM, Q_LORA, KV_LORA, LANES, LANES)


def _in_proj_body(x_ref, nw_ref, w_ref, *out_refs):
    h = _rms(x_ref[...], nw_ref[...]).astype(BF16)
    off = 0
    for ref, wd in zip(out_refs, _IN_WIDTHS):
        ref[...] = _dot(h, w_ref[:, off:off + wd])
        off += wd


def _in_proj(x2d, nw, wcat):
    t = x2d.shape[0]
    tm = _row_block(t, 256)
    wtot = sum(_IN_WIDTHS)
    return pl.pallas_call(
        _in_proj_body,
        grid=(t // tm,),
        in_specs=[pl.BlockSpec((tm, D_MODEL), lambda i: (i, 0)),
                  pl.BlockSpec((1, D_MODEL), lambda i: (0, 0)),
                  pl.BlockSpec((D_MODEL, wtot), lambda i: (0, 0))],
        out_specs=[pl.BlockSpec((tm, wd), lambda i: (i, 0)) for wd in _IN_WIDTHS],
        out_shape=[jax.ShapeDtypeStruct((t, wd), F32) for wd in _IN_WIDTHS],
        compiler_params=_cparams("parallel"),
        name="in_proj",
    )(x2d, nw, wcat)


def _head_expand():
    r = lax.broadcasted_iota(I32, (LANES, D_INNER), 0)
    c = lax.broadcasted_iota(I32, (LANES, D_INNER), 1)
    return (jnp.right_shift(c, 6) == (r - DT_LANE)).astype(F32)


def _ssd_prompt_body(nc, xbc_ref, sm_ref, z_ref, cw_ref, cb_ref, dtb_ref, alog_ref, dsk_ref, nrm_ref,
                     yn_ref, hfin_ref, xpad_ref, st_ref):
    c = pl.program_id(1)
    cl = SSD_CHUNK

    @pl.when(c == 0)
    def _():
        xpad_ref[0:8, :] = jnp.zeros((8, CONV_DIM), F32)
        st_ref[...] = jnp.zeros_like(st_ref)

    xpad_ref[8:8 + cl, :] = xbc_ref[...]
    conv = cb_ref[...]
    for k in range(CONV_WIDTH):
        conv = conv + xpad_ref[5 + k:5 + k + cl, :] * cw_ref[k:k + 1, :]
    xpad_ref[0:8, :] = xpad_ref[cl:cl + 8, :]
    u = conv * _sigmoid(conv)
    xs = u[:, :D_INNER]
    bb = u[:, D_INNER:D_INNER + SSM_GROUPS * D_STATE].astype(BF16)
    cb = u[:, D_INNER + SSM_GROUPS * D_STATE:].astype(BF16)

    row = lax.broadcasted_iota(I32, (cl, cl), 0)
    col = lax.broadcasted_iota(I32, (cl, cl), 1)
    dt_ok = (col >= DT_LANE) & (col < DT_LANE + SSM_HEADS)
    dt = jnp.where(dt_ok, _softplus(sm_ref[...] + dtb_ref[...]), 0.0)
    a = dt * (-jnp.exp(alog_ref[...]))
    tri = (row >= col).astype(F32)
    a_cs = jnp.dot(tri, a, precision=_HI, preferred_element_type=F32)
    a_cs_t = a_cs.T
    ea = jnp.exp(a_cs)
    dte = jnp.exp(a_cs[cl - 1:cl, :] - a_cs)
    ex = _head_expand()
    dt_e = jnp.dot(dt, ex, precision=_HI, preferred_element_type=F32)
    ea_e = jnp.dot(ea, ex, precision=_HI, preferred_element_type=F32)
    dte_e = jnp.dot(dte, ex, precision=_HI, preferred_element_type=F32)

    xdt = xs * dt_e
    xdt_b = xdt.astype(BF16)
    causal = row >= col
    ys = []
    for g in range(SSM_GROUPS):
        cbm = _dot_nt(cb[:, g * D_STATE:(g + 1) * D_STATE], bb[:, g * D_STATE:(g + 1) * D_STATE])
        for e in range(SSM_HPG):
            h = g * SSM_HPG + e
            seg = a_cs[:, DT_LANE + h:DT_LANE + h + 1] - a_cs_t[DT_LANE + h:DT_LANE + h + 1, :]
            lm = jnp.exp(jnp.where(causal, seg, -jnp.inf))
            ys.append(_dot((cbm * lm).astype(BF16), xdt_b[:, h * SSM_HEAD_DIM:(h + 1) * SSM_HEAD_DIM]))
    y_diag = jnp.concatenate(ys, axis=1)

    xw = (xdt * dte_e).astype(BF16)
    dec = ea_e[cl - 1:cl, :]
    half = D_INNER // SSM_GROUPS
    yoffs = []
    for g in range(SSM_GROUPS):
        s_prev = st_ref[g]
        yoffs.append(_dot(cb[:, g * D_STATE:(g + 1) * D_STATE], s_prev.astype(BF16)))
        st_ref[g] = dec[:, g * half:(g + 1) * half] * s_prev + _dot_tn(
            bb[:, g * D_STATE:(g + 1) * D_STATE], xw[:, g * half:(g + 1) * half])
    y = y_diag + jnp.concatenate(yoffs, axis=1) * ea_e + dsk_ref[...] * xs
    zv = z_ref[...]
    y = y * (zv * _sigmoid(zv))
    outs = []
    for g in range(SSM_GROUPS):
        outs.append(_rms(y[:, g * half:(g + 1) * half], nrm_ref[:, g * half:(g + 1) * half]))
    yn_ref[...] = jnp.concatenate(outs, axis=1).astype(BF16)

    @pl.when(c == nc - 1)
    def _():
        for g in range(SSM_GROUPS):
            s_t = st_ref[g].T
            for e in range(SSM_HPG):
                hfin_ref[0, g, e] = s_t[e * SSM_HEAD_DIM:(e + 1) * SSM_HEAD_DIM, :]


def _ssd_prompt(xbc, sm2, z, lw, nb, seq):
    nc = seq // SSD_CHUNK
    cl = SSD_CHUNK
    rowmap = lambda b, c: (b * nc + c, 0)
    full = lambda b, c: (0, 0)
    return pl.pallas_call(
        functools.partial(_ssd_prompt_body, nc),
        grid=(nb, nc),
        in_specs=[pl.BlockSpec((cl, CONV_DIM), rowmap),
                  pl.BlockSpec((cl, LANES), rowmap),
                  pl.BlockSpec((cl, D_INNER), rowmap),
                  pl.BlockSpec((CONV_WIDTH, CONV_DIM), full),
                  pl.BlockSpec((1, CONV_DIM), full),
                  pl.BlockSpec((1, LANES), full),
                  pl.BlockSpec((1, LANES), full),
                  pl.BlockSpec((1, D_INNER), full),
                  pl.BlockSpec((1, D_INNER), full)],
        out_specs=[pl.BlockSpec((cl, D_INNER), rowmap),
                   pl.BlockSpec((1, SSM_GROUPS, SSM_HPG, SSM_HEAD_DIM, D_STATE), lambda b, c: (b, 0, 0, 0, 0))],
        out_shape=[jax.ShapeDtypeStruct((nb * seq, D_INNER), BF16),
                   jax.ShapeDtypeStruct((nb, SSM_GROUPS, SSM_HPG, SSM_HEAD_DIM, D_STATE), F32)],
        scratch_shapes=[pltpu.VMEM((cl + 8, CONV_DIM), F32),
                        pltpu.VMEM((SSM_GROUPS, D_STATE, D_INNER // SSM_GROUPS), F32)],
        compiler_params=_cparams("parallel", "arbitrary"),
        name="ssd_prompt",
    )(xbc, sm2, z, lw["conv_w"], lw["conv_b"], lw["dt_bias_p"], lw["a_log_p"], lw["d_skip_e"], lw["ssm_norm"])


def _ssd_step_pre_body(xbc_ref, prev_ref, sm_ref, cw_ref, cb_ref, dtb_ref, u_ref, dt_ref):
    conv = cb_ref[...]
    for k in range(CONV_WIDTH - 1):
        conv = conv + prev_ref[k] * cw_ref[k:k + 1, :]
    conv = conv + xbc_ref[...] * cw_ref[CONV_WIDTH - 1:CONV_WIDTH, :]
    u_ref[...] = conv * _sigmoid(conv)
    dt_ref[...] = _softplus(sm_ref[...] + dtb_ref[...])


def _ssd_step_pre(xbc, prev_t, sm2, lw):
    nb = xbc.shape[0]
    return pl.pallas_call(
        _ssd_step_pre_body,
        out_shape=[jax.ShapeDtypeStruct((nb, CONV_DIM), F32), jax.ShapeDtypeStruct((nb, LANES), F32)],
        compiler_params=pltpu.CompilerParams(vmem_limit_bytes=VMEM_LIMIT),
        name="ssd_step_pre",
    )(xbc, prev_t, sm2, lw["conv_w"], lw["conv_b"], lw["dt_bias_p"])


def _ssd_step_body(h0_ref, xt_ref, zt_ref, b_ref, c_ref, dt_ref, alog_ref, dsk_ref, nrm_ref, hn_ref, yt_ref):
    xt = xt_ref[0]
    dtv = dt_ref[0]
    av = -jnp.exp(alog_ref[...])
    lane = lax.broadcasted_iota(I32, (SSM_HEAD_DIM, SSM_HEADS), 1)
    y = jnp.zeros((SSM_HEAD_DIM, SSM_HEADS), F32)
    for g in range(SSM_GROUPS):
        bg = b_ref[0][:, g * D_STATE:(g + 1) * D_STATE]
        cg = c_ref[0][:, g * D_STATE:(g + 1) * D_STATE]
        for e in range(SSM_HPG):
            h = g * SSM_HPG + e
            dth = dtv[:, DT_LANE + h:DT_LANE + h + 1]
            dec = jnp.exp(dth * av[:, DT_LANE + h:DT_LANE + h + 1])
            hn = dec * h0_ref[0, g, e] + (xt[:, h:h + 1] * dth) * bg
            hn_ref[0, g, e] = hn
            ycol = jnp.sum(hn * cg, axis=1, keepdims=True)
            y = jnp.where(lane == h, ycol, y)
    y = y + dsk_ref[...] * xt
    zv = zt_ref[0]
    y = y * (zv * _sigmoid(zv))
    sq = y * y
    in_g0 = lane < SSM_HPG
    n_g = float(D_INNER // SSM_GROUPS)
    ms0 = jnp.sum(jnp.sum(jnp.where(in_g0, sq, 0.0), axis=1, keepdims=True), axis=0, keepdims=True) / n_g
    ms1 = jnp.sum(jnp.sum(jnp.where(in_g0, 0.0, sq), axis=1, keepdims=True), axis=0, keepdims=True) / n_g
    scale = jnp.where(in_g0, lax.rsqrt(ms0 + EPS), lax.rsqrt(ms1 + EPS))
    yt_ref[0] = (y * scale) * nrm_ref[...]


def _ssd_step(h0_all, layer, xt, zt, bm, cm, dt, lw):
    nb = xt.shape[0]
    st_block = (None, 1, SSM_GROUPS, SSM_HPG, SSM_HEAD_DIM, D_STATE)
    pt_block = (1, SSM_HEAD_DIM, SSM_HEADS)
    vec = lambda n: pl.BlockSpec((1, 1, n), lambda b: (b, 0, 0))
    full = lambda r, n: pl.BlockSpec((r, n), lambda b: (0, 0))
    return pl.pallas_call(
        _ssd_step_body,
        grid=(nb,),
        in_specs=[pl.BlockSpec(st_block, lambda b: (layer, b, 0, 0, 0, 0)),
                  pl.BlockSpec(pt_block, lambda b: (b, 0, 0)),
                  pl.BlockSpec(pt_block, lambda b: (b, 0, 0)),
                  vec(SSM_GROUPS * D_STATE), vec(SSM_GROUPS * D_STATE), vec(LANES),
                  full(1, LANES), full(1, SSM_HEADS), full(SSM_HEAD_DIM, SSM_HEADS)],
        out_specs=[pl.BlockSpec((1, SSM_GROUPS, SSM_HPG, SSM_HEAD_DIM, D_STATE), lambda b: (b, 0, 0, 0, 0)),
                   pl.BlockSpec(pt_block, lambda b: (b, 0, 0))],
        out_shape=[jax.ShapeDtypeStruct((nb, SSM_GROUPS, SSM_HPG, SSM_HEAD_DIM, D_STATE), F32),
                   jax.ShapeDtypeStruct((nb, SSM_HEAD_DIM, SSM_HEADS), F32)],
        compiler_params=_cparams("parallel"),
        name="ssd_step",
    )(h0_all, xt, zt, bm, cm, dt, lw["a_log_p"], lw["d_skip_row"], lw["ssm_norm_t"])


def _mla_prep_body(cq_ref, ckv_ref, sm1_ref, sm2_ref, qtab_ref, ktab_ref, qn_ref, kvn_ref,
                   wn_ref, wpe_ref, wpes_ref, wkb_ref, qcat_ref, kcat_ref, lat_ref, kpe_ref):
    hq = _rms(cq_ref[...], qn_ref[...]).astype(BF16)
    qn = _dot(hq, wn_ref[...]).astype(BF16)
    npe = MLA_HEADS * LANES
    qpe = (_dot(hq, wpe_ref[...]) * qtab_ref[:, :npe] + _dot(hq, wpes_ref[...]) * qtab_ref[:, npe:]) * ATTN_SCALE
    for h in range(MLA_HEADS):
        ql = _dot(qn[:, h * QK_NOPE:(h + 1) * QK_NOPE], wkb_ref[h]) * ATTN_SCALE
        qcat_ref[0, h] = jnp.concatenate([ql, qpe[:, h * LANES:(h + 1) * LANES]], axis=1).astype(BF16)
    lat = _rms(ckv_ref[...], kvn_ref[...])
    lat_ref[...] = lat
    kpe = sm1_ref[...] * ktab_ref[:, :LANES] + sm2_ref[...] * ktab_ref[:, LANES:]
    kpe_ref[...] = kpe[:, :QK_ROPE]
    kcat_ref[...] = jnp.concatenate([lat, kpe], axis=1).astype(BF16)


def _mla_prep(cq, ckv, sm1, sm2, qtab, ktab, lw, nb, seq):
    t = nb * seq
    tm = _row_block(seq, 512)
    nl = seq // tm
    rowmap = lambda b, i: (b * nl + i, 0)
    tabmap = lambda b, i: (i, 0)
    full2 = lambda b, i: (0, 0)
    npe = MLA_HEADS * LANES
    return pl.pallas_call(
        _mla_prep_body,
        grid=(nb, nl),
        in_specs=[pl.BlockSpec((tm, Q_LORA), rowmap),
                  pl.BlockSpec((tm, KV_LORA), rowmap),
                  pl.BlockSpec((tm, LANES), rowmap),
                  pl.BlockSpec((tm, LANES), rowmap),
                  pl.BlockSpec((tm, 2 * npe), tabmap),
                  pl.BlockSpec((tm, 2 * LANES), tabmap),
                  pl.BlockSpec((1, Q_LORA), full2),
                  pl.BlockSpec((1, KV_LORA), full2),
                  pl.BlockSpec((Q_LORA, MLA_HEADS * QK_NOPE), full2),
                  pl.BlockSpec((Q_LORA, npe), full2),
                  pl.BlockSpec((Q_LORA, npe), full2),
                  pl.BlockSpec((MLA_HEADS, QK_NOPE, KV_LORA), lambda b, i: (0, 0, 0))],
        out_specs=[pl.BlockSpec((1, MLA_HEADS, tm, QK_CAT), lambda b, i: (b, 0, i, 0)),
                   pl.BlockSpec((tm, QK_CAT), rowmap),
                   pl.BlockSpec((tm, KV_LORA), rowmap),
                   pl.BlockSpec((tm, QK_ROPE), rowmap)],
        out_shape=[jax.ShapeDtypeStruct((nb, MLA_HEADS, seq, QK_CAT), BF16),
                   jax.ShapeDtypeStruct((t, QK_CAT), BF16),
                   jax.ShapeDtypeStruct((t, KV_LORA), F32),
                   jax.ShapeDtypeStruct((t, QK_ROPE), F32)],
        compiler_params=_cparams("parallel", "parallel"),
        name="mla_prep",
    )(cq, ckv, sm1, sm2, qtab, ktab, lw["q_norm"], lw["kv_norm"], lw["wq_nope"], lw["wq_pe"], lw["wq_pe_sw"],
      lw["w_kb_t"])


def _attn_prompt_body(tq, tk, nk, q_ref, k_ref, wvb_ref, o_ref, m_ref, l_ref, acc_ref):
    i = pl.program_id(1)
    j = pl.program_id(2)

    @pl.when(j == 0)
    def _():
        m_ref[...] = jnp.full(m_ref.shape, -jnp.inf, F32)
        l_ref[...] = jnp.zeros(l_ref.shape, F32)
        acc_ref[...] = jnp.zeros(acc_ref.shape, F32)

    @pl.when(j * tk <= i * tq + tq - 1)
    def _():
        k = k_ref[...]
        v = k[:, :KV_LORA]
        qpos = i * tq + lax.broadcasted_iota(I32, (tq, tk), 0)
        kpos = j * tk + lax.broadcasted_iota(I32, (tq, tk), 1)
        visible = kpos <= qpos
        ahead = 2
        s_q = [_dot_nt(q_ref[0, h], k) for h in range(ahead)]
        for h in range(MLA_HEADS):
            hs = slice(h * tq, (h + 1) * tq)
            s = jnp.where(visible, s_q.pop(0), -jnp.inf)
            if h + ahead < MLA_HEADS:
                s_q.append(_dot_nt(q_ref[0, h + ahead], k))
            m_prev = m_ref[hs]
            m_new = jnp.maximum(m_prev, jnp.max(s, axis=1, keepdims=True))
            p = jnp.exp(s - m_new)
            alpha = jnp.exp(m_prev - m_new)
            l_ref[hs] = alpha * l_ref[hs] + jnp.sum(p, axis=1, keepdims=True)
            acc_ref[hs] = alpha * acc_ref[hs] + _dot(p.astype(BF16), v)
            m_ref[hs] = m_new

    @pl.when(j == nk - 1)
    def _():
        outs = []
        for h in range(MLA_HEADS):
            hs = slice(h * tq, (h + 1) * tq)
            outs.append(_dot((acc_ref[hs] / l_ref[hs]).astype(BF16), wvb_ref[h]))
        o_ref[...] = jnp.concatenate(outs, axis=1).astype(BF16)


def _attn_prompt(qcat, kcat, w_vb, nb, seq):
    tq = _row_block(seq, 256)
    tk = _row_block(seq, 512)
    nq, nk = seq // tq, seq // tk

    def kmap(b, i, j):
        return (b * nk + jnp.minimum(j, (i * tq + tq - 1) // tk), 0)

    return pl.pallas_call(
        functools.partial(_attn_prompt_body, tq, tk, nk),
        grid=(nb, nq, nk),
        in_specs=[pl.BlockSpec((1, MLA_HEADS, tq, QK_CAT), lambda b, i, j: (b, 0, i, 0)),
                  pl.BlockSpec((tk, QK_CAT), kmap),
                  pl.BlockSpec((MLA_HEADS, KV_LORA, V_HEAD), lambda b, i, j: (0, 0, 0))],
        out_specs=pl.BlockSpec((tq, MLA_HEADS * V_HEAD), lambda b, i, j: (b * nq + i, 0)),
        out_shape=jax.ShapeDtypeStruct((nb * seq, MLA_HEADS * V_HEAD), BF16),
        scratch_shapes=[pltpu.VMEM((MLA_HEADS * tq, 1), F32), pltpu.VMEM((MLA_HEADS * tq, 1), F32),
                        pltpu.VMEM((MLA_HEADS * tq, KV_LORA), F32)],
        compiler_params=_cparams("parallel", "parallel", "arbitrary"),
        name="attn_prompt",
    )(qcat, kcat, w_vb)


def _attn_decode_body(pg, nj, pt_ref, q_ref, kc_ref, wvb_ref, *refs):
    lat_refs = refs[:pg]
    kr_refs = refs[pg:2 * pg]
    o_ref, m_ref, l_ref, acc_ref = refs[2 * pg:]
    j = pl.program_id(1)

    @pl.when(j == 0)
    def _():
        m_ref[...] = jnp.full(m_ref.shape, -jnp.inf, F32)
        l_ref[...] = jnp.zeros(l_ref.shape, F32)
        acc_ref[...] = jnp.zeros(acc_ref.shape, F32)

    q = q_ref[0]
    ql = q[:, :KV_LORA]
    qr = q[:, KV_LORA:KV_LORA + QK_ROPE]
    kvs = [lat_refs[i][...].astype(BF16) for i in range(pg)]
    ss = [_dot_nt(ql, kvs[i]) + _dot(qr, kr_refs[i][...]) for i in range(pg)]
    m_prev = m_ref[...]
    m_blk = ss[0]
    for i in range(1, pg):
        m_blk = jnp.maximum(m_blk, ss[i])
    m_new = jnp.maximum(m_prev, jnp.max(m_blk, axis=1, keepdims=True))
    ps = [jnp.exp(ss[i] - m_new) for i in range(pg)]
    p_sum = ps[0]
    for i in range(1, pg):
        p_sum = p_sum + ps[i]
    pvs = [_dot(ps[i].astype(BF16), kvs[i]) for i in range(pg)]
    while len(pvs) > 1:
        pvs = [pvs[i] + pvs[i + 1] for i in range(0, len(pvs), 2)]
    alpha = jnp.exp(m_prev - m_new)
    l_ref[...] = alpha * l_ref[...] + jnp.sum(p_sum, axis=1, keepdims=True)
    acc_ref[...] = alpha * acc_ref[...] + pvs[0]
    m_ref[...] = m_new

    @pl.when(j == nj - 1)
    def _():
        kc = kc_ref[0].astype(F32)
        s = jnp.sum(q.astype(F32) * kc, axis=1, keepdims=True)
        m_prev = m_ref[...]
        m_new = jnp.maximum(m_prev, s)
        p = jnp.exp(s - m_new)
        alpha = jnp.exp(m_prev - m_new)
        l_fin = alpha * l_ref[...] + p
        acc = alpha * acc_ref[...] + p.astype(BF16).astype(F32) * kc[:, :KV_LORA]
        o = (acc / l_fin).astype(BF16)
        full = _dot(o, wvb_ref[...])
        hrow = lax.broadcasted_iota(I32, full.shape, 0)
        hcol = jnp.right_shift(lax.broadcasted_iota(I32, full.shape, 1), 6)
        o_ref[0] = jnp.sum(jnp.where(hrow == hcol, full, 0.0), axis=0, keepdims=True).astype(BF16)


def _attn_decode(page_table, qdec, kcur, w_vb_all, cache_latent, cache_krope, layer):
    nb, n_pages = page_table.shape
    page = cache_latent.shape[2]
    pg = 16
    while n_pages % pg:
        pg //= 2
    nj = n_pages // pg

    def pmap(i):
        return lambda b, j, pt: (layer, pt[b, j * pg + i], 0, 0)

    in_specs = [pl.BlockSpec((1, MLA_HEADS, QK_CAT), lambda b, j, pt: (b, 0, 0)),
                pl.BlockSpec((1, 1, QK_CAT), lambda b, j, pt: (b, 0, 0)),
                pl.BlockSpec((KV_LORA, MLA_HEADS * V_HEAD), lambda b, j, pt: (0, 0))]
    in_specs += [pl.BlockSpec((None, None, page, KV_LORA), pmap(i)) for i in range(pg)]
    in_specs += [pl.BlockSpec((None, None, QK_ROPE, page), pmap(i)) for i in range(pg)]
    grid_spec = pltpu.PrefetchScalarGridSpec(
        num_scalar_prefetch=1,
        grid=(nb, nj),
        in_specs=in_specs,
        out_specs=pl.BlockSpec((1, 1, MLA_HEADS * V_HEAD), lambda b, j, pt: (b, 0, 0)),
        scratch_shapes=[pltpu.VMEM((MLA_HEADS, 1), F32), pltpu.VMEM((MLA_HEADS, 1), F32),
                        pltpu.VMEM((MLA_HEADS, KV_LORA), F32)])
    return pl.pallas_call(
        functools.partial(_attn_decode_body, pg, nj),
        grid_spec=grid_spec,
        out_shape=jax.ShapeDtypeStruct((nb, 1, MLA_HEADS * V_HEAD), BF16),
        compiler_params=_cparams("parallel", "arbitrary"),
        name="attn_decode",
    )(page_table, qdec, kcur, w_vb_all, *([cache_latent] * pg), *([cache_krope] * pg))


def _mix_body(x_ref, g_ref, bg_ref, yn_ref, o_ref, wssm_ref, wmla_ref, wout_ref, x1_ref):
    ya = _dot(yn_ref[...], wssm_ref[...])
    yb = _dot(o_ref[...], wmla_ref[...])
    gate = _sigmoid(g_ref[...] + bg_ref[...])
    mix = gate[:, :D_MODEL] * ya + gate[:, D_MODEL:] * yb
    x1_ref[...] = x_ref[...] + _dot(mix.astype(BF16), wout_ref[...])


def _mix(x2d, gates, yn, o, lw):
    t = x2d.shape[0]
    tm = _row_block(t, 512)
    row = lambda n: pl.BlockSpec((tm, n), lambda i: (i, 0))
    full = lambda r, n: pl.BlockSpec((r, n), lambda i: (0, 0))
    return pl.pallas_call(
        _mix_body,
        grid=(t // tm,),
        in_specs=[row(D_MODEL), row(2 * D_MODEL), full(1, 2 * D_MODEL), row(D_INNER), row(MLA_HEADS * V_HEAD),
                  full(D_INNER, D_MODEL), full(MLA_HEADS * V_HEAD, D_MODEL), full(D_MODEL, D_MODEL)],
        out_specs=row(D_MODEL),
        out_shape=jax.ShapeDtypeStruct((t, D_MODEL), F32),
        compiler_params=_cparams("parallel"),
        name="mix",
    )(x2d, gates, lw["b_gate"], yn, o, lw["w_ssm_proj"], lw["w_mla_proj"], lw["w_out"])


def _peer_scores_body(x_ref, nw_ref, wq_ref, keys_ref, hn_ref, s1_ref, s2_ref):
    hn = _rms(x_ref[...], nw_ref[...]).astype(BF16)
    hn_ref[...] = hn
    q = _dot(hn, wq_ref[...]).astype(BF16)
    half = PEER_D_KEY // 2
    for h in range(PEER_HEADS):
        s1_ref[h] = _dot_nt(keys_ref[h, 0], q[:, h * PEER_D_KEY:h * PEER_D_KEY + half])
        s2_ref[h] = _dot_nt(keys_ref[h, 1], q[:, h * PEER_D_KEY + half:(h + 1) * PEER_D_KEY])


def _peer_scores(x1, lw):
    t = x1.shape[0]
    tm = _row_block(t, 512)
    sblock = pl.BlockSpec((PEER_HEADS, N_KEYS, tm), lambda i: (0, 0, i))
    return pl.pallas_call(
        _peer_scores_body,
        grid=(t // tm,),
        in_specs=[pl.BlockSpec((tm, D_MODEL), lambda i: (i, 0)),
                  pl.BlockSpec((1, D_MODEL), lambda i: (0, 0)),
                  pl.BlockSpec((D_MODEL, PEER_HEADS * PEER_D_KEY), lambda i: (0, 0)),
                  pl.BlockSpec((PEER_HEADS, 2, N_KEYS, PEER_D_KEY // 2), lambda i: (0, 0, 0, 0))],
        out_specs=[pl.BlockSpec((tm, D_MODEL), lambda i: (i, 0)), sblock, sblock],
        out_shape=[jax.ShapeDtypeStruct((t, D_MODEL), BF16),
                   jax.ShapeDtypeStruct((PEER_HEADS, N_KEYS, t), F32),
                   jax.ShapeDtypeStruct((PEER_HEADS, N_KEYS, t), F32)],
        compiler_params=_cparams("parallel"),
        name="peer_scores",
    )(x1, lw["norm_ffn"], lw["peer_wq"], lw["peer_keys"])


SUBLANES = 8


def _top16_rows(s):
    n = s.shape[0]
    row = lax.broadcasted_iota(I32, s.shape, 0)
    r16 = lax.broadcasted_iota(I32, (PEER_TOPK, LANES), 0)

    def body(r, carry):
        rank, vals = carry
        alive = rank == PEER_TOPK
        m = jnp.max(jnp.where(alive, s, -jnp.inf), axis=0, keepdims=True)
        idx = jnp.min(jnp.where(alive & (s == m), row, n), axis=0, keepdims=True)
        rank = jnp.where(row == idx, r, rank)
        vals = jnp.where(r16 == r, m, vals)
        return rank, vals

    return lax.fori_loop(0, PEER_TOPK, body,
                         (jnp.full(s.shape, PEER_TOPK, I32), jnp.zeros((PEER_TOPK, LANES), F32)))


def _batcher_net(n):
    def merge(lo, hi, r):
        step = r * 2
        if step < hi - lo:
            yield from merge(lo, hi, step)
            yield from merge(lo + r, hi, step)
            yield from [(i, i + r) for i in range(lo + r, hi - r, step)]
        else:
            yield (lo, lo + r)

    def sort(lo, hi):
        if hi - lo >= 1:
            mid = lo + (hi - lo) // 2
            yield from sort(lo, mid)
            yield from sort(mid + 1, hi)
            yield from merge(lo, hi, 1)

    return tuple(sort(0, n - 1))


_NET16 = _batcher_net(PEER_TOPK)


def _cmpx(v, x, i, j):
    a, b = v[i], v[j]
    if b is None:
        return
    if a is None:
        v[i], v[j] = b, None
        if x is not None:
            x[i], x[j] = x[j], x[i]
        return
    v[i], v[j] = jnp.maximum(a, b), jnp.minimum(a, b)
    if x is not None:
        first = a >= b
        x[i], x[j] = jnp.where(first, x[i], x[j]), jnp.where(first, x[j], x[i])


def _top16_sorted(groups, payload=None):
    pad = PEER_TOPK - len(groups)
    v = list(groups) + [None] * pad
    x = None if payload is None else list(payload) + [None] * pad
    for i, j in _NET16:
        _cmpx(v, x, i, j)
    for shift in (4, 2, 1):
        roll = lambda y: None if y is None else pltpu.roll(y, shift, 0)
        b = [roll(y) for y in v]
        bx = None if x is None else [roll(y) for y in x]
        c, cx = [], (None if x is None else [])
        for i in range(PEER_TOPK):
            p, q = v[i], b[PEER_TOPK - 1 - i]
            if p is None or q is None:
                c.append(q if p is None else p)
                if x is not None:
                    cx.append(bx[PEER_TOPK - 1 - i] if p is None else x[i])
            else:
                c.append(jnp.maximum(p, q))
                if x is not None:
                    cx.append(jnp.where(p >= q, x[i], bx[PEER_TOPK - 1 - i]))
        for d in (8, 4, 2, 1):
            for i in range(PEER_TOPK):
                if not i & d:
                    _cmpx(c, cx, i, i + d)
        v, x = c, cx
    return (v, x) if payload is not None else v


def _colsum(x):
    x = x + pltpu.roll(x, 4, 0)
    x = x + pltpu.roll(x, 2, 0)
    return x + pltpu.roll(x, 1, 0)


def _picks_by_value(s1, s2):
    k = PEER_TOPK
    ng = N_KEYS // SUBLANES
    sub = lax.broadcasted_iota(I32, (SUBLANES, LANES), 0)
    subf = sub.astype(F32)
    one = lambda m: jnp.where(m, 1.0, 0.0)
    c1 = [s1[SUBLANES * v:SUBLANES * (v + 1)] for v in range(ng)]
    c2 = [s2[SUBLANES * v:SUBLANES * (v + 1)] for v in range(ng)]
    ids = [subf + float(SUBLANES * v) for v in range(ng)]
    t1, i1 = _top16_sorted(c1, ids)
    t2, i2 = _top16_sorted(c2, ids)

    def as_rows(t, base):
        out = t[base]
        for r in range(1, SUBLANES):
            out = jnp.where(sub == r, t[base + r], out)
        return out

    sc2_lo, sc2_hi, sc1_hi = as_rows(t2, 0), as_rows(t2, SUBLANES), as_rows(t1, SUBLANES)
    cand = [t1[0] + sc2_lo, t1[0] + sc2_hi] + [t1[r] + sc2_lo for r in range(1, SUBLANES)] + [sc1_hi + t2[0]]
    tau = _top16_sorted(cand)[k - 1]
    sel = [c >= tau for c in cand]
    n_r = [_colsum(one(sel[0]) + one(sel[1]))] + [_colsum(one(sel[1 + r])) for r in range(1, SUBLANES)]
    n_hi = one(sel[9])
    n_r += [_colsum(jnp.where(sub == q, n_hi, 0.0)) for q in range(SUBLANES)]
    cmax = t1[0] + t2[0]
    zsum = jnp.where(sel[0], jnp.exp(cand[0] - cmax), 0.0)
    for g in range(1, len(cand)):
        zsum = zsum + jnp.where(sel[g], jnp.exp(cand[g] - cmax), 0.0)
    scale = 0.5 / _colsum(zsum)
    pre = [jnp.zeros((SUBLANES, LANES), F32)]
    for r in range(k):
        pre.append(pre[r] + n_r[r])
    bad = pre[k] != float(k)
    for r in range(k - 1):
        bad = bad | (t1[r] == t1[r + 1]) | (t2[r] == t2[r + 1])
    n_in1, n_in2 = one(c1[0] >= t1[k - 1]), one(c2[0] >= t2[k - 1])
    for v in range(1, ng):
        n_in1, n_in2 = n_in1 + one(c1[v] >= t1[k - 1]), n_in2 + one(c2[v] >= t2[k - 1])
    bad = bad | (_colsum(n_in1) != float(k)) | (_colsum(n_in2) != float(k))
    outs = ([], [], [])
    for half in range(k // SUBLANES):
        pick = subf + float(SUBLANES * half)
        r1p = jnp.zeros((SUBLANES, LANES), F32)
        base = jnp.zeros((SUBLANES, LANES), F32)
        for r in range(1, k):
            started = pre[r] <= pick
            r1p = jnp.where(started, float(r), r1p)
            base = jnp.where(started, pre[r], base)
        r2p = pick - base
        a_id, b_id, v1, v2 = i1[0], i2[0], t1[0], t2[0]
        for r in range(1, k):
            is1, is2 = r1p == float(r), r2p == float(r)
            a_id, v1 = jnp.where(is1, i1[r], a_id), jnp.where(is1, t1[r], v1)
            b_id, v2 = jnp.where(is2, i2[r], b_id), jnp.where(is2, t2[r], v2)
        outs[0].append(a_id)
        outs[1].append(b_id)
        outs[2].append(jnp.exp((v1 + v2) - cmax) * scale)
    return tuple(jnp.concatenate(o, axis=0) for o in outs) + (one(bad),)


def _picks_by_index(s1, s2):
    k = PEER_TOPK
    r16 = lax.broadcasted_iota(I32, (k, LANES), 0)
    rowf = lax.broadcasted_iota(I32, (N_KEYS, LANES), 0).astype(F32)
    rank1, sc1 = _top16_rows(s1)
    rank2, sc2 = _top16_rows(s2)

    def index_of_rank(rank):
        out = jnp.zeros((k, LANES), F32)
        for r in range(k):
            out = jnp.where(r16 == r, jnp.sum(jnp.where(rank == r, rowf, 0.0), axis=0, keepdims=True), out)
        return out

    i1, i2 = index_of_rank(rank1), index_of_rank(rank2)
    rep = lambda row, n: jnp.broadcast_to(row, (n, LANES))
    cand = jnp.concatenate([sc1[0:1] + sc2] + [sc1[r:r + 1] + sc2[0:8] for r in range(1, 8)] + [sc1[8:16] + sc2[0:1]],
                           axis=0)
    a_c = jnp.concatenate([rep(i1[0:1], k)] + [rep(i1[r:r + 1], 8) for r in range(1, 8)] + [i1[8:16]], axis=0)
    b_c = jnp.concatenate([i2] + [i2[0:8]] * 7 + [rep(i2[0:1], 8)], axis=0)
    rankc, _ = _top16_rows(cand)
    z = jnp.sum(jnp.where(rankc < k, jnp.exp(cand - cand[0:1]), 0.0), axis=0, keepdims=True)
    a_id = jnp.zeros((k, LANES), F32)
    b_id = jnp.zeros((k, LANES), F32)
    val = jnp.zeros((k, LANES), F32)
    for p in range(k):
        m = rankc == p
        pick = lambda c: jnp.sum(jnp.where(m, c, 0.0), axis=0, keepdims=True)
        a_id = jnp.where(r16 == p, pick(a_c), a_id)
        b_id = jnp.where(r16 == p, pick(b_c), b_id)
        val = jnp.where(r16 == p, pick(cand), val)
    return a_id, b_id, jnp.exp(val - cand[0:1]) * (0.5 / z)


def _peer_gates_body(s1_ref, s2_ref, a_ref, b_ref, g_ref, at_ref, bt_ref, gt_ref):
    k = PEER_TOPK

    def head(h, _):
        s1 = s1_ref[h]
        s2 = s2_ref[h]
        rows = pl.ds(pl.multiple_of(h * k, k), k)
        a_id, b_id, gate, bad = _picks_by_value(s1, s2)
        at_ref[rows, :] = a_id
        bt_ref[rows, :] = b_id
        gt_ref[rows, :] = gate

        @pl.when(jnp.max(bad) > 0.0)
        def _():
            at_ref[rows, :], bt_ref[rows, :], gt_ref[rows, :] = _picks_by_index(s1, s2)

        return 0

    lax.fori_loop(0, PEER_HEADS, head, 0)
    a_ref[...] = at_ref[...].T
    b_ref[...] = bt_ref[...].T
    g_ref[...] = gt_ref[...].T


def _peer_gates(s1t, s2t):
    t = s1t.shape[2]
    nslot = PEER_HEADS * PEER_TOPK
    blk = pl.BlockSpec((PEER_HEADS, N_KEYS, LANES), lambda i: (0, 0, i))
    oblk = pl.BlockSpec((LANES, nslot), lambda i: (i, 0))
    shp = jax.ShapeDtypeStruct((t, nslot), F32)
    return pl.pallas_call(
        _peer_gates_body,
        grid=(t // LANES,),
        in_specs=[blk, blk],
        out_specs=[oblk, oblk, oblk],
        out_shape=[shp, shp, shp],
        scratch_shapes=[pltpu.VMEM((nslot, LANES), F32)] * 3,
        compiler_params=_cparams("parallel"),
        name="peer_gates",
    )(s1t, s2t)


_SQRT_HALF = 0.7071067811865476
_HI16 = -65536
_GPAD = 8


def _peer_experts_body(tm, eb, nj, final, hn_ref, u_ref, v_ref, a_ref, b_ref, g_ref, x_ref, nf_ref,
                       o_ref, acc_ref, gw_ref):
    j = pl.program_id(1)
    pitch = tm + _GPAD
    half_a = N_KEYS // 2

    @pl.when(j == 0)
    def _():
        acc_ref[...] = jnp.zeros(acc_ref.shape, F32)
        key = lax.broadcasted_iota(I32, (N_KEYS, PEER_HEADS * PEER_TOPK), 0).astype(F32)

        def group(i, _):
            rows = pl.ds(pl.multiple_of(i * SUBLANES, SUBLANES), SUBLANES)
            a8, b8, g8 = a_ref[rows, :], b_ref[rows, :], g_ref[rows, :]
            for q in range(SUBLANES):
                pt = jnp.where(key == a8[q:q + 1], g8[q:q + 1], 0.0).astype(BF16)
                qt = jnp.where(key == b8[q:q + 1], 1.0, 0.0).astype(BF16)
                gt = _dot_nt(pt, qt).astype(BF16).astype(F32)
                hi = jnp.bitwise_and(pltpu.bitcast(gt[:half_a], I32), _HI16)
                lo = lax.shift_right_logical(pltpu.bitcast(gt[half_a:], I32), 16)
                gw_ref[pl.ds(i * SUBLANES + q, half_a, stride=pitch), :] = jnp.bitwise_or(hi, lo)
            return 0

        lax.fori_loop(0, tm // SUBLANES, group, 0, unroll=4)

    tiles = eb // N_KEYS
    a_lo = lax.rem(j * tiles, half_a)
    shift = jnp.where(j * tiles >= half_a, 16, 0)
    act = _dot_nt(hn_ref[...], u_ref[...])
    ws = []
    for al in range(tiles):
        word = gw_ref[pl.ds(pl.multiple_of((a_lo + al) * pitch, SUBLANES), tm), :]
        gate = pltpu.bitcast(jnp.bitwise_and(lax.shift_left(word, shift), _HI16), F32)
        x = act[:, al * N_KEYS:(al + 1) * N_KEYS]
        ws.append((gate * (x * (1.0 + lax.erf(x * _SQRT_HALF)))).astype(BF16))
    acc_ref[...] += _dot(jnp.concatenate(ws, axis=1), v_ref[...])

    @pl.when(j == nj - 1)
    def _():
        x2 = x_ref[...] + acc_ref[...]
        if final:
            o_ref[...] = _rms(x2, nf_ref[...])
        else:
            o_ref[...] = x2


def _peer_experts(hn, picks, x1, lw, norm_final, final):
    t = x1.shape[0]
    tm = _row_block(t, 512)
    eb = 8 * N_KEYS
    nj = N_EXPERTS // eb
    nslot = PEER_HEADS * PEER_TOPK
    a_id, b_id, gate = picks
    row = lambda n: pl.BlockSpec((tm, n), lambda i, j: (i, 0))
    return pl.pallas_call(
        functools.partial(_peer_experts_body, tm, eb, nj, final),
        grid=(t // tm, nj),
        in_specs=[row(D_MODEL),
                  pl.BlockSpec((eb, D_MODEL), lambda i, j: (j, 0)),
                  pl.BlockSpec((eb, D_MODEL), lambda i, j: (j, 0)),
                  row(nslot), row(nslot), row(nslot), row(D_MODEL),
                  pl.BlockSpec((1, D_MODEL), lambda i, j: (0, 0))],
        out_specs=row(D_MODEL),
        out_shape=jax.ShapeDtypeStruct((t, D_MODEL), F32),
        scratch_shapes=[pltpu.VMEM((tm, D_MODEL), F32),
                        pltpu.VMEM((N_KEYS // 2 * (tm + _GPAD), N_KEYS), I32)],
        compiler_params=_cparams("parallel", "arbitrary"),
        name="peer_experts",
    )(hn, lw["peer_u"], lw["peer_v"], a_id, b_id, gate, x1, norm_final)


def _peer(x1, lw, norm_final, final):
    hn, s1t, s2t = _peer_scores(x1, lw)
    picks = _peer_gates(s1t, s2t)
    return _peer_experts(hn, picks, x1, lw, norm_final, final)


def _pad_lanes(v, start, total=LANES):
    return jnp.zeros((1, total), F32).at[0, start:start + v.shape[0]].set(v)


def _layer_weights(l, norm_mix, w_in, b_gate, conv_w, conv_b, dt_bias, a_log, d_skip, ssm_norm, w_ssm_proj,
                   q_norm, w_qb, kv_norm, w_kvb, w_mla_proj, w_out, norm_ffn, peer_wq, peer_keys, peer_u, peer_v):
    o_dt = 3 * D_MODEL + CONV_DIM
    o_cq = o_dt + SSM_HEADS
    o_ckv = o_cq + Q_LORA
    o_kr = o_ckv + KV_LORA
    w = w_in[l]
    half = QK_ROPE // 2
    zcols = lambda n: jnp.zeros((D_MODEL, n), F32)
    kr = w[:, o_kr:o_kr + QK_ROPE]
    kr_sw = jnp.concatenate([kr[:, half:], kr[:, :half]], axis=1)
    sm1 = jnp.concatenate([kr, zcols(LANES - QK_ROPE)], axis=1)
    sm2 = jnp.concatenate([kr_sw, zcols(DT_LANE - QK_ROPE), w[:, o_dt:o_cq], zcols(LANES - DT_LANE - SSM_HEADS)], axis=1)
    wcat = jnp.concatenate([w[:, :o_dt], w[:, o_cq:o_kr], sm1, sm2], axis=1).astype(BF16)

    wq = w_qb[l].reshape(Q_LORA, MLA_HEADS, QK_NOPE + QK_ROPE)
    wq_nope = wq[:, :, :QK_NOPE].reshape(Q_LORA, MLA_HEADS * QK_NOPE)
    pe = wq[:, :, QK_NOPE:]
    pe_sw = jnp.concatenate([pe[:, :, half:], pe[:, :, :half]], axis=2)
    padpe = lambda p: jnp.pad(p, ((0, 0), (0, 0), (0, LANES - QK_ROPE))).reshape(Q_LORA, MLA_HEADS * LANES)
    wkv = w_kvb[l].reshape(KV_LORA, MLA_HEADS, QK_NOPE + V_HEAD)
    w_kb_t = jnp.transpose(wkv[:, :, :QK_NOPE], (1, 2, 0))
    w_vb = jnp.transpose(wkv[:, :, QK_NOPE:], (1, 0, 2))
    w_vb_all = wkv[:, :, QK_NOPE:].reshape(KV_LORA, MLA_HEADS * V_HEAD)
    row = lambda v: v.reshape(1, -1)
    return dict(
        norm_mix=row(norm_mix[l]), wcat=wcat, b_gate=row(b_gate[l]),
        conv_w=conv_w[l], conv_b=row(conv_b[l]),
        dt_bias_p=_pad_lanes(dt_bias[l], DT_LANE), a_log_p=_pad_lanes(a_log[l], DT_LANE),
        d_skip_e=row(jnp.repeat(d_skip[l], SSM_HEAD_DIM)), d_skip_row=row(d_skip[l]),
        ssm_norm=row(ssm_norm[l]), ssm_norm_t=ssm_norm[l].reshape(SSM_HEADS, SSM_HEAD_DIM).T,
        w_ssm_proj=w_ssm_proj[l].astype(BF16),
        q_norm=row(q_norm[l]), kv_norm=row(kv_norm[l]),
        wq_nope=wq_nope.astype(BF16), wq_pe=padpe(pe).astype(BF16), wq_pe_sw=padpe(pe_sw).astype(BF16),
        w_kb_t=w_kb_t.astype(BF16), w_vb=w_vb.astype(BF16), w_vb_all=w_vb_all.astype(BF16),
        w_mla_proj=w_mla_proj[l].astype(BF16), w_out=w_out[l].astype(BF16),
        norm_ffn=row(norm_ffn[l]), peer_wq=peer_wq[l].astype(BF16), peer_keys=peer_keys[l].astype(BF16),
        peer_u=peer_u[l].astype(BF16), peer_v=peer_v[l].astype(BF16))


def _rope_tables(pos):
    half = QK_ROPE // 2
    inv = ROPE_THETA ** (-jnp.arange(half, dtype=F32) / half)
    ang = pos.astype(F32)[:, None] * inv[None, :]
    cos, sin = jnp.cos(ang), jnp.sin(ang)
    pad = jnp.zeros((pos.shape[0], LANES - QK_ROPE), F32)
    c128 = jnp.concatenate([cos, cos, pad], axis=1)
    s128 = jnp.concatenate([-sin, sin, pad], axis=1)
    ktab = jnp.concatenate([c128, s128], axis=1)
    qtab = jnp.concatenate([jnp.tile(c128, (1, MLA_HEADS)), jnp.tile(s128, (1, MLA_HEADS))], axis=1)
    return qtab, ktab


def _prompt_layer(x2d, nb, seq, lw, tabs, norm_final, final):
    gates, z, xbc, cq, ckv, sm1, sm2 = _in_proj(x2d, lw["norm_mix"], lw["wcat"])
    yn, h_new = _ssd_prompt(xbc, sm2, z, lw, nb, seq)
    qcat, kcat, lat, kpe = _mla_prep(cq, ckv, sm1, sm2, tabs[0], tabs[1], lw, nb, seq)
    o = _attn_prompt(qcat, kcat, lw["w_vb"], nb, seq)
    x1 = _mix(x2d, gates, yn, o, lw)
    x2 = _peer(x1, lw, norm_final, final)
    conv_new = xbc.reshape(nb, seq, CONV_DIM)[:, seq - (CONV_WIDTH - 1):, :]
    return x2, h_new, conv_new, lat.reshape(nb, seq, KV_LORA), kpe.reshape(nb, seq, QK_ROPE)


def _sample_layer(x2d, layer, lw, tabs, state_ssm, state_conv, cache_latent, cache_krope, page_table,
                  norm_final, final):
    nb = x2d.shape[0]
    gates, z, xbc, cq, ckv, sm1, sm2 = _in_proj(x2d, lw["norm_mix"], lw["wcat"])
    prev = state_conv[layer]
    u, dt = _ssd_step_pre(xbc, jnp.transpose(prev, (1, 0, 2)), sm2, lw)
    to_t = lambda v: jnp.transpose(v.reshape(nb, SSM_HEADS, SSM_HEAD_DIM), (0, 2, 1))
    n_bc = SSM_GROUPS * D_STATE
    h_new, yt = _ssd_step(state_ssm, layer, to_t(u[:, :D_INNER]), to_t(z),
                          u[:, D_INNER:D_INNER + n_bc].reshape(nb, 1, n_bc),
                          u[:, D_INNER + n_bc:].reshape(nb, 1, n_bc), dt.reshape(nb, 1, LANES), lw)
    yn = jnp.transpose(yt, (0, 2, 1)).reshape(nb, D_INNER).astype(BF16)
    conv_new = jnp.concatenate([prev[:, 1:], xbc[:, None, :]], axis=1)

    qcat, kcat, lat, kpe = _mla_prep(cq, ckv, sm1, sm2, tabs[0], tabs[1], lw, 1, nb)
    qdec = jnp.transpose(qcat[0], (1, 0, 2))
    o = _attn_decode(page_table, qdec, kcat.reshape(nb, 1, QK_CAT), lw["w_vb_all"], cache_latent, cache_krope, layer)
    x1 = _mix(x2d, gates, yn, o.reshape(nb, MLA_HEADS * V_HEAD), lw)
    pad = (-nb) % LANES
    x2 = _peer(jnp.pad(x1, ((0, pad), (0, 0))), lw, norm_final, final)[:nb]
    return x2, h_new, conv_new, lat.reshape(nb, 1, KV_LORA), kpe.reshape(nb, 1, QK_ROPE)


def kernel(x_prompt, x_sample, cache_latent, cache_krope, state_ssm, state_conv, page_table, norm_mix, w_in, b_gate,
           conv_w, conv_b, dt_bias, a_log, d_skip, ssm_norm, w_ssm_proj, q_norm, w_qb, kv_norm, w_kvb, w_mla_proj,
           w_out, norm_ffn, peer_wq, peer_keys, peer_u, peer_v, norm_final):
    bp, sp = x_prompt.shape[:2]
    bs, ss = x_sample.shape[:2]
    depth = w_in.shape[0]
    assert ss == 1 and sp % SSD_CHUNK == 0
    past_len = page_table.shape[1] * cache_latent.shape[2]
    tabs_p = _rope_tables(jnp.arange(sp, dtype=I32))
    tabs_s = _rope_tables(jnp.full((bs,), past_len, I32))
    nf = norm_final.reshape(1, D_MODEL)
    cache_krope = jnp.swapaxes(cache_krope, 2, 3).astype(BF16)
    xp = x_prompt.reshape(bp * sp, D_MODEL)
    xs = x_sample.reshape(bs, D_MODEL)
    outs_p, outs_s = [], []
    for l in range(depth):
        lw = _layer_weights(l, norm_mix, w_in, b_gate, conv_w, conv_b, dt_bias, a_log, d_skip, ssm_norm, w_ssm_proj,
                            q_norm, w_qb, kv_norm, w_kvb, w_mla_proj, w_out, norm_ffn, peer_wq, peer_keys, peer_u,
                            peer_v)
        final = l == depth - 1
        xp, hp, cp, lp, kp = _prompt_layer(xp, bp, sp, lw, tabs_p, nf, final)
        xs, hs, cs, lsn, ksn = _sample_layer(xs, l, lw, tabs_s, state_ssm, state_conv, cache_latent, cache_krope,
                                             page_table, nf, final)
        outs_p.append((lp, kp, hp, cp))
        outs_s.append((lsn, ksn, hs, cs))
    stack = lambda outs, i: jnp.stack([o[i] for o in outs])
    return (xp.reshape(bp, sp, D_MODEL), xs.reshape(bs, ss, D_MODEL),
            stack(outs_p, 0), stack(outs_p, 1), stack(outs_p, 2), stack(outs_p, 3),
            stack(outs_s, 0), stack(outs_s, 1), stack(outs_s, 2), stack(outs_s, 3))
```

```python
import functools

import numpy as np
import jax
import jax.numpy as jnp
from jax import lax
from jax.experimental import pallas as pl
from jax.experimental.pallas import tpu as pltpu

F32 = jnp.float32
BF16 = jnp.bfloat16
I32 = jnp.int32

D_MODEL = 1024
D_INNER = 1024
SSM_HEAD_DIM = 64
SSM_HEADS = 16
SSM_GROUPS = 2
SSM_HPG = 8
D_STATE = 128
CONV_WIDTH = 4
CONV_DIM = D_INNER + 2 * SSM_GROUPS * D_STATE
SSD_CHUNK = 128
MLA_HEADS = 8
QK_NOPE = 64
QK_ROPE = 32
V_HEAD = 64
Q_LORA = 384
KV_LORA = 256
ROPE_THETA = 10000.0
ATTN_SCALE = (QK_NOPE + QK_ROPE) ** -0.5
PEER_HEADS = 8
N_KEYS = 128
N_EXPERTS = N_KEYS * N_KEYS
PEER_D_KEY = 256
PEER_TOPK = 16
EPS = 1e-6

LANES = 128
QK_CAT = KV_LORA + LANES
DT_LANE = 64
VMEM_LIMIT = 56 * 1024 * 1024

_HI = lax.Precision.HIGHEST


def _cparams(*sem):
    return pltpu.CompilerParams(dimension_semantics=sem, vmem_limit_bytes=VMEM_LIMIT)


def _dot(a, b):
    return jnp.dot(a, b, preferred_element_type=F32)


def _dot_nt(a, b):
    return lax.dot_general(a, b, (((1,), (1,)), ((), ())), preferred_element_type=F32)


def _dot_tn(a, b):
    return lax.dot_general(a, b, (((0,), (0,)), ((), ())), preferred_element_type=F32)


def _rms(x, w):
    ms = jnp.mean(x * x, axis=-1, keepdims=True)
    return (x * lax.rsqrt(ms + EPS)) * w


def _sigmoid(x):
    return jax.nn.sigmoid(x)


def _softplus(x):
    return jnp.maximum(x, 0.0) + jnp.log1p(jnp.exp(-jnp.abs(x)))


def _row_block(t, cap):
    if t <= cap:
        return t
    b = cap
    while t % b:
        b //= 2
    return b


_IN_WIDTHS = (2 * D_MODEL, D_INNER, CONV_DIM, Q_LORA, KV_LORA, LANES, LANES)


def _in_proj_body(x_ref, nw_ref, w_ref, *out_refs):
    h = _rms(x_ref[...], nw_ref[...]).astype(BF16)
    off = 0
    for ref, wd in zip(out_refs, _IN_WIDTHS):
        ref[...] = _dot(h, w_ref[:, off:off + wd])
        off += wd


def _in_proj(x2d, nw, wcat):
    t = x2d.shape[0]
    tm = _row_block(t, 256)
    wtot = sum(_IN_WIDTHS)
    return pl.pallas_call(
        _in_proj_body,
        grid=(t // tm,),
        in_specs=[pl.BlockSpec((tm, D_MODEL), lambda i: (i, 0)),
                  pl.BlockSpec((1, D_MODEL), lambda i: (0, 0)),
                  pl.BlockSpec((D_MODEL, wtot), lambda i: (0, 0))],
        out_specs=[pl.BlockSpec((tm, wd), lambda i: (i, 0)) for wd in _IN_WIDTHS],
        out_shape=[jax.ShapeDtypeStruct((t, wd), F32) for wd in _IN_WIDTHS],
        compiler_params=_cparams("parallel"),
        name="in_proj",
    )(x2d, nw, wcat)


def _head_expand():
    r = lax.broadcasted_iota(I32, (LANES, D_INNER), 0)
    c = lax.broadcasted_iota(I32, (LANES, D_INNER), 1)
    return (jnp.right_shift(c, 6) == (r - DT_LANE)).astype(F32)


def _ssd_prompt_body(nc, xbc_ref, sm_ref, z_ref, cw_ref, cb_ref, dtb_ref, alog_ref, dsk_ref, nrm_ref,
                     yn_ref, hfin_ref, xpad_ref, st_ref):
    c = pl.program_id(1)
    cl = SSD_CHUNK

    @pl.when(c == 0)
    def _():
        xpad_ref[0:8, :] = jnp.zeros((8, CONV_DIM), F32)
        st_ref[...] = jnp.zeros_like(st_ref)

    xpad_ref[8:8 + cl, :] = xbc_ref[...]
    conv = cb_ref[...]
    for k in range(CONV_WIDTH):
        conv = conv + xpad_ref[5 + k:5 + k + cl, :] * cw_ref[k:k + 1, :]
    xpad_ref[0:8, :] = xpad_ref[cl:cl + 8, :]
    u = conv * _sigmoid(conv)
    xs = u[:, :D_INNER]
    bb = u[:, D_INNER:D_INNER + SSM_GROUPS * D_STATE].astype(BF16)
    cb = u[:, D_INNER + SSM_GROUPS * D_STATE:].astype(BF16)

    row = lax.broadcasted_iota(I32, (cl, cl), 0)
    col = lax.broadcasted_iota(I32, (cl, cl), 1)
    dt_ok = (col >= DT_LANE) & (col < DT_LANE + SSM_HEADS)
    dt = jnp.where(dt_ok, _softplus(sm_ref[...] + dtb_ref[...]), 0.0)
    a = dt * (-jnp.exp(alog_ref[...]))
    tri = (row >= col).astype(F32)
    a_cs = jnp.dot(tri, a, precision=_HI, preferred_element_type=F32)
    a_cs_t = a_cs.T
    ea = jnp.exp(a_cs)
    dte = jnp.exp(a_cs[cl - 1:cl, :] - a_cs)
    ex = _head_expand()
    dt_e = jnp.dot(dt, ex, precision=_HI, preferred_element_type=F32)
    ea_e = jnp.dot(ea, ex, precision=_HI, preferred_element_type=F32)
    dte_e = jnp.dot(dte, ex, precision=_HI, preferred_element_type=F32)

    xdt = xs * dt_e
    xdt_b = xdt.astype(BF16)
    causal = row >= col
    ys = []
    for g in range(SSM_GROUPS):
        cbm = _dot_nt(cb[:, g * D_STATE:(g + 1) * D_STATE], bb[:, g * D_STATE:(g + 1) * D_STATE])
        for e in range(SSM_HPG):
            h = g * SSM_HPG + e
            seg = a_cs[:, DT_LANE + h:DT_LANE + h + 1] - a_cs_t[DT_LANE + h:DT_LANE + h + 1, :]
            lm = jnp.exp(jnp.where(causal, seg, -jnp.inf))
            ys.append(_dot((cbm * lm).astype(BF16), xdt_b[:, h * SSM_HEAD_DIM:(h + 1) * SSM_HEAD_DIM]))
    y_diag = jnp.concatenate(ys, axis=1)

    xw = (xdt * dte_e).astype(BF16)
    dec = ea_e[cl - 1:cl, :]
    half = D_INNER // SSM_GROUPS
    yoffs = []
    for g in range(SSM_GROUPS):
        s_prev = st_ref[g]
        yoffs.append(_dot(cb[:, g * D_STATE:(g + 1) * D_STATE], s_prev.astype(BF16)))
        st_ref[g] = dec[:, g * half:(g + 1) * half] * s_prev + _dot_tn(
            bb[:, g * D_STATE:(g + 1) * D_STATE], xw[:, g * half:(g + 1) * half])
    y = y_diag + jnp.concatenate(yoffs, axis=1) * ea_e + dsk_ref[...] * xs
    zv = z_ref[...]
    y = y * (zv * _sigmoid(zv))
    outs = []
    for g in range(SSM_GROUPS):
        outs.append(_rms(y[:, g * half:(g + 1) * half], nrm_ref[:, g * half:(g + 1) * half]))
    yn_ref[...] = jnp.concatenate(outs, axis=1).astype(BF16)

    @pl.when(c == nc - 1)
    def _():
        for g in range(SSM_GROUPS):
            s_t = st_ref[g].T
            for e in range(SSM_HPG):
                hfin_ref[0, g, e] = s_t[e * SSM_HEAD_DIM:(e + 1) * SSM_HEAD_DIM, :]


def _ssd_prompt(xbc, sm2, z, lw, nb, seq):
    nc = seq // SSD_CHUNK
    cl = SSD_CHUNK
    rowmap = lambda b, c: (b * nc + c, 0)
    full = lambda b, c: (0, 0)
    return pl.pallas_call(
        functools.partial(_ssd_prompt_body, nc),
        grid=(nb, nc),
        in_specs=[pl.BlockSpec((cl, CONV_DIM), rowmap),
                  pl.BlockSpec((cl, LANES), rowmap),
                  pl.BlockSpec((cl, D_INNER), rowmap),
                  pl.BlockSpec((CONV_WIDTH, CONV_DIM), full),
                  pl.BlockSpec((1, CONV_DIM), full),
                  pl.BlockSpec((1, LANES), full),
                  pl.BlockSpec((1, LANES), full),
                  pl.BlockSpec((1, D_INNER), full),
                  pl.BlockSpec((1, D_INNER), full)],
        out_specs=[pl.BlockSpec((cl, D_INNER), rowmap),
                   pl.BlockSpec((1, SSM_GROUPS, SSM_HPG, SSM_HEAD_DIM, D_STATE), lambda b, c: (b, 0, 0, 0, 0))],
        out_shape=[jax.ShapeDtypeStruct((nb * seq, D_INNER), BF16),
                   jax.ShapeDtypeStruct((nb, SSM_GROUPS, SSM_HPG, SSM_HEAD_DIM, D_STATE), F32)],
        scratch_shapes=[pltpu.VMEM((cl + 8, CONV_DIM), F32),
                        pltpu.VMEM((SSM_GROUPS, D_STATE, D_INNER // SSM_GROUPS), F32)],
        compiler_params=_cparams("parallel", "arbitrary"),
        name="ssd_prompt",
    )(xbc, sm2, z, lw["conv_w"], lw["conv_b"], lw["dt_bias_p"], lw["a_log_p"], lw["d_skip_e"], lw["ssm_norm"])


def _ssd_step_pre_body(xbc_ref, prev_ref, sm_ref, cw_ref, cb_ref, dtb_ref, u_ref, dt_ref):
    conv = cb_ref[...]
    for k in range(CONV_WIDTH - 1):
        conv = conv + prev_ref[k] * cw_ref[k:k + 1, :]
    conv = conv + xbc_ref[...] * cw_ref[CONV_WIDTH - 1:CONV_WIDTH, :]
    u_ref[...] = conv * _sigmoid(conv)
    dt_ref[...] = _softplus(sm_ref[...] + dtb_ref[...])


def _ssd_step_pre(xbc, prev_t, sm2, lw):
    nb = xbc.shape[0]
    return pl.pallas_call(
        _ssd_step_pre_body,
        out_shape=[jax.ShapeDtypeStruct((nb, CONV_DIM), F32), jax.ShapeDtypeStruct((nb, LANES), F32)],
        compiler_params=pltpu.CompilerParams(vmem_limit_bytes=VMEM_LIMIT),
        name="ssd_step_pre",
    )(xbc, prev_t, sm2, lw["conv_w"], lw["conv_b"], lw["dt_bias_p"])


def _ssd_step_body(h0_ref, xt_ref, zt_ref, b_ref, c_ref, dt_ref, alog_ref, dsk_ref, nrm_ref, hn_ref, yt_ref):
    xt = xt_ref[0]
    dtv = dt_ref[0]
    av = -jnp.exp(alog_ref[...])
    lane = lax.broadcasted_iota(I32, (SSM_HEAD_DIM, SSM_HEADS), 1)
    y = jnp.zeros((SSM_HEAD_DIM, SSM_HEADS), F32)
    for g in range(SSM_GROUPS):
        bg = b_ref[0][:, g * D_STATE:(g + 1) * D_STATE]
        cg = c_ref[0][:, g * D_STATE:(g + 1) * D_STATE]
        for e in range(SSM_HPG):
            h = g * SSM_HPG + e
            dth = dtv[:, DT_LANE + h:DT_LANE + h + 1]
            dec = jnp.exp(dth * av[:, DT_LANE + h:DT_LANE + h + 1])
            hn = dec * h0_ref[0, g, e] + (xt[:, h:h + 1] * dth) * bg
            hn_ref[0, g, e] = hn
            ycol = jnp.sum(hn * cg, axis=1, keepdims=True)
            y = jnp.where(lane == h, ycol, y)
    y = y + dsk_ref[...] * xt
    zv = zt_ref[0]
    y = y * (zv * _sigmoid(zv))
    sq = y * y
    in_g0 = lane < SSM_HPG
    n_g = float(D_INNER // SSM_GROUPS)
    ms0 = jnp.sum(jnp.sum(jnp.where(in_g0, sq, 0.0), axis=1, keepdims=True), axis=0, keepdims=True) / n_g
    ms1 = jnp.sum(jnp.sum(jnp.where(in_g0, 0.0, sq), axis=1, keepdims=True), axis=0, keepdims=True) / n_g
    scale = jnp.where(in_g0, lax.rsqrt(ms0 + EPS), lax.rsqrt(ms1 + EPS))
    yt_ref[0] = (y * scale) * nrm_ref[...]


def _ssd_step(h0_all, layer, xt, zt, bm, cm, dt, lw):
    nb = xt.shape[0]
    st_block = (None, 1, SSM_GROUPS, SSM_HPG, SSM_HEAD_DIM, D_STATE)
    pt_block = (1, SSM_HEAD_DIM, SSM_HEADS)
    vec = lambda n: pl.BlockSpec((1, 1, n), lambda b: (b, 0, 0))
    full = lambda r, n: pl.BlockSpec((r, n), lambda b: (0, 0))
    return pl.pallas_call(
        _ssd_step_body,
        grid=(nb,),
        in_specs=[pl.BlockSpec(st_block, lambda b: (layer, b, 0, 0, 0, 0)),
                  pl.BlockSpec(pt_block, lambda b: (b, 0, 0)),
                  pl.BlockSpec(pt_block, lambda b: (b, 0, 0)),
                  vec(SSM_GROUPS * D_STATE), vec(SSM_GROUPS * D_STATE), vec(LANES),
                  full(1, LANES), full(1, SSM_HEADS), full(SSM_HEAD_DIM, SSM_HEADS)],
        out_specs=[pl.BlockSpec((1, SSM_GROUPS, SSM_HPG, SSM_HEAD_DIM, D_STATE), lambda b: (b, 0, 0, 0, 0)),
                   pl.BlockSpec(pt_block, lambda b: (b, 0, 0))],
        out_shape=[jax.ShapeDtypeStruct((nb, SSM_GROUPS, SSM_HPG, SSM_HEAD_DIM, D_STATE), F32),
                   jax.ShapeDtypeStruct((nb, SSM_HEAD_DIM, SSM_HEADS), F32)],
        compiler_params=_cparams("parallel"),
        name="ssd_step",
    )(h0_all, xt, zt, bm, cm, dt, lw["a_log_p"], lw["d_skip_row"], lw["ssm_norm_t"])


def _mla_prep_body(cq_ref, ckv_ref, sm1_ref, sm2_ref, qtab_ref, ktab_ref, qn_ref, kvn_ref,
                   wn_ref, wpe_ref, wpes_ref, wkb_ref, qcat_ref, kcat_ref, lat_ref, kpe_ref):
    hq = _rms(cq_ref[...], qn_ref[...]).astype(BF16)
    qn = _dot(hq, wn_ref[...]).astype(BF16)
    npe = MLA_HEADS * LANES
    qpe = (_dot(hq, wpe_ref[...]) * qtab_ref[:, :npe] + _dot(hq, wpes_ref[...]) * qtab_ref[:, npe:]) * ATTN_SCALE
    for h in range(MLA_HEADS):
        ql = _dot(qn[:, h * QK_NOPE:(h + 1) * QK_NOPE], wkb_ref[h]) * ATTN_SCALE
        qcat_ref[0, h] = jnp.concatenate([ql, qpe[:, h * LANES:(h + 1) * LANES]], axis=1).astype(BF16)
    lat = _rms(ckv_ref[...], kvn_ref[...])
    lat_ref[...] = lat
    kpe = sm1_ref[...] * ktab_ref[:, :LANES] + sm2_ref[...] * ktab_ref[:, LANES:]
    kpe_ref[...] = kpe[:, :QK_ROPE]
    kcat_ref[...] = jnp.concatenate([lat, kpe], axis=1).astype(BF16)


def _mla_prep(cq, ckv, sm1, sm2, qtab, ktab, lw, nb, seq):
    t = nb * seq
    tm = _row_block(seq, 512)
    nl = seq // tm
    rowmap = lambda b, i: (b * nl + i, 0)
    tabmap = lambda b, i: (i, 0)
    full2 = lambda b, i: (0, 0)
    npe = MLA_HEADS * LANES
    return pl.pallas_call(
        _mla_prep_body,
        grid=(nb, nl),
        in_specs=[pl.BlockSpec((tm, Q_LORA), rowmap),
                  pl.BlockSpec((tm, KV_LORA), rowmap),
                  pl.BlockSpec((tm, LANES), rowmap),
                  pl.BlockSpec((tm, LANES), rowmap),
                  pl.BlockSpec((tm, 2 * npe), tabmap),
                  pl.BlockSpec((tm, 2 * LANES), tabmap),
                  pl.BlockSpec((1, Q_LORA), full2),
                  pl.BlockSpec((1, KV_LORA), full2),
                  pl.BlockSpec((Q_LORA, MLA_HEADS * QK_NOPE), full2),
                  pl.BlockSpec((Q_LORA, npe), full2),
                  pl.BlockSpec((Q_LORA, npe), full2),
                  pl.BlockSpec((MLA_HEADS, QK_NOPE, KV_LORA), lambda b, i: (0, 0, 0))],
        out_specs=[pl.BlockSpec((1, MLA_HEADS, tm, QK_CAT), lambda b, i: (b, 0, i, 0)),
                   pl.BlockSpec((tm, QK_CAT), rowmap),
                   pl.BlockSpec((tm, KV_LORA), rowmap),
                   pl.BlockSpec((tm, QK_ROPE), rowmap)],
        out_shape=[jax.ShapeDtypeStruct((nb, MLA_HEADS, seq, QK_CAT), BF16),
                   jax.ShapeDtypeStruct((t, QK_CAT), BF16),
                   jax.ShapeDtypeStruct((t, KV_LORA), F32),
                   jax.ShapeDtypeStruct((t, QK_ROPE), F32)],
        compiler_params=_cparams("parallel", "parallel"),
        name="mla_prep",
    )(cq, ckv, sm1, sm2, qtab, ktab, lw["q_norm"], lw["kv_norm"], lw["wq_nope"], lw["wq_pe"], lw["wq_pe_sw"],
      lw["w_kb_t"])


def _attn_prompt_body(tq, tk, nk, q_ref, k_ref, wvb_ref, o_ref, m_ref, l_ref, acc_ref):
    i = pl.program_id(1)
    j = pl.program_id(2)

    @pl.when(j == 0)
    def _():
        m_ref[...] = jnp.full(m_ref.shape, -jnp.inf, F32)
        l_ref[...] = jnp.zeros(l_ref.shape, F32)
        acc_ref[...] = jnp.zeros(acc_ref.shape, F32)

    @pl.when(j * tk <= i * tq + tq - 1)
    def _():
        k = k_ref[...]
        v = k[:, :KV_LORA]
        qpos = i * tq + lax.broadcasted_iota(I32, (tq, tk), 0)
        kpos = j * tk + lax.broadcasted_iota(I32, (tq, tk), 1)
        visible = kpos <= qpos
        ahead = 2
        s_q = [_dot_nt(q_ref[0, h], k) for h in range(ahead)]
        for h in range(MLA_HEADS):
            hs = slice(h * tq, (h + 1) * tq)
            s = jnp.where(visible, s_q.pop(0), -jnp.inf)
            if h + ahead < MLA_HEADS:
                s_q.append(_dot_nt(q_ref[0, h + ahead], k))
            m_prev = m_ref[hs]
            m_new = jnp.maximum(m_prev, jnp.max(s, axis=1, keepdims=True))
            p = jnp.exp(s - m_new)
            alpha = jnp.exp(m_prev - m_new)
            l_ref[hs] = alpha * l_ref[hs] + jnp.sum(p, axis=1, keepdims=True)
            acc_ref[hs] = alpha * acc_ref[hs] + _dot(p.astype(BF16), v)
            m_ref[hs] = m_new

    @pl.when(j == nk - 1)
    def _():
        outs = []
        for h in range(MLA_HEADS):
            hs = slice(h * tq, (h + 1) * tq)
            outs.append(_dot((acc_ref[hs] / l_ref[hs]).astype(BF16), wvb_ref[h]))
        o_ref[...] = jnp.concatenate(outs, axis=1).astype(BF16)


def _attn_prompt(qcat, kcat, w_vb, nb, seq):
    tq = _row_block(seq, 256)
    tk = _row_block(seq, 512)
    nq, nk = seq // tq, seq // tk

    def kmap(b, i, j):
        return (b * nk + jnp.minimum(j, (i * tq + tq - 1) // tk), 0)

    return pl.pallas_call(
        functools.partial(_attn_prompt_body, tq, tk, nk),
        grid=(nb, nq, nk),
        in_specs=[pl.BlockSpec((1, MLA_HEADS, tq, QK_CAT), lambda b, i, j: (b, 0, i, 0)),
                  pl.BlockSpec((tk, QK_CAT), kmap),
                  pl.BlockSpec((MLA_HEADS, KV_LORA, V_HEAD), lambda b, i, j: (0, 0, 0))],
        out_specs=pl.BlockSpec((tq, MLA_HEADS * V_HEAD), lambda b, i, j: (b * nq + i, 0)),
        out_shape=jax.ShapeDtypeStruct((nb * seq, MLA_HEADS * V_HEAD), BF16),
        scratch_shapes=[pltpu.VMEM((MLA_HEADS * tq, 1), F32), pltpu.VMEM((MLA_HEADS * tq, 1), F32),
                        pltpu.VMEM((MLA_HEADS * tq, KV_LORA), F32)],
        compiler_params=_cparams("parallel", "parallel", "arbitrary"),
        name="attn_prompt",
    )(qcat, kcat, w_vb)


def _attn_decode_body(pg, nj, pt_ref, q_ref, kc_ref, wvb_ref, *refs):
    lat_refs = refs[:pg]
    kr_refs = refs[pg:2 * pg]
    o_ref, m_ref, l_ref, acc_ref = refs[2 * pg:]
    j = pl.program_id(1)

    @pl.when(j == 0)
    def _():
        m_ref[...] = jnp.full(m_ref.shape, -jnp.inf, F32)
        l_ref[...] = jnp.zeros(l_ref.shape, F32)
        acc_ref[...] = jnp.zeros(acc_ref.shape, F32)

    q = q_ref[0]
    ql = q[:, :KV_LORA]
    qr = q[:, KV_LORA:KV_LORA + QK_ROPE]
    kvs = [lat_refs[i][...].astype(BF16) for i in range(pg)]
    ss = [_dot_nt(ql, kvs[i]) + _dot(qr, kr_refs[i][...]) for i in range(pg)]
    m_prev = m_ref[...]
    m_blk = ss[0]
    for i in range(1, pg):
        m_blk = jnp.maximum(m_blk, ss[i])
    m_new = jnp.maximum(m_prev, jnp.max(m_blk, axis=1, keepdims=True))
    ps = [jnp.exp(ss[i] - m_new) for i in range(pg)]
    p_sum = ps[0]
    for i in range(1, pg):
        p_sum = p_sum + ps[i]
    pvs = [_dot(ps[i].astype(BF16), kvs[i]) for i in range(pg)]
    while len(pvs) > 1:
        pvs = [pvs[i] + pvs[i + 1] for i in range(0, len(pvs), 2)]
    alpha = jnp.exp(m_prev - m_new)
    l_ref[...] = alpha * l_ref[...] + jnp.sum(p_sum, axis=1, keepdims=True)
    acc_ref[...] = alpha * acc_ref[...] + pvs[0]
    m_ref[...] = m_new

    @pl.when(j == nj - 1)
    def _():
        kc = kc_ref[0].astype(F32)
        s = jnp.sum(q.astype(F32) * kc, axis=1, keepdims=True)
        m_prev = m_ref[...]
        m_new = jnp.maximum(m_prev, s)
        p = jnp.exp(s - m_new)
        alpha = jnp.exp(m_prev - m_new)
        l_fin = alpha * l_ref[...] + p
        acc = alpha * acc_ref[...] + p.astype(BF16).astype(F32) * kc[:, :KV_LORA]
        o = (acc / l_fin).astype(BF16)
        full = _dot(o, wvb_ref[...])
        hrow = lax.broadcasted_iota(I32, full.shape, 0)
        hcol = jnp.right_shift(lax.broadcasted_iota(I32, full.shape, 1), 6)
        o_ref[0] = jnp.sum(jnp.where(hrow == hcol, full, 0.0), axis=0, keepdims=True).astype(BF16)


def _attn_decode(page_table, qdec, kcur, w_vb_all, cache_latent, cache_krope, layer):
    nb, n_pages = page_table.shape
    page = cache_latent.shape[2]
    pg = 32
    while n_pages % pg:
        pg //= 2
    nj = n_pages // pg

    def pmap(i):
        return lambda b, j, pt: (layer, pt[b, j * pg + i], 0, 0)

    in_specs = [pl.BlockSpec((1, MLA_HEADS, QK_CAT), lambda b, j, pt: (b, 0, 0)),
                pl.BlockSpec((1, 1, QK_CAT), lambda b, j, pt: (b, 0, 0)),
                pl.BlockSpec((KV_LORA, MLA_HEADS * V_HEAD), lambda b, j, pt: (0, 0))]
    in_specs += [pl.BlockSpec((None, None, page, KV_LORA), pmap(i)) for i in range(pg)]
    in_specs += [pl.BlockSpec((None, None, QK_ROPE, page), pmap(i)) for i in range(pg)]
    grid_spec = pltpu.PrefetchScalarGridSpec(
        num_scalar_prefetch=1,
        grid=(nb, nj),
        in_specs=in_specs,
        out_specs=pl.BlockSpec((1, 1, MLA_HEADS * V_HEAD), lambda b, j, pt: (b, 0, 0)),
        scratch_shapes=[pltpu.VMEM((MLA_HEADS, 1), F32), pltpu.VMEM((MLA_HEADS, 1), F32),
                        pltpu.VMEM((MLA_HEADS, KV_LORA), F32)])
    return pl.pallas_call(
        functools.partial(_attn_decode_body, pg, nj),
        grid_spec=grid_spec,
        out_shape=jax.ShapeDtypeStruct((nb, 1, MLA_HEADS * V_HEAD), BF16),
        compiler_params=_cparams("parallel", "arbitrary"),
        name="attn_decode",
    )(page_table, qdec, kcur, w_vb_all, *([cache_latent] * pg), *([cache_krope] * pg))


def _mix_body(x_ref, g_ref, bg_ref, yn_ref, o_ref, wssm_ref, wmla_ref, wout_ref, x1_ref):
    ya = _dot(yn_ref[...], wssm_ref[...])
    yb = _dot(o_ref[...], wmla_ref[...])
    gate = _sigmoid(g_ref[...] + bg_ref[...])
    mix = gate[:, :D_MODEL] * ya + gate[:, D_MODEL:] * yb
    x1_ref[...] = x_ref[...] + _dot(mix.astype(BF16), wout_ref[...])


def _mix(x2d, gates, yn, o, lw):
    t = x2d.shape[0]
    tm = _row_block(t, 512)
    row = lambda n: pl.BlockSpec((tm, n), lambda i: (i, 0))
    full = lambda r, n: pl.BlockSpec((r, n), lambda i: (0, 0))
    return pl.pallas_call(
        _mix_body,
        grid=(t // tm,),
        in_specs=[row(D_MODEL), row(2 * D_MODEL), full(1, 2 * D_MODEL), row(D_INNER), row(MLA_HEADS * V_HEAD),
                  full(D_INNER, D_MODEL), full(MLA_HEADS * V_HEAD, D_MODEL), full(D_MODEL, D_MODEL)],
        out_specs=row(D_MODEL),
        out_shape=jax.ShapeDtypeStruct((t, D_MODEL), F32),
        compiler_params=_cparams("parallel"),
        name="mix",
    )(x2d, gates, lw["b_gate"], yn, o, lw["w_ssm_proj"], lw["w_mla_proj"], lw["w_out"])


def _peer_scores_body(x_ref, nw_ref, wq_ref, keys_ref, hn_ref, s1_ref, s2_ref):
    hn = _rms(x_ref[...], nw_ref[...]).astype(BF16)
    hn_ref[...] = hn
    q = _dot(hn, wq_ref[...]).astype(BF16)
    half = PEER_D_KEY // 2
    for h in range(PEER_HEADS):
        s1_ref[h] = _dot_nt(keys_ref[h, 0], q[:, h * PEER_D_KEY:h * PEER_D_KEY + half])
        s2_ref[h] = _dot_nt(keys_ref[h, 1], q[:, h * PEER_D_KEY + half:(h + 1) * PEER_D_KEY])


def _peer_scores(x1, lw):
    t = x1.shape[0]
    tm = _row_block(t, 512)
    sblock = pl.BlockSpec((PEER_HEADS, N_KEYS, tm), lambda i: (0, 0, i))
    return pl.pallas_call(
        _peer_scores_body,
        grid=(t // tm,),
        in_specs=[pl.BlockSpec((tm, D_MODEL), lambda i: (i, 0)),
                  pl.BlockSpec((1, D_MODEL), lambda i: (0, 0)),
                  pl.BlockSpec((D_MODEL, PEER_HEADS * PEER_D_KEY), lambda i: (0, 0)),
                  pl.BlockSpec((PEER_HEADS, 2, N_KEYS, PEER_D_KEY // 2), lambda i: (0, 0, 0, 0))],
        out_specs=[pl.BlockSpec((tm, D_MODEL), lambda i: (i, 0)), sblock, sblock],
        out_shape=[jax.ShapeDtypeStruct((t, D_MODEL), BF16),
                   jax.ShapeDtypeStruct((PEER_HEADS, N_KEYS, t), F32),
                   jax.ShapeDtypeStruct((PEER_HEADS, N_KEYS, t), F32)],
        compiler_params=_cparams("parallel"),
        name="peer_scores",
    )(x1, lw["norm_ffn"], lw["peer_wq"], lw["peer_keys"])


SUBLANES = 8


def _top16_rows(s):
    n = s.shape[0]
    row = lax.broadcasted_iota(I32, s.shape, 0)
    r16 = lax.broadcasted_iota(I32, (PEER_TOPK, LANES), 0)

    def body(r, carry):
        rank, vals = carry
        alive = rank == PEER_TOPK
        m = jnp.max(jnp.where(alive, s, -jnp.inf), axis=0, keepdims=True)
        idx = jnp.min(jnp.where(alive & (s == m), row, n), axis=0, keepdims=True)
        rank = jnp.where(row == idx, r, rank)
        vals = jnp.where(r16 == r, m, vals)
        return rank, vals

    return lax.fori_loop(0, PEER_TOPK, body,
                         (jnp.full(s.shape, PEER_TOPK, I32), jnp.zeros((PEER_TOPK, LANES), F32)))


def _batcher_net(n):
    def merge(lo, hi, r):
        step = r * 2
        if step < hi - lo:
            yield from merge(lo, hi, step)
            yield from merge(lo + r, hi, step)
            yield from [(i, i + r) for i in range(lo + r, hi - r, step)]
        else:
            yield (lo, lo + r)

    def sort(lo, hi):
        if hi - lo >= 1:
            mid = lo + (hi - lo) // 2
            yield from sort(lo, mid)
            yield from sort(mid + 1, hi)
            yield from merge(lo, hi, 1)

    return tuple(sort(0, n - 1))


_NET16 = _batcher_net(PEER_TOPK)


def _cmpx(v, x, i, j):
    a, b = v[i], v[j]
    if b is None:
        return
    if a is None:
        v[i], v[j] = b, None
        if x is not None:
            x[i], x[j] = x[j], x[i]
        return
    v[i], v[j] = jnp.maximum(a, b), jnp.minimum(a, b)
    if x is not None:
        first = a >= b
        x[i], x[j] = jnp.where(first, x[i], x[j]), jnp.where(first, x[j], x[i])


def _top16_sorted(groups, payload=None):
    pad = PEER_TOPK - len(groups)
    v = list(groups) + [None] * pad
    x = None if payload is None else list(payload) + [None] * pad
    for i, j in _NET16:
        _cmpx(v, x, i, j)
    for shift in (4, 2, 1):
        roll = lambda y: None if y is None else pltpu.roll(y, shift, 0)
        b = [roll(y) for y in v]
        bx = None if x is None else [roll(y) for y in x]
        c, cx = [], (None if x is None else [])
        for i in range(PEER_TOPK):
            p, q = v[i], b[PEER_TOPK - 1 - i]
            if p is None or q is None:
                c.append(q if p is None else p)
                if x is not None:
                    cx.append(bx[PEER_TOPK - 1 - i] if p is None else x[i])
            else:
                c.append(jnp.maximum(p, q))
                if x is not None:
                    cx.append(jnp.where(p >= q, x[i], bx[PEER_TOPK - 1 - i]))
        for d in (8, 4, 2, 1):
            for i in range(PEER_TOPK):
                if not i & d:
                    _cmpx(c, cx, i, i + d)
        v, x = c, cx
    return (v, x) if payload is not None else v


def _colsum(x):
    x = x + pltpu.roll(x, 4, 0)
    x = x + pltpu.roll(x, 2, 0)
    return x + pltpu.roll(x, 1, 0)


def _picks_by_value(s1, s2):
    k = PEER_TOPK
    ng = N_KEYS // SUBLANES
    sub = lax.broadcasted_iota(I32, (SUBLANES, LANES), 0)
    subf = sub.astype(F32)
    one = lambda m: jnp.where(m, 1.0, 0.0)
    c1 = [s1[SUBLANES * v:SUBLANES * (v + 1)] for v in range(ng)]
    c2 = [s2[SUBLANES * v:SUBLANES * (v + 1)] for v in range(ng)]
    ids = [subf + float(SUBLANES * v) for v in range(ng)]
    t1, i1 = _top16_sorted(c1, ids)
    t2, i2 = _top16_sorted(c2, ids)

    def as_rows(t, base):
        out = t[base]
        for r in range(1, SUBLANES):
            out = jnp.where(sub == r, t[base + r], out)
        return out

    sc2_lo, sc2_hi, sc1_hi = as_rows(t2, 0), as_rows(t2, SUBLANES), as_rows(t1, SUBLANES)
    cand = [t1[0] + sc2_lo, t1[0] + sc2_hi] + [t1[r] + sc2_lo for r in range(1, SUBLANES)] + [sc1_hi + t2[0]]
    tau = _top16_sorted(cand)[k - 1]
    sel = [c >= tau for c in cand]
    n_r = [_colsum(one(sel[0]) + one(sel[1]))] + [_colsum(one(sel[1 + r])) for r in range(1, SUBLANES)]
    n_hi = one(sel[9])
    n_r += [_colsum(jnp.where(sub == q, n_hi, 0.0)) for q in range(SUBLANES)]
    cmax = t1[0] + t2[0]
    zsum = jnp.where(sel[0], jnp.exp(cand[0] - cmax), 0.0)
    for g in range(1, len(cand)):
        zsum = zsum + jnp.where(sel[g], jnp.exp(cand[g] - cmax), 0.0)
    scale = 0.5 / _colsum(zsum)
    pre = [jnp.zeros((SUBLANES, LANES), F32)]
    for r in range(k):
        pre.append(pre[r] + n_r[r])
    bad = pre[k] != float(k)
    for r in range(k - 1):
        bad = bad | (t1[r] == t1[r + 1]) | (t2[r] == t2[r + 1])
    n_in1, n_in2 = one(c1[0] >= t1[k - 1]), one(c2[0] >= t2[k - 1])
    for v in range(1, ng):
        n_in1, n_in2 = n_in1 + one(c1[v] >= t1[k - 1]), n_in2 + one(c2[v] >= t2[k - 1])
    bad = bad | (_colsum(n_in1) != float(k)) | (_colsum(n_in2) != float(k))
    outs = ([], [], [])
    for half in range(k // SUBLANES):
        pick = subf + float(SUBLANES * half)
        r1p = jnp.zeros((SUBLANES, LANES), F32)
        base = jnp.zeros((SUBLANES, LANES), F32)
        for r in range(1, k):
            started = pre[r] <= pick
            r1p = jnp.where(started, float(r), r1p)
            base = jnp.where(started, pre[r], base)
        r2p = pick - base
        a_id, b_id, v1, v2 = i1[0], i2[0], t1[0], t2[0]
        for r in range(1, k):
            is1, is2 = r1p == float(r), r2p == float(r)
            a_id, v1 = jnp.where(is1, i1[r], a_id), jnp.where(is1, t1[r], v1)
            b_id, v2 = jnp.where(is2, i2[r], b_id), jnp.where(is2, t2[r], v2)
        outs[0].append(a_id)
        outs[1].append(b_id)
        outs[2].append(jnp.exp((v1 + v2) - cmax) * scale)
    return tuple(jnp.concatenate(o, axis=0) for o in outs) + (one(bad),)


def _picks_by_index(s1, s2):
    k = PEER_TOPK
    r16 = lax.broadcasted_iota(I32, (k, LANES), 0)
    rowf = lax.broadcasted_iota(I32, (N_KEYS, LANES), 0).astype(F32)
    rank1, sc1 = _top16_rows(s1)
    rank2, sc2 = _top16_rows(s2)

    def index_of_rank(rank):
        out = jnp.zeros((k, LANES), F32)
        for r in range(k):
            out = jnp.where(r16 == r, jnp.sum(jnp.where(rank == r, rowf, 0.0), axis=0, keepdims=True), out)
        return out

    i1, i2 = index_of_rank(rank1), index_of_rank(rank2)
    rep = lambda row, n: jnp.broadcast_to(row, (n, LANES))
    cand = jnp.concatenate([sc1[0:1] + sc2] + [sc1[r:r + 1] + sc2[0:8] for r in range(1, 8)] + [sc1[8:16] + sc2[0:1]],
                           axis=0)
    a_c = jnp.concatenate([rep(i1[0:1], k)] + [rep(i1[r:r + 1], 8) for r in range(1, 8)] + [i1[8:16]], axis=0)
    b_c = jnp.concatenate([i2] + [i2[0:8]] * 7 + [rep(i2[0:1], 8)], axis=0)
    rankc, _ = _top16_rows(cand)
    z = jnp.sum(jnp.where(rankc < k, jnp.exp(cand - cand[0:1]), 0.0), axis=0, keepdims=True)
    a_id = jnp.zeros((k, LANES), F32)
    b_id = jnp.zeros((k, LANES), F32)
    val = jnp.zeros((k, LANES), F32)
    for p in range(k):
        m = rankc == p
        pick = lambda c: jnp.sum(jnp.where(m, c, 0.0), axis=0, keepdims=True)
        a_id = jnp.where(r16 == p, pick(a_c), a_id)
        b_id = jnp.where(r16 == p, pick(b_c), b_id)
        val = jnp.where(r16 == p, pick(cand), val)
    return a_id, b_id, jnp.exp(val - cand[0:1]) * (0.5 / z)


def _peer_gates_body(strips, s1_ref, s2_ref, a_ref, b_ref, g_ref, at_ref, bt_ref, gt_ref):
    k = PEER_TOPK
    for s in range(strips):
        ls = slice(s * LANES, (s + 1) * LANES)

        def head(h, _):
            s1 = s1_ref[h, :, ls]
            s2 = s2_ref[h, :, ls]
            rows = pl.ds(pl.multiple_of(h * k, k), k)
            a_id, b_id, gate, bad = _picks_by_value(s1, s2)
            at_ref[rows, :] = a_id
            bt_ref[rows, :] = b_id
            gt_ref[rows, :] = gate

            @pl.when(jnp.max(bad) > 0.0)
            def _():
                at_ref[rows, :], bt_ref[rows, :], gt_ref[rows, :] = _picks_by_index(s1, s2)

            return 0

        lax.fori_loop(0, PEER_HEADS, head, 0)
        a_ref[ls, :] = at_ref[...].T
        b_ref[ls, :] = bt_ref[...].T
        g_ref[ls, :] = gt_ref[...].T


def _peer_gates(s1t, s2t):
    t = s1t.shape[2]
    nslot = PEER_HEADS * PEER_TOPK
    strips = 2 if t % (2 * LANES) == 0 else 1
    tb = strips * LANES
    blk = pl.BlockSpec((PEER_HEADS, N_KEYS, tb), lambda i: (0, 0, i))
    oblk = pl.BlockSpec((tb, nslot), lambda i: (i, 0))
    shp = jax.ShapeDtypeStruct((t, nslot), F32)
    return pl.pallas_call(
        functools.partial(_peer_gates_body, strips),
        grid=(t // tb,),
        in_specs=[blk, blk],
        out_specs=[oblk, oblk, oblk],
        out_shape=[shp, shp, shp],
        scratch_shapes=[pltpu.VMEM((nslot, LANES), F32)] * 3,
        compiler_params=_cparams("parallel"),
        name="peer_gates",
    )(s1t, s2t)


_SQRT_HALF = 0.7071067811865476
_GPAD = 8


def _peer_experts_body(tm, eb, nj, final, hn_ref, u_ref, v_ref, a_ref, b_ref, g_ref, x_ref, nf_ref,
                       o_ref, acc_ref, gw_ref):
    j = pl.program_id(1)
    pitch = tm + _GPAD
    half_a = N_KEYS // 2
    tiles = eb // N_KEYS
    steps_per_half = half_a // tiles

    @pl.when(j == 0)
    def _():
        acc_ref[...] = jnp.zeros(acc_ref.shape, F32)

    @pl.when(lax.rem(j, steps_per_half) == 0)
    def _():
        key = (lax.broadcasted_iota(I32, (half_a, PEER_HEADS * PEER_TOPK), 0) + j * tiles).astype(F32)
        key2 = lax.broadcasted_iota(I32, (N_KEYS, PEER_HEADS * PEER_TOPK), 0).astype(F32)

        def group(i, _):
            rows = pl.ds(pl.multiple_of(i * SUBLANES, SUBLANES), SUBLANES)
            a8, b8, g8 = a_ref[rows, :], b_ref[rows, :], g_ref[rows, :]
            for q in range(SUBLANES):
                pt = jnp.where(key == a8[q:q + 1], g8[q:q + 1], 0.0).astype(BF16)
                qt = jnp.where(key2 == b8[q:q + 1], 1.0, 0.0).astype(BF16)
                gw_ref[pl.ds(i * SUBLANES + q, half_a, stride=pitch), :] = _dot_nt(pt, qt)
            return 0

        lax.fori_loop(0, tm // SUBLANES, group, 0, unroll=4)

    a_lo = lax.rem(j * tiles, half_a)
    act = _dot_nt(hn_ref[...], u_ref[...])
    ws = []
    for al in range(tiles):
        gate = gw_ref[pl.ds(pl.multiple_of((a_lo + al) * pitch, SUBLANES), tm), :]
        x = act[:, al * N_KEYS:(al + 1) * N_KEYS]
        ws.append((gate * (x * (1.0 + lax.erf(x * _SQRT_HALF)))).astype(BF16))
    acc_ref[...] += _dot(jnp.concatenate(ws, axis=1), v_ref[...])

    @pl.when(j == nj - 1)
    def _():
        x2 = x_ref[...] + acc_ref[...]
        if final:
            o_ref[...] = _rms(x2, nf_ref[...])
        else:
            o_ref[...] = x2


def _peer_experts(hn, picks, x1, lw, norm_final, final):
    t = x1.shape[0]
    tm = _row_block(t, 512)
    eb = 16 * N_KEYS
    nj = N_EXPERTS // eb
    nslot = PEER_HEADS * PEER_TOPK
    a_id, b_id, gate = picks
    row = lambda n: pl.BlockSpec((tm, n), lambda i, j: (i, 0))
    return pl.pallas_call(
        functools.partial(_peer_experts_body, tm, eb, nj, final),
        grid=(t // tm, nj),
        in_specs=[row(D_MODEL),
                  pl.BlockSpec((eb, D_MODEL), lambda i, j: (j, 0)),
                  pl.BlockSpec((eb, D_MODEL), lambda i, j: (j, 0)),
                  row(nslot), row(nslot), row(nslot), row(D_MODEL),
                  pl.BlockSpec((1, D_MODEL), lambda i, j: (0, 0))],
        out_specs=row(D_MODEL),
        out_shape=jax.ShapeDtypeStruct((t, D_MODEL), F32),
        scratch_shapes=[pltpu.VMEM((tm, D_MODEL), F32),
                        pltpu.VMEM((N_KEYS // 2 * (tm + _GPAD), N_KEYS), F32)],
        compiler_params=_cparams("parallel", "arbitrary"),
        name="peer_experts",
    )(hn, lw["peer_u"], lw["peer_v"], a_id, b_id, gate, x1, norm_final)


def _peer(x1, lw, norm_final, final):
    hn, s1t, s2t = _peer_scores(x1, lw)
    picks = _peer_gates(s1t, s2t)
    return _peer_experts(hn, picks, x1, lw, norm_final, final)


def _pad_lanes(v, start, total=LANES):
    return jnp.zeros((1, total), F32).at[0, start:start + v.shape[0]].set(v)


def _layer_weights(l, norm_mix, w_in, b_gate, conv_w, conv_b, dt_bias, a_log, d_skip, ssm_norm, w_ssm_proj,
                   q_norm, w_qb, kv_norm, w_kvb, w_mla_proj, w_out, norm_ffn, peer_wq, peer_keys, peer_u, peer_v):
    o_dt = 3 * D_MODEL + CONV_DIM
    o_cq = o_dt + SSM_HEADS
    o_ckv = o_cq + Q_LORA
    o_kr = o_ckv + KV_LORA
    w = w_in[l]
    half = QK_ROPE // 2
    zcols = lambda n: jnp.zeros((D_MODEL, n), F32)
    kr = w[:, o_kr:o_kr + QK_ROPE]
    kr_sw = jnp.concatenate([kr[:, half:], kr[:, :half]], axis=1)
    sm1 = jnp.concatenate([kr, zcols(LANES - QK_ROPE)], axis=1)
    sm2 = jnp.concatenate([kr_sw, zcols(DT_LANE - QK_ROPE), w[:, o_dt:o_cq], zcols(LANES - DT_LANE - SSM_HEADS)], axis=1)
    wcat = jnp.concatenate([w[:, :o_dt], w[:, o_cq:o_kr], sm1, sm2], axis=1).astype(BF16)

    wq = w_qb[l].reshape(Q_LORA, MLA_HEADS, QK_NOPE + QK_ROPE)
    wq_nope = wq[:, :, :QK_NOPE].reshape(Q_LORA, MLA_HEADS * QK_NOPE)
    pe = wq[:, :, QK_NOPE:]
    pe_sw = jnp.concatenate([pe[:, :, half:], pe[:, :, :half]], axis=2)
    padpe = lambda p: jnp.pad(p, ((0, 0), (0, 0), (0, LANES - QK_ROPE))).reshape(Q_LORA, MLA_HEADS * LANES)
    wkv = w_kvb[l].reshape(KV_LORA, MLA_HEADS, QK_NOPE + V_HEAD)
    w_kb_t = jnp.transpose(wkv[:, :, :QK_NOPE], (1, 2, 0))
    w_vb = jnp.transpose(wkv[:, :, QK_NOPE:], (1, 0, 2))
    w_vb_all = wkv[:, :, QK_NOPE:].reshape(KV_LORA, MLA_HEADS * V_HEAD)
    row = lambda v: v.reshape(1, -1)
    return dict(
        norm_mix=row(norm_mix[l]), wcat=wcat, b_gate=row(b_gate[l]),
        conv_w=conv_w[l], conv_b=row(conv_b[l]),
        dt_bias_p=_pad_lanes(dt_bias[l], DT_LANE), a_log_p=_pad_lanes(a_log[l], DT_LANE),
        d_skip_e=row(jnp.repeat(d_skip[l], SSM_HEAD_DIM)), d_skip_row=row(d_skip[l]),
        ssm_norm=row(ssm_norm[l]), ssm_norm_t=ssm_norm[l].reshape(SSM_HEADS, SSM_HEAD_DIM).T,
        w_ssm_proj=w_ssm_proj[l].astype(BF16),
        q_norm=row(q_norm[l]), kv_norm=row(kv_norm[l]),
        wq_nope=wq_nope.astype(BF16), wq_pe=padpe(pe).astype(BF16), wq_pe_sw=padpe(pe_sw).astype(BF16),
        w_kb_t=w_kb_t.astype(BF16), w_vb=w_vb.astype(BF16), w_vb_all=w_vb_all.astype(BF16),
        w_mla_proj=w_mla_proj[l].astype(BF16), w_out=w_out[l].astype(BF16),
        norm_ffn=row(norm_ffn[l]), peer_wq=peer_wq[l].astype(BF16), peer_keys=peer_keys[l].astype(BF16),
        peer_u=peer_u[l].astype(BF16), peer_v=peer_v[l].astype(BF16))


def _rope_tables(pos):
    half = QK_ROPE // 2
    inv = ROPE_THETA ** (-jnp.arange(half, dtype=F32) / half)
    ang = pos.astype(F32)[:, None] * inv[None, :]
    cos, sin = jnp.cos(ang), jnp.sin(ang)
    pad = jnp.zeros((pos.shape[0], LANES - QK_ROPE), F32)
    c128 = jnp.concatenate([cos, cos, pad], axis=1)
    s128 = jnp.concatenate([-sin, sin, pad], axis=1)
    ktab = jnp.concatenate([c128, s128], axis=1)
    qtab = jnp.concatenate([jnp.tile(c128, (1, MLA_HEADS)), jnp.tile(s128, (1, MLA_HEADS))], axis=1)
    return qtab, ktab


def _prompt_layer(x2d, nb, seq, lw, tabs, norm_final, final):
    gates, z, xbc, cq, ckv, sm1, sm2 = _in_proj(x2d, lw["norm_mix"], lw["wcat"])
    yn, h_new = _ssd_prompt(xbc, sm2, z, lw, nb, seq)
    qcat, kcat, lat, kpe = _mla_prep(cq, ckv, sm1, sm2, tabs[0], tabs[1], lw, nb, seq)
    o = _attn_prompt(qcat, kcat, lw["w_vb"], nb, seq)
    x1 = _mix(x2d, gates, yn, o, lw)
    x2 = _peer(x1, lw, norm_final, final)
    conv_new = xbc.reshape(nb, seq, CONV_DIM)[:, seq - (CONV_WIDTH - 1):, :]
    return x2, h_new, conv_new, lat.reshape(nb, seq, KV_LORA), kpe.reshape(nb, seq, QK_ROPE)


def _sample_layer(x2d, layer, lw, tabs, state_ssm, state_conv, cache_latent, cache_krope, page_table,
                  norm_final, final):
    nb = x2d.shape[0]
    gates, z, xbc, cq, ckv, sm1, sm2 = _in_proj(x2d, lw["norm_mix"], lw["wcat"])
    prev = state_conv[layer]
    u, dt = _ssd_step_pre(xbc, jnp.transpose(prev, (1, 0, 2)), sm2, lw)
    to_t = lambda v: jnp.transpose(v.reshape(nb, SSM_HEADS, SSM_HEAD_DIM), (0, 2, 1))
    n_bc = SSM_GROUPS * D_STATE
    h_new, yt = _ssd_step(state_ssm, layer, to_t(u[:, :D_INNER]), to_t(z),
                          u[:, D_INNER:D_INNER + n_bc].reshape(nb, 1, n_bc),
                          u[:, D_INNER + n_bc:].reshape(nb, 1, n_bc), dt.reshape(nb, 1, LANES), lw)
    yn = jnp.transpose(yt, (0, 2, 1)).reshape(nb, D_INNER).astype(BF16)
    conv_new = jnp.concatenate([prev[:, 1:], xbc[:, None, :]], axis=1)

    qcat, kcat, lat, kpe = _mla_prep(cq, ckv, sm1, sm2, tabs[0], tabs[1], lw, 1, nb)
    qdec = jnp.transpose(qcat[0], (1, 0, 2))
    o = _attn_decode(page_table, qdec, kcat.reshape(nb, 1, QK_CAT), lw["w_vb_all"], cache_latent, cache_krope, layer)
    x1 = _mix(x2d, gates, yn, o.reshape(nb, MLA_HEADS * V_HEAD), lw)
    pad = (-nb) % LANES
    x2 = _peer(jnp.pad(x1, ((0, pad), (0, 0))), lw, norm_final, final)[:nb]
    return x2, h_new, conv_new, lat.reshape(nb, 1, KV_LORA), kpe.reshape(nb, 1, QK_ROPE)


def kernel(x_prompt, x_sample, cache_latent, cache_krope, state_ssm, state_conv, page_table, norm_mix, w_in, b_gate,
           conv_w, conv_b, dt_bias, a_log, d_skip, ssm_norm, w_ssm_proj, q_norm, w_qb, kv_norm, w_kvb, w_mla_proj,
           w_out, norm_ffn, peer_wq, peer_keys, peer_u, peer_v, norm_final):
    bp, sp = x_prompt.shape[:2]
    bs, ss = x_sample.shape[:2]
    depth = w_in.shape[0]
    assert ss == 1 and sp % SSD_CHUNK == 0
    past_len = page_table.shape[1] * cache_latent.shape[2]
    tabs_p = _rope_tables(jnp.arange(sp, dtype=I32))
    tabs_s = _rope_tables(jnp.full((bs,), past_len, I32))
    nf = norm_final.reshape(1, D_MODEL)
    cache_krope = jnp.swapaxes(cache_krope, 2, 3).astype(BF16)
    xp = x_prompt.reshape(bp * sp, D_MODEL)
    xs = x_sample.reshape(bs, D_MODEL)
    outs_p, outs_s = [], []
    for l in range(depth):
        lw = _layer_weights(l, norm_mix, w_in, b_gate, conv_w, conv_b, dt_bias, a_log, d_skip, ssm_norm, w_ssm_proj,
                            q_norm, w_qb, kv_norm, w_kvb, w_mla_proj, w_out, norm_ffn, peer_wq, peer_keys, peer_u,
                            peer_v)
        final = l == depth - 1
        xp, hp, cp, lp, kp = _prompt_layer(xp, bp, sp, lw, tabs_p, nf, final)
        xs, hs, cs, lsn, ksn = _sample_layer(xs, l, lw, tabs_s, state_ssm, state_conv, cache_latent, cache_krope,
                                             page_table, nf, final)
        outs_p.append((lp, kp, hp, cp))
        outs_s.append((lsn, ksn, hs, cs))
    stack = lambda outs, i: jnp.stack([o[i] for o in outs])
    return (xp.reshape(bp, sp, D_MODEL), xs.reshape(bs, ss, D_MODEL),
            stack(outs_p, 0), stack(outs_p, 1), stack(outs_p, 2), stack(outs_p, 3),
            stack(outs_s, 0), stack(outs_s, 1), stack(outs_s, 2), stack(outs_s, 3))
```

```python
import functools

import numpy as np
import jax
import jax.numpy as jnp
from jax import lax
from jax.experimental import pallas as pl
from jax.experimental.pallas import tpu as pltpu

F32 = jnp.float32
BF16 = jnp.bfloat16
I32 = jnp.int32

D_MODEL = 1024
D_INNER = 1024
SSM_HEAD_DIM = 64
SSM_HEADS = 16
SSM_GROUPS = 2
SSM_HPG = 8
D_STATE = 128
CONV_WIDTH = 4
CONV_DIM = D_INNER + 2 * SSM_GROUPS * D_STATE
SSD_CHUNK = 128
MLA_HEADS = 8
QK_NOPE = 64
QK_ROPE = 32
V_HEAD = 64
Q_LORA = 384
KV_LORA = 256
ROPE_THETA = 10000.0
ATTN_SCALE = (QK_NOPE + QK_ROPE) ** -0.5
PEER_HEADS = 8
N_KEYS = 128
N_EXPERTS = N_KEYS * N_KEYS
PEER_D_KEY = 256
PEER_TOPK = 16
EPS = 1e-6

LANES = 128
QK_CAT = KV_LORA + LANES
DT_LANE = 64
VMEM_LIMIT = 56 * 1024 * 1024

_HI = lax.Precision.HIGHEST


def _cparams(*sem):
    return pltpu.CompilerParams(dimension_semantics=sem, vmem_limit_bytes=VMEM_LIMIT)


def _dot(a, b):
    return jnp.dot(a, b, preferred_element_type=F32)


def _dot_nt(a, b):
    return lax.dot_general(a, b, (((1,), (1,)), ((), ())), preferred_element_type=F32)


def _dot_tn(a, b):
    return lax.dot_general(a, b, (((0,), (0,)), ((), ())), preferred_element_type=F32)


def _rms(x, w):
    ms = jnp.mean(x * x, axis=-1, keepdims=True)
    return (x * lax.rsqrt(ms + EPS)) * w


def _sigmoid(x):
    return jax.nn.sigmoid(x)


def _softplus(x):
    return jnp.maximum(x, 0.0) + jnp.log1p(jnp.exp(-jnp.abs(x)))


def _row_block(t, cap):
    if t <= cap:
        return t
    b = cap
    while t % b:
        b //= 2
    return b


_IN_WIDTHS = (2 * D_MODEL, D_INNER, CONV_DIM, Q_LORA, KV_LORA, LANES, LANES)


def _in_proj_body(x_ref, nw_ref, w_ref, *out_refs):
    h = _rms(x_ref[...], nw_ref[...]).astype(BF16)
    off = 0
    for ref, wd in zip(out_refs, _IN_WIDTHS):
        ref[...] = _dot(h, w_ref[:, off:off + wd])
        off += wd


def _in_proj(x2d, nw, wcat):
    t = x2d.shape[0]
    tm = _row_block(t, 256)
    wtot = sum(_IN_WIDTHS)
    return pl.pallas_call(
        _in_proj_body,
        grid=(t // tm,),
        in_specs=[pl.BlockSpec((tm, D_MODEL), lambda i: (i, 0)),
                  pl.BlockSpec((1, D_MODEL), lambda i: (0, 0)),
                  pl.BlockSpec((D_MODEL, wtot), lambda i: (0, 0))],
        out_specs=[pl.BlockSpec((tm, wd), lambda i: (i, 0)) for wd in _IN_WIDTHS],
        out_shape=[jax.ShapeDtypeStruct((t, wd), F32) for wd in _IN_WIDTHS],
        compiler_params=_cparams("parallel"),
        name="in_proj",
    )(x2d, nw, wcat)


def _head_expand():
    r = lax.broadcasted_iota(I32, (LANES, D_INNER), 0)
    c = lax.broadcasted_iota(I32, (LANES, D_INNER), 1)
    return (jnp.right_shift(c, 6) == (r - DT_LANE)).astype(F32)


def _ssd_prompt_body(nc, xbc_ref, sm_ref, z_ref, cw_ref, cb_ref, dtb_ref, alog_ref, dsk_ref, nrm_ref,
                     yn_ref, hfin_ref, xpad_ref, st_ref):
    c = pl.program_id(1)
    cl = SSD_CHUNK

    @pl.when(c == 0)
    def _():
        xpad_ref[0:8, :] = jnp.zeros((8, CONV_DIM), F32)
        st_ref[...] = jnp.zeros_like(st_ref)

    xpad_ref[8:8 + cl, :] = xbc_ref[...]
    conv = cb_ref[...]
    for k in range(CONV_WIDTH):
        conv = conv + xpad_ref[5 + k:5 + k + cl, :] * cw_ref[k:k + 1, :]
    xpad_ref[0:8, :] = xpad_ref[cl:cl + 8, :]
    u = conv * _sigmoid(conv)
    xs = u[:, :D_INNER]
    bb = u[:, D_INNER:D_INNER + SSM_GROUPS * D_STATE].astype(BF16)
    cb = u[:, D_INNER + SSM_GROUPS * D_STATE:].astype(BF16)

    row = lax.broadcasted_iota(I32, (cl, cl), 0)
    col = lax.broadcasted_iota(I32, (cl, cl), 1)
    dt_ok = (col >= DT_LANE) & (col < DT_LANE + SSM_HEADS)
    dt = jnp.where(dt_ok, _softplus(sm_ref[...] + dtb_ref[...]), 0.0)
    a = dt * (-jnp.exp(alog_ref[...]))
    tri = (row >= col).astype(F32)
    a_cs = jnp.dot(tri, a, precision=_HI, preferred_element_type=F32)
    a_cs_t = a_cs.T
    ea = jnp.exp(a_cs)
    dte = jnp.exp(a_cs[cl - 1:cl, :] - a_cs)
    ex = _head_expand()
    dt_e = jnp.dot(dt, ex, precision=_HI, preferred_element_type=F32)
    ea_e = jnp.dot(ea, ex, precision=_HI, preferred_element_type=F32)
    dte_e = jnp.dot(dte, ex, precision=_HI, preferred_element_type=F32)

    xdt = xs * dt_e
    xdt_b = xdt.astype(BF16)
    causal = row >= col
    ys = []
    for g in range(SSM_GROUPS):
        cbm = _dot_nt(cb[:, g * D_STATE:(g + 1) * D_STATE], bb[:, g * D_STATE:(g + 1) * D_STATE])
        for e in range(SSM_HPG):
            h = g * SSM_HPG + e
            seg = a_cs[:, DT_LANE + h:DT_LANE + h + 1] - a_cs_t[DT_LANE + h:DT_LANE + h + 1, :]
            lm = jnp.exp(jnp.where(causal, seg, -jnp.inf))
            ys.append(_dot((cbm * lm).astype(BF16), xdt_b[:, h * SSM_HEAD_DIM:(h + 1) * SSM_HEAD_DIM]))
    y_diag = jnp.concatenate(ys, axis=1)

    xw = (xdt * dte_e).astype(BF16)
    dec = ea_e[cl - 1:cl, :]
    half = D_INNER // SSM_GROUPS
    yoffs = []
    for g in range(SSM_GROUPS):
        s_prev = st_ref[g]
        yoffs.append(_dot(cb[:, g * D_STATE:(g + 1) * D_STATE], s_prev.astype(BF16)))
        st_ref[g] = dec[:, g * half:(g + 1) * half] * s_prev + _dot_tn(
            bb[:, g * D_STATE:(g + 1) * D_STATE], xw[:, g * half:(g + 1) * half])
    y = y_diag + jnp.concatenate(yoffs, axis=1) * ea_e + dsk_ref[...] * xs
    zv = z_ref[...]
    y = y * (zv * _sigmoid(zv))
    outs = []
    for g in range(SSM_GROUPS):
        outs.append(_rms(y[:, g * half:(g + 1) * half], nrm_ref[:, g * half:(g + 1) * half]))
    yn_ref[...] = jnp.concatenate(outs, axis=1).astype(BF16)

    @pl.when(c == nc - 1)
    def _():
        for g in range(SSM_GROUPS):
            s_t = st_ref[g].T
            for e in range(SSM_HPG):
                hfin_ref[0, g, e] = s_t[e * SSM_HEAD_DIM:(e + 1) * SSM_HEAD_DIM, :]


def _ssd_prompt(xbc, sm2, z, lw, nb, seq):
    nc = seq // SSD_CHUNK
    cl = SSD_CHUNK
    rowmap = lambda b, c: (b * nc + c, 0)
    full = lambda b, c: (0, 0)
    return pl.pallas_call(
        functools.partial(_ssd_prompt_body, nc),
        grid=(nb, nc),
        in_specs=[pl.BlockSpec((cl, CONV_DIM), rowmap),
                  pl.BlockSpec((cl, LANES), rowmap),
                  pl.BlockSpec((cl, D_INNER), rowmap),
                  pl.BlockSpec((CONV_WIDTH, CONV_DIM), full),
                  pl.BlockSpec((1, CONV_DIM), full),
                  pl.BlockSpec((1, LANES), full),
                  pl.BlockSpec((1, LANES), full),
                  pl.BlockSpec((1, D_INNER), full),
                  pl.BlockSpec((1, D_INNER), full)],
        out_specs=[pl.BlockSpec((cl, D_INNER), rowmap),
                   pl.BlockSpec((1, SSM_GROUPS, SSM_HPG, SSM_HEAD_DIM, D_STATE), lambda b, c: (b, 0, 0, 0, 0))],
        out_shape=[jax.ShapeDtypeStruct((nb * seq, D_INNER), BF16),
                   jax.ShapeDtypeStruct((nb, SSM_GROUPS, SSM_HPG, SSM_HEAD_DIM, D_STATE), F32)],
        scratch_shapes=[pltpu.VMEM((cl + 8, CONV_DIM), F32),
                        pltpu.VMEM((SSM_GROUPS, D_STATE, D_INNER // SSM_GROUPS), F32)],
        compiler_params=_cparams("parallel", "arbitrary"),
        name="ssd_prompt",
    )(xbc, sm2, z, lw["conv_w"], lw["conv_b"], lw["dt_bias_p"], lw["a_log_p"], lw["d_skip_e"], lw["ssm_norm"])


def _ssd_step_pre_body(xbc_ref, prev_ref, sm_ref, cw_ref, cb_ref, dtb_ref, u_ref, dt_ref):
    conv = cb_ref[...]
    for k in range(CONV_WIDTH - 1):
        conv = conv + prev_ref[k] * cw_ref[k:k + 1, :]
    conv = conv + xbc_ref[...] * cw_ref[CONV_WIDTH - 1:CONV_WIDTH, :]
    u_ref[...] = conv * _sigmoid(conv)
    dt_ref[...] = _softplus(sm_ref[...] + dtb_ref[...])


def _ssd_step_pre(xbc, prev_t, sm2, lw):
    nb = xbc.shape[0]
    return pl.pallas_call(
        _ssd_step_pre_body,
        out_shape=[jax.ShapeDtypeStruct((nb, CONV_DIM), F32), jax.ShapeDtypeStruct((nb, LANES), F32)],
        compiler_params=pltpu.CompilerParams(vmem_limit_bytes=VMEM_LIMIT),
        name="ssd_step_pre",
    )(xbc, prev_t, sm2, lw["conv_w"], lw["conv_b"], lw["dt_bias_p"])


def _ssd_step_body(h0_ref, xt_ref, zt_ref, b_ref, c_ref, dt_ref, alog_ref, dsk_ref, nrm_ref, hn_ref, yt_ref):
    xt = xt_ref[0]
    dtv = dt_ref[0]
    av = -jnp.exp(alog_ref[...])
    lane = lax.broadcasted_iota(I32, (SSM_HEAD_DIM, SSM_HEADS), 1)
    y = jnp.zeros((SSM_HEAD_DIM, SSM_HEADS), F32)
    for g in range(SSM_GROUPS):
        bg = b_ref[0][:, g * D_STATE:(g + 1) * D_STATE]
        cg = c_ref[0][:, g * D_STATE:(g + 1) * D_STATE]
        for e in range(SSM_HPG):
            h = g * SSM_HPG + e
            dth = dtv[:, DT_LANE + h:DT_LANE + h + 1]
            dec = jnp.exp(dth * av[:, DT_LANE + h:DT_LANE + h + 1])
            hn = dec * h0_ref[0, g, e] + (xt[:, h:h + 1] * dth) * bg
            hn_ref[0, g, e] = hn
            ycol = jnp.sum(hn * cg, axis=1, keepdims=True)
            y = jnp.where(lane == h, ycol, y)
    y = y + dsk_ref[...] * xt
    zv = zt_ref[0]
    y = y * (zv * _sigmoid(zv))
    sq = y * y
    in_g0 = lane < SSM_HPG
    n_g = float(D_INNER // SSM_GROUPS)
    ms0 = jnp.sum(jnp.sum(jnp.where(in_g0, sq, 0.0), axis=1, keepdims=True), axis=0, keepdims=True) / n_g
    ms1 = jnp.sum(jnp.sum(jnp.where(in_g0, 0.0, sq), axis=1, keepdims=True), axis=0, keepdims=True) / n_g
    scale = jnp.where(in_g0, lax.rsqrt(ms0 + EPS), lax.rsqrt(ms1 + EPS))
    yt_ref[0] = (y * scale) * nrm_ref[...]


def _ssd_step(h0_all, layer, xt, zt, bm, cm, dt, lw):
    nb = xt.shape[0]
    st_block = (None, 1, SSM_GROUPS, SSM_HPG, SSM_HEAD_DIM, D_STATE)
    pt_block = (1, SSM_HEAD_DIM, SSM_HEADS)
    vec = lambda n: pl.BlockSpec((1, 1, n), lambda b: (b, 0, 0))
    full = lambda r, n: pl.BlockSpec((r, n), lambda b: (0, 0))
    return pl.pallas_call(
        _ssd_step_body,
        grid=(nb,),
        in_specs=[pl.BlockSpec(st_block, lambda b: (layer, b, 0, 0, 0, 0)),
                  pl.BlockSpec(pt_block, lambda b: (b, 0, 0)),
                  pl.BlockSpec(pt_block, lambda b: (b, 0, 0)),
                  vec(SSM_GROUPS * D_STATE), vec(SSM_GROUPS * D_STATE), vec(LANES),
                  full(1, LANES), full(1, SSM_HEADS), full(SSM_HEAD_DIM, SSM_HEADS)],
        out_specs=[pl.BlockSpec((1, SSM_GROUPS, SSM_HPG, SSM_HEAD_DIM, D_STATE), lambda b: (b, 0, 0, 0, 0)),
                   pl.BlockSpec(pt_block, lambda b: (b, 0, 0))],
        out_shape=[jax.ShapeDtypeStruct((nb, SSM_GROUPS, SSM_HPG, SSM_HEAD_DIM, D_STATE), F32),
                   jax.ShapeDtypeStruct((nb, SSM_HEAD_DIM, SSM_HEADS), F32)],
        compiler_params=_cparams("parallel"),
        name="ssd_step",
    )(h0_all, xt, zt, bm, cm, dt, lw["a_log_p"], lw["d_skip_row"], lw["ssm_norm_t"])


def _mla_prep_body(cq_ref, ckv_ref, sm1_ref, sm2_ref, qtab_ref, ktab_ref, qn_ref, kvn_ref,
                   wn_ref, wpe_ref, wpes_ref, wkb_ref, qcat_ref, kcat_ref, lat_ref, kpe_ref):
    hq = _rms(cq_ref[...], qn_ref[...]).astype(BF16)
    qn = _dot(hq, wn_ref[...]).astype(BF16)
    npe = MLA_HEADS * LANES
    qpe = (_dot(hq, wpe_ref[...]) * qtab_ref[:, :npe] + _dot(hq, wpes_ref[...]) * qtab_ref[:, npe:]) * ATTN_SCALE
    for h in range(MLA_HEADS):
        ql = _dot(qn[:, h * QK_NOPE:(h + 1) * QK_NOPE], wkb_ref[h]) * ATTN_SCALE
        qcat_ref[0, h] = jnp.concatenate([ql, qpe[:, h * LANES:(h + 1) * LANES]], axis=1).astype(BF16)
    lat = _rms(ckv_ref[...], kvn_ref[...])
    lat_ref[...] = lat
    kpe = sm1_ref[...] * ktab_ref[:, :LANES] + sm2_ref[...] * ktab_ref[:, LANES:]
    kpe_ref[...] = kpe[:, :QK_ROPE]
    kcat_ref[...] = jnp.concatenate([lat, kpe], axis=1).astype(BF16)


def _mla_prep(cq, ckv, sm1, sm2, qtab, ktab, lw, nb, seq):
    t = nb * seq
    tm = _row_block(seq, 512)
    nl = seq // tm
    rowmap = lambda b, i: (b * nl + i, 0)
    tabmap = lambda b, i: (i, 0)
    full2 = lambda b, i: (0, 0)
    npe = MLA_HEADS * LANES
    return pl.pallas_call(
        _mla_prep_body,
        grid=(nb, nl),
        in_specs=[pl.BlockSpec((tm, Q_LORA), rowmap),
                  pl.BlockSpec((tm, KV_LORA), rowmap),
                  pl.BlockSpec((tm, LANES), rowmap),
                  pl.BlockSpec((tm, LANES), rowmap),
                  pl.BlockSpec((tm, 2 * npe), tabmap),
                  pl.BlockSpec((tm, 2 * LANES), tabmap),
                  pl.BlockSpec((1, Q_LORA), full2),
                  pl.BlockSpec((1, KV_LORA), full2),
                  pl.BlockSpec((Q_LORA, MLA_HEADS * QK_NOPE), full2),
                  pl.BlockSpec((Q_LORA, npe), full2),
                  pl.BlockSpec((Q_LORA, npe), full2),
                  pl.BlockSpec((MLA_HEADS, QK_NOPE, KV_LORA), lambda b, i: (0, 0, 0))],
        out_specs=[pl.BlockSpec((1, MLA_HEADS, tm, QK_CAT), lambda b, i: (b, 0, i, 0)),
                   pl.BlockSpec((tm, QK_CAT), rowmap),
                   pl.BlockSpec((tm, KV_LORA), rowmap),
                   pl.BlockSpec((tm, QK_ROPE), rowmap)],
        out_shape=[jax.ShapeDtypeStruct((nb, MLA_HEADS, seq, QK_CAT), BF16),
                   jax.ShapeDtypeStruct((t, QK_CAT), BF16),
                   jax.ShapeDtypeStruct((t, KV_LORA), F32),
                   jax.ShapeDtypeStruct((t, QK_ROPE), F32)],
        compiler_params=_cparams("parallel", "parallel"),
        name="mla_prep",
    )(cq, ckv, sm1, sm2, qtab, ktab, lw["q_norm"], lw["kv_norm"], lw["wq_nope"], lw["wq_pe"], lw["wq_pe_sw"],
      lw["w_kb_t"])


def _attn_prompt_body(tq, tk, nk, q_ref, k_ref, wvb_ref, o_ref, m_ref, l_ref, acc_ref):
    i = pl.program_id(1)
    j = pl.program_id(2)

    @pl.when(j == 0)
    def _():
        m_ref[...] = jnp.full(m_ref.shape, -jnp.inf, F32)
        l_ref[...] = jnp.zeros(l_ref.shape, F32)
        acc_ref[...] = jnp.zeros(acc_ref.shape, F32)

    @pl.when(j * tk <= i * tq + tq - 1)
    def _():
        k = k_ref[...]
        v = k[:, :KV_LORA]
        qpos = i * tq + lax.broadcasted_iota(I32, (tq, tk), 0)
        kpos = j * tk + lax.broadcasted_iota(I32, (tq, tk), 1)
        visible = kpos <= qpos
        ahead = 2
        s_q = [_dot_nt(q_ref[0, h], k) for h in range(ahead)]
        for h in range(MLA_HEADS):
            hs = slice(h * tq, (h + 1) * tq)
            s = jnp.where(visible, s_q.pop(0), -jnp.inf)
            if h + ahead < MLA_HEADS:
                s_q.append(_dot_nt(q_ref[0, h + ahead], k))
            m_prev = m_ref[hs]
            m_new = jnp.maximum(m_prev, jnp.max(s, axis=1, keepdims=True))
            p = jnp.exp(s - m_new)
            alpha = jnp.exp(m_prev - m_new)
            l_ref[hs] = alpha * l_ref[hs] + jnp.sum(p, axis=1, keepdims=True)
            acc_ref[hs] = alpha * acc_ref[hs] + _dot(p.astype(BF16), v)
            m_ref[hs] = m_new

    @pl.when(j == nk - 1)
    def _():
        outs = []
        for h in range(MLA_HEADS):
            hs = slice(h * tq, (h + 1) * tq)
            outs.append(_dot((acc_ref[hs] / l_ref[hs]).astype(BF16), wvb_ref[h]))
        o_ref[...] = jnp.concatenate(outs, axis=1).astype(BF16)


def _attn_prompt(qcat, kcat, w_vb, nb, seq):
    tq = _row_block(seq, 256)
    tk = _row_block(seq, 512)
    nq, nk = seq // tq, seq // tk

    def kmap(b, i, j):
        return (b * nk + jnp.minimum(j, (i * tq + tq - 1) // tk), 0)

    return pl.pallas_call(
        functools.partial(_attn_prompt_body, tq, tk, nk),
        grid=(nb, nq, nk),
        in_specs=[pl.BlockSpec((1, MLA_HEADS, tq, QK_CAT), lambda b, i, j: (b, 0, i, 0)),
                  pl.BlockSpec((tk, QK_CAT), kmap),
                  pl.BlockSpec((MLA_HEADS, KV_LORA, V_HEAD), lambda b, i, j: (0, 0, 0))],
        out_specs=pl.BlockSpec((tq, MLA_HEADS * V_HEAD), lambda b, i, j: (b * nq + i, 0)),
        out_shape=jax.ShapeDtypeStruct((nb * seq, MLA_HEADS * V_HEAD), BF16),
        scratch_shapes=[pltpu.VMEM((MLA_HEADS * tq, 1), F32), pltpu.VMEM((MLA_HEADS * tq, 1), F32),
                        pltpu.VMEM((MLA_HEADS * tq, KV_LORA), F32)],
        compiler_params=_cparams("parallel", "parallel", "arbitrary"),
        name="attn_prompt",
    )(qcat, kcat, w_vb)


def _attn_decode_body(pg, nj, pt_ref, q_ref, kc_ref, wvb_ref, *refs):
    lat_refs = refs[:pg]
    kr_refs = refs[pg:2 * pg]
    o_ref, m_ref, l_ref, acc_ref = refs[2 * pg:]
    j = pl.program_id(1)

    @pl.when(j == 0)
    def _():
        m_ref[...] = jnp.full(m_ref.shape, -jnp.inf, F32)
        l_ref[...] = jnp.zeros(l_ref.shape, F32)
        acc_ref[...] = jnp.zeros(acc_ref.shape, F32)

    q = q_ref[0]
    ql = q[:, :KV_LORA]
    qr = q[:, KV_LORA:KV_LORA + QK_ROPE]
    kvs = [lat_refs[i][...].astype(BF16) for i in range(pg)]
    ss = [_dot_nt(ql, kvs[i]) + _dot(qr, kr_refs[i][...]) for i in range(pg)]
    m_prev = m_ref[...]
    m_blk = ss[0]
    for i in range(1, pg):
        m_blk = jnp.maximum(m_blk, ss[i])
    m_new = jnp.maximum(m_prev, jnp.max(m_blk, axis=1, keepdims=True))
    ps = [jnp.exp(ss[i] - m_new) for i in range(pg)]
    p_sum = ps[0]
    for i in range(1, pg):
        p_sum = p_sum + ps[i]
    pvs = [_dot(ps[i].astype(BF16), kvs[i]) for i in range(pg)]
    while len(pvs) > 1:
        pvs = [pvs[i] + pvs[i + 1] for i in range(0, len(pvs), 2)]
    alpha = jnp.exp(m_prev - m_new)
    l_ref[...] = alpha * l_ref[...] + jnp.sum(p_sum, axis=1, keepdims=True)
    acc_ref[...] = alpha * acc_ref[...] + pvs[0]
    m_ref[...] = m_new

    @pl.when(j == nj - 1)
    def _():
        kc = kc_ref[0].astype(F32)
        s = jnp.sum(q.astype(F32) * kc, axis=1, keepdims=True)
        m_prev = m_ref[...]
        m_new = jnp.maximum(m_prev, s)
        p = jnp.exp(s - m_new)
        alpha = jnp.exp(m_prev - m_new)
        l_fin = alpha * l_ref[...] + p
        acc = alpha * acc_ref[...] + p.astype(BF16).astype(F32) * kc[:, :KV_LORA]
        o = (acc / l_fin).astype(BF16)
        full = _dot(o, wvb_ref[...])
        hrow = lax.broadcasted_iota(I32, full.shape, 0)
        hcol = jnp.right_shift(lax.broadcasted_iota(I32, full.shape, 1), 6)
        o_ref[0] = jnp.sum(jnp.where(hrow == hcol, full, 0.0), axis=0, keepdims=True).astype(BF16)


def _attn_decode(page_table, qdec, kcur, w_vb_all, cache_latent, cache_krope, layer):
    nb, n_pages = page_table.shape
    page = cache_latent.shape[2]
    pg = 32
    while n_pages % pg:
        pg //= 2
    nj = n_pages // pg

    def pmap(i):
        return lambda b, j, pt: (layer, pt[b, j * pg + i], 0, 0)

    in_specs = [pl.BlockSpec((1, MLA_HEADS, QK_CAT), lambda b, j, pt: (b, 0, 0)),
                pl.BlockSpec((1, 1, QK_CAT), lambda b, j, pt: (b, 0, 0)),
                pl.BlockSpec((KV_LORA, MLA_HEADS * V_HEAD), lambda b, j, pt: (0, 0))]
    in_specs += [pl.BlockSpec((None, None, page, KV_LORA), pmap(i)) for i in range(pg)]
    in_specs += [pl.BlockSpec((None, None, QK_ROPE, page), pmap(i)) for i in range(pg)]
    grid_spec = pltpu.PrefetchScalarGridSpec(
        num_scalar_prefetch=1,
        grid=(nb, nj),
        in_specs=in_specs,
        out_specs=pl.BlockSpec((1, 1, MLA_HEADS * V_HEAD), lambda b, j, pt: (b, 0, 0)),
        scratch_shapes=[pltpu.VMEM((MLA_HEADS, 1), F32), pltpu.VMEM((MLA_HEADS, 1), F32),
                        pltpu.VMEM((MLA_HEADS, KV_LORA), F32)])
    return pl.pallas_call(
        functools.partial(_attn_decode_body, pg, nj),
        grid_spec=grid_spec,
        out_shape=jax.ShapeDtypeStruct((nb, 1, MLA_HEADS * V_HEAD), BF16),
        compiler_params=_cparams("parallel", "arbitrary"),
        name="attn_decode",
    )(page_table, qdec, kcur, w_vb_all, *([cache_latent] * pg), *([cache_krope] * pg))


def _mix_body(x_ref, g_ref, bg_ref, yn_ref, o_ref, wssm_ref, wmla_ref, wout_ref, x1_ref):
    ya = _dot(yn_ref[...], wssm_ref[...])
    yb = _dot(o_ref[...], wmla_ref[...])
    gate = _sigmoid(g_ref[...] + bg_ref[...])
    mix = gate[:, :D_MODEL] * ya + gate[:, D_MODEL:] * yb
    x1_ref[...] = x_ref[...] + _dot(mix.astype(BF16), wout_ref[...])


def _mix(x2d, gates, yn, o, lw):
    t = x2d.shape[0]
    tm = _row_block(t, 512)
    row = lambda n: pl.BlockSpec((tm, n), lambda i: (i, 0))
    full = lambda r, n: pl.BlockSpec((r, n), lambda i: (0, 0))
    return pl.pallas_call(
        _mix_body,
        grid=(t // tm,),
        in_specs=[row(D_MODEL), row(2 * D_MODEL), full(1, 2 * D_MODEL), row(D_INNER), row(MLA_HEADS * V_HEAD),
                  full(D_INNER, D_MODEL), full(MLA_HEADS * V_HEAD, D_MODEL), full(D_MODEL, D_MODEL)],
        out_specs=row(D_MODEL),
        out_shape=jax.ShapeDtypeStruct((t, D_MODEL), F32),
        compiler_params=_cparams("parallel"),
        name="mix",
    )(x2d, gates, lw["b_gate"], yn, o, lw["w_ssm_proj"], lw["w_mla_proj"], lw["w_out"])


def _peer_scores_body(x_ref, nw_ref, wq_ref, keys_ref, hn_ref, s1_ref, s2_ref):
    hn = _rms(x_ref[...], nw_ref[...]).astype(BF16)
    hn_ref[...] = hn
    q = _dot(hn, wq_ref[...]).astype(BF16)
    half = PEER_D_KEY // 2
    for h in range(PEER_HEADS):
        s1_ref[h] = _dot_nt(keys_ref[h, 0], q[:, h * PEER_D_KEY:h * PEER_D_KEY + half])
        s2_ref[h] = _dot_nt(keys_ref[h, 1], q[:, h * PEER_D_KEY + half:(h + 1) * PEER_D_KEY])


def _peer_scores(x1, lw):
    t = x1.shape[0]
    tm = _row_block(t, 512)
    sblock = pl.BlockSpec((PEER_HEADS, N_KEYS, tm), lambda i: (0, 0, i))
    return pl.pallas_call(
        _peer_scores_body,
        grid=(t // tm,),
        in_specs=[pl.BlockSpec((tm, D_MODEL), lambda i: (i, 0)),
                  pl.BlockSpec((1, D_MODEL), lambda i: (0, 0)),
                  pl.BlockSpec((D_MODEL, PEER_HEADS * PEER_D_KEY), lambda i: (0, 0)),
                  pl.BlockSpec((PEER_HEADS, 2, N_KEYS, PEER_D_KEY // 2), lambda i: (0, 0, 0, 0))],
        out_specs=[pl.BlockSpec((tm, D_MODEL), lambda i: (i, 0)), sblock, sblock],
        out_shape=[jax.ShapeDtypeStruct((t, D_MODEL), BF16),
                   jax.ShapeDtypeStruct((PEER_HEADS, N_KEYS, t), F32),
                   jax.ShapeDtypeStruct((PEER_HEADS, N_KEYS, t), F32)],
        compiler_params=_cparams("parallel"),
        name="peer_scores",
    )(x1, lw["norm_ffn"], lw["peer_wq"], lw["peer_keys"])


SUBLANES = 8


def _top16_rows(s):
    n = s.shape[0]
    row = lax.broadcasted_iota(I32, s.shape, 0)
    r16 = lax.broadcasted_iota(I32, (PEER_TOPK, LANES), 0)

    def body(r, carry):
        rank, vals = carry
        alive = rank == PEER_TOPK
        m = jnp.max(jnp.where(alive, s, -jnp.inf), axis=0, keepdims=True)
        idx = jnp.min(jnp.where(alive & (s == m), row, n), axis=0, keepdims=True)
        rank = jnp.where(row == idx, r, rank)
        vals = jnp.where(r16 == r, m, vals)
        return rank, vals

    return lax.fori_loop(0, PEER_TOPK, body,
                         (jnp.full(s.shape, PEER_TOPK, I32), jnp.zeros((PEER_TOPK, LANES), F32)))


def _batcher_net(n):
    def merge(lo, hi, r):
        step = r * 2
        if step < hi - lo:
            yield from merge(lo, hi, step)
            yield from merge(lo + r, hi, step)
            yield from [(i, i + r) for i in range(lo + r, hi - r, step)]
        else:
            yield (lo, lo + r)

    def sort(lo, hi):
        if hi - lo >= 1:
            mid = lo + (hi - lo) // 2
            yield from sort(lo, mid)
            yield from sort(mid + 1, hi)
            yield from merge(lo, hi, 1)

    return tuple(sort(0, n - 1))


_NET16 = _batcher_net(PEER_TOPK)


def _cmpx(v, x, i, j):
    a, b = v[i], v[j]
    if b is None:
        return
    if a is None:
        v[i], v[j] = b, None
        if x is not None:
            x[i], x[j] = x[j], x[i]
        return
    v[i], v[j] = jnp.maximum(a, b), jnp.minimum(a, b)
    if x is not None:
        first = a >= b
        x[i], x[j] = jnp.where(first, x[i], x[j]), jnp.where(first, x[j], x[i])


def _top16_sorted(groups, payload=None):
    pad = PEER_TOPK - len(groups)
    v = list(groups) + [None] * pad
    x = None if payload is None else list(payload) + [None] * pad
    for i, j in _NET16:
        _cmpx(v, x, i, j)
    for shift in (4, 2, 1):
        roll = lambda y: None if y is None else pltpu.roll(y, shift, 0)
        b = [roll(y) for y in v]
        bx = None if x is None else [roll(y) for y in x]
        c, cx = [], (None if x is None else [])
        for i in range(PEER_TOPK):
            p, q = v[i], b[PEER_TOPK - 1 - i]
            if p is None or q is None:
                c.append(q if p is None else p)
                if x is not None:
                    cx.append(bx[PEER_TOPK - 1 - i] if p is None else x[i])
            else:
                c.append(jnp.maximum(p, q))
                if x is not None:
                    cx.append(jnp.where(p >= q, x[i], bx[PEER_TOPK - 1 - i]))
        for d in (8, 4, 2, 1):
            for i in range(PEER_TOPK):
                if not i & d:
                    _cmpx(c, cx, i, i + d)
        v, x = c, cx
    return (v, x) if payload is not None else v


def _colsum(x):
    x = x + pltpu.roll(x, 4, 0)
    x = x + pltpu.roll(x, 2, 0)
    return x + pltpu.roll(x, 1, 0)


def _picks_by_value(s1, s2):
    k = PEER_TOPK
    ng = N_KEYS // SUBLANES
    sub = lax.broadcasted_iota(I32, (SUBLANES, LANES), 0)
    subf = sub.astype(F32)
    one = lambda m: jnp.where(m, 1.0, 0.0)
    c1 = [s1[SUBLANES * v:SUBLANES * (v + 1)] for v in range(ng)]
    c2 = [s2[SUBLANES * v:SUBLANES * (v + 1)] for v in range(ng)]
    ids = [subf + float(SUBLANES * v) for v in range(ng)]
    t1, i1 = _top16_sorted(c1, ids)
    t2, i2 = _top16_sorted(c2, ids)

    def as_rows(t, base):
        out = t[base]
        for r in range(1, SUBLANES):
            out = jnp.where(sub == r, t[base + r], out)
        return out

    sc2_lo, sc2_hi, sc1_hi = as_rows(t2, 0), as_rows(t2, SUBLANES), as_rows(t1, SUBLANES)
    cand = [t1[0] + sc2_lo, t1[0] + sc2_hi] + [t1[r] + sc2_lo for r in range(1, SUBLANES)] + [sc1_hi + t2[0]]
    tau = _top16_sorted(cand)[k - 1]
    sel = [c >= tau for c in cand]
    n_r = [_colsum(one(sel[0]) + one(sel[1]))] + [_colsum(one(sel[1 + r])) for r in range(1, SUBLANES)]
    n_hi = one(sel[9])
    n_r += [_colsum(jnp.where(sub == q, n_hi, 0.0)) for q in range(SUBLANES)]
    cmax = t1[0] + t2[0]
    zsum = jnp.where(sel[0], jnp.exp(cand[0] - cmax), 0.0)
    for g in range(1, len(cand)):
        zsum = zsum + jnp.where(sel[g], jnp.exp(cand[g] - cmax), 0.0)
    scale = 0.5 / _colsum(zsum)
    pre = [jnp.zeros((SUBLANES, LANES), F32)]
    for r in range(k):
        pre.append(pre[r] + n_r[r])
    bad = pre[k] != float(k)
    for r in range(k - 1):
        bad = bad | (t1[r] == t1[r + 1]) | (t2[r] == t2[r + 1])
    n_in1, n_in2 = one(c1[0] >= t1[k - 1]), one(c2[0] >= t2[k - 1])
    for v in range(1, ng):
        n_in1, n_in2 = n_in1 + one(c1[v] >= t1[k - 1]), n_in2 + one(c2[v] >= t2[k - 1])
    bad = bad | (_colsum(n_in1) != float(k)) | (_colsum(n_in2) != float(k))
    outs = ([], [], [])
    for half in range(k // SUBLANES):
        pick = subf + float(SUBLANES * half)
        r1p = jnp.zeros((SUBLANES, LANES), F32)
        base = jnp.zeros((SUBLANES, LANES), F32)
        for r in range(1, k):
            started = pre[r] <= pick
            r1p = jnp.where(started, float(r), r1p)
            base = jnp.where(started, pre[r], base)
        r2p = pick - base
        a_id, b_id, v1, v2 = i1[0], i2[0], t1[0], t2[0]
        for r in range(1, k):
            is1, is2 = r1p == float(r), r2p == float(r)
            a_id, v1 = jnp.where(is1, i1[r], a_id), jnp.where(is1, t1[r], v1)
            b_id, v2 = jnp.where(is2, i2[r], b_id), jnp.where(is2, t2[r], v2)
        outs[0].append(a_id)
        outs[1].append(b_id)
        outs[2].append(jnp.exp((v1 + v2) - cmax) * scale)
    return tuple(jnp.concatenate(o, axis=0) for o in outs) + (one(bad),)


def _picks_by_index(s1, s2):
    k = PEER_TOPK
    r16 = lax.broadcasted_iota(I32, (k, LANES), 0)
    rowf = lax.broadcasted_iota(I32, (N_KEYS, LANES), 0).astype(F32)
    rank1, sc1 = _top16_rows(s1)
    rank2, sc2 = _top16_rows(s2)

    def index_of_rank(rank):
        out = jnp.zeros((k, LANES), F32)
        for r in range(k):
            out = jnp.where(r16 == r, jnp.sum(jnp.where(rank == r, rowf, 0.0), axis=0, keepdims=True), out)
        return out

    i1, i2 = index_of_rank(rank1), index_of_rank(rank2)
    rep = lambda row, n: jnp.broadcast_to(row, (n, LANES))
    cand = jnp.concatenate([sc1[0:1] + sc2] + [sc1[r:r + 1] + sc2[0:8] for r in range(1, 8)] + [sc1[8:16] + sc2[0:1]],
                           axis=0)
    a_c = jnp.concatenate([rep(i1[0:1], k)] + [rep(i1[r:r + 1], 8) for r in range(1, 8)] + [i1[8:16]], axis=0)
    b_c = jnp.concatenate([i2] + [i2[0:8]] * 7 + [rep(i2[0:1], 8)], axis=0)
    rankc, _ = _top16_rows(cand)
    z = jnp.sum(jnp.where(rankc < k, jnp.exp(cand - cand[0:1]), 0.0), axis=0, keepdims=True)
    a_id = jnp.zeros((k, LANES), F32)
    b_id = jnp.zeros((k, LANES), F32)
    val = jnp.zeros((k, LANES), F32)
    for p in range(k):
        m = rankc == p
        pick = lambda c: jnp.sum(jnp.where(m, c, 0.0), axis=0, keepdims=True)
        a_id = jnp.where(r16 == p, pick(a_c), a_id)
        b_id = jnp.where(r16 == p, pick(b_c), b_id)
        val = jnp.where(r16 == p, pick(cand), val)
    return a_id, b_id, jnp.exp(val - cand[0:1]) * (0.5 / z)


def _peer_gates_body(strips, s1_ref, s2_ref, a_ref, b_ref, g_ref, at_ref, bt_ref, gt_ref):
    k = PEER_TOPK
    for s in range(strips):
        ls = slice(s * LANES, (s + 1) * LANES)

        def head(h, _):
            s1 = s1_ref[h, :, ls]
            s2 = s2_ref[h, :, ls]
            rows = pl.ds(pl.multiple_of(h * k, k), k)
            a_id, b_id, gate, bad = _picks_by_value(s1, s2)
            at_ref[rows, :] = a_id
            bt_ref[rows, :] = b_id
            gt_ref[rows, :] = gate

            @pl.when(jnp.max(bad) > 0.0)
            def _():
                at_ref[rows, :], bt_ref[rows, :], gt_ref[rows, :] = _picks_by_index(s1, s2)

            return 0

        lax.fori_loop(0, PEER_HEADS, head, 0)
        a_ref[ls, :] = at_ref[...].T
        b_ref[ls, :] = bt_ref[...].T
        g_ref[ls, :] = gt_ref[...].T


def _peer_gates(s1t, s2t):
    t = s1t.shape[2]
    nslot = PEER_HEADS * PEER_TOPK
    strips = 2 if t % (2 * LANES) == 0 else 1
    tb = strips * LANES
    blk = pl.BlockSpec((PEER_HEADS, N_KEYS, tb), lambda i: (0, 0, i))
    oblk = pl.BlockSpec((tb, nslot), lambda i: (i, 0))
    shp = jax.ShapeDtypeStruct((t, nslot), F32)
    return pl.pallas_call(
        functools.partial(_peer_gates_body, strips),
        grid=(t // tb,),
        in_specs=[blk, blk],
        out_specs=[oblk, oblk, oblk],
        out_shape=[shp, shp, shp],
        scratch_shapes=[pltpu.VMEM((nslot, LANES), F32)] * 3,
        compiler_params=_cparams("parallel"),
        name="peer_gates",
    )(s1t, s2t)


_SQRT_HALF = 0.7071067811865476
_GPAD = 8


def _peer_experts_body(tm, eb, nj, final, hn_ref, u_ref, v_ref, a_ref, b_ref, g_ref, x_ref, nf_ref,
                       o_ref, acc_ref, gw_ref):
    j = pl.program_id(1)
    pitch = tm + _GPAD
    half_a = N_KEYS // 2
    tiles = eb // N_KEYS
    steps_per_half = half_a // tiles

    @pl.when(j == 0)
    def _():
        acc_ref[...] = jnp.zeros(acc_ref.shape, F32)

    @pl.when(lax.rem(j, steps_per_half) == 0)
    def _():
        key = (lax.broadcasted_iota(I32, (half_a, PEER_HEADS * PEER_TOPK), 0) + j * tiles).astype(F32)
        key2 = lax.broadcasted_iota(I32, (N_KEYS, PEER_HEADS * PEER_TOPK), 0).astype(F32)

        def group(i, _):
            rows = pl.ds(pl.multiple_of(i * SUBLANES, SUBLANES), SUBLANES)
            a8, b8, g8 = a_ref[rows, :], b_ref[rows, :], g_ref[rows, :]
            for q in range(SUBLANES):
                pt = jnp.where(key == a8[q:q + 1], g8[q:q + 1], 0.0).astype(BF16)
                qt = jnp.where(key2 == b8[q:q + 1], 1.0, 0.0).astype(BF16)
                gw_ref[pl.ds(i * SUBLANES + q, half_a, stride=pitch), :] = _dot_nt(pt, qt)
            return 0

        lax.fori_loop(0, tm // SUBLANES, group, 0, unroll=8)

    a_lo = lax.rem(j * tiles, half_a)
    act = _dot_nt(hn_ref[...], u_ref[...])
    ws = []
    for al in range(tiles):
        gate = gw_ref[pl.ds(pl.multiple_of((a_lo + al) * pitch, SUBLANES), tm), :]
        x = act[:, al * N_KEYS:(al + 1) * N_KEYS]
        ws.append((gate * (x * (1.0 + lax.erf(x * _SQRT_HALF)))).astype(BF16))
    acc_ref[...] += _dot(jnp.concatenate(ws, axis=1), v_ref[...])

    @pl.when(j == nj - 1)
    def _():
        x2 = x_ref[...] + acc_ref[...]
        if final:
            o_ref[...] = _rms(x2, nf_ref[...])
        else:
            o_ref[...] = x2


def _peer_experts(hn, picks, x1, lw, norm_final, final):
    t = x1.shape[0]
    tm = _row_block(t, 512)
    eb = 16 * N_KEYS
    nj = N_EXPERTS // eb
    nslot = PEER_HEADS * PEER_TOPK
    a_id, b_id, gate = picks
    row = lambda n: pl.BlockSpec((tm, n), lambda i, j: (i, 0))
    return pl.pallas_call(
        functools.partial(_peer_experts_body, tm, eb, nj, final),
        grid=(t // tm, nj),
        in_specs=[row(D_MODEL),
                  pl.BlockSpec((eb, D_MODEL), lambda i, j: (j, 0)),
                  pl.BlockSpec((eb, D_MODEL), lambda i, j: (j, 0)),
                  row(nslot), row(nslot), row(nslot), row(D_MODEL),
                  pl.BlockSpec((1, D_MODEL), lambda i, j: (0, 0))],
        out_specs=row(D_MODEL),
        out_shape=jax.ShapeDtypeStruct((t, D_MODEL), F32),
        scratch_shapes=[pltpu.VMEM((tm, D_MODEL), F32),
                        pltpu.VMEM((N_KEYS // 2 * (tm + _GPAD), N_KEYS), F32)],
        compiler_params=_cparams("parallel", "arbitrary"),
        name="peer_experts",
    )(hn, lw["peer_u"], lw["peer_v"], a_id, b_id, gate, x1, norm_final)


def _peer(x1, lw, norm_final, final):
    hn, s1t, s2t = _peer_scores(x1, lw)
    picks = _peer_gates(s1t, s2t)
    return _peer_experts(hn, picks, x1, lw, norm_final, final)


def _pad_lanes(v, start, total=LANES):
    return jnp.zeros((1, total), F32).at[0, start:start + v.shape[0]].set(v)


def _layer_weights(l, norm_mix, w_in, b_gate, conv_w, conv_b, dt_bias, a_log, d_skip, ssm_norm, w_ssm_proj,
                   q_norm, w_qb, kv_norm, w_kvb, w_mla_proj, w_out, norm_ffn, peer_wq, peer_keys, peer_u, peer_v):
    o_dt = 3 * D_MODEL + CONV_DIM
    o_cq = o_dt + SSM_HEADS
    o_ckv = o_cq + Q_LORA
    o_kr = o_ckv + KV_LORA
    w = w_in[l]
    half = QK_ROPE // 2
    zcols = lambda n: jnp.zeros((D_MODEL, n), F32)
    kr = w[:, o_kr:o_kr + QK_ROPE]
    kr_sw = jnp.concatenate([kr[:, half:], kr[:, :half]], axis=1)
    sm1 = jnp.concatenate([kr, zcols(LANES - QK_ROPE)], axis=1)
    sm2 = jnp.concatenate([kr_sw, zcols(DT_LANE - QK_ROPE), w[:, o_dt:o_cq], zcols(LANES - DT_LANE - SSM_HEADS)], axis=1)
    wcat = jnp.concatenate([w[:, :o_dt], w[:, o_cq:o_kr], sm1, sm2], axis=1).astype(BF16)

    wq = w_qb[l].reshape(Q_LORA, MLA_HEADS, QK_NOPE + QK_ROPE)
    wq_nope = wq[:, :, :QK_NOPE].reshape(Q_LORA, MLA_HEADS * QK_NOPE)
    pe = wq[:, :, QK_NOPE:]
    pe_sw = jnp.concatenate([pe[:, :, half:], pe[:, :, :half]], axis=2)
    padpe = lambda p: jnp.pad(p, ((0, 0), (0, 0), (0, LANES - QK_ROPE))).reshape(Q_LORA, MLA_HEADS * LANES)
    wkv = w_kvb[l].reshape(KV_LORA, MLA_HEADS, QK_NOPE + V_HEAD)
    w_kb_t = jnp.transpose(wkv[:, :, :QK_NOPE], (1, 2, 0))
    w_vb = jnp.transpose(wkv[:, :, QK_NOPE:], (1, 0, 2))
    w_vb_all = wkv[:, :, QK_NOPE:].reshape(KV_LORA, MLA_HEADS * V_HEAD)
    row = lambda v: v.reshape(1, -1)
    return dict(
        norm_mix=row(norm_mix[l]), wcat=wcat, b_gate=row(b_gate[l]),
        conv_w=conv_w[l], conv_b=row(conv_b[l]),
        dt_bias_p=_pad_lanes(dt_bias[l], DT_LANE), a_log_p=_pad_lanes(a_log[l], DT_LANE),
        d_skip_e=row(jnp.repeat(d_skip[l], SSM_HEAD_DIM)), d_skip_row=row(d_skip[l]),
        ssm_norm=row(ssm_norm[l]), ssm_norm_t=ssm_norm[l].reshape(SSM_HEADS, SSM_HEAD_DIM).T,
        w_ssm_proj=w_ssm_proj[l].astype(BF16),
        q_norm=row(q_norm[l]), kv_norm=row(kv_norm[l]),
        wq_nope=wq_nope.astype(BF16), wq_pe=padpe(pe).astype(BF16), wq_pe_sw=padpe(pe_sw).astype(BF16),
        w_kb_t=w_kb_t.astype(BF16), w_vb=w_vb.astype(BF16), w_vb_all=w_vb_all.astype(BF16),
        w_mla_proj=w_mla_proj[l].astype(BF16), w_out=w_out[l].astype(BF16),
        norm_ffn=row(norm_ffn[l]), peer_wq=peer_wq[l].astype(BF16), peer_keys=peer_keys[l].astype(BF16),
        peer_u=peer_u[l].astype(BF16), peer_v=peer_v[l].astype(BF16))


def _rope_tables(pos):
    half = QK_ROPE // 2
    inv = ROPE_THETA ** (-jnp.arange(half, dtype=F32) / half)
    ang = pos.astype(F32)[:, None] * inv[None, :]
    cos, sin = jnp.cos(ang), jnp.sin(ang)
    pad = jnp.zeros((pos.shape[0], LANES - QK_ROPE), F32)
    c128 = jnp.concatenate([cos, cos, pad], axis=1)
    s128 = jnp.concatenate([-sin, sin, pad], axis=1)
    ktab = jnp.concatenate([c128, s128], axis=1)
    qtab = jnp.concatenate([jnp.tile(c128, (1, MLA_HEADS)), jnp.tile(s128, (1, MLA_HEADS))], axis=1)
    return qtab, ktab


def _prompt_layer(x2d, nb, seq, lw, tabs, norm_final, final):
    gates, z, xbc, cq, ckv, sm1, sm2 = _in_proj(x2d, lw["norm_mix"], lw["wcat"])
    yn, h_new = _ssd_prompt(xbc, sm2, z, lw, nb, seq)
    qcat, kcat, lat, kpe = _mla_prep(cq, ckv, sm1, sm2, tabs[0], tabs[1], lw, nb, seq)
    o = _attn_prompt(qcat, kcat, lw["w_vb"], nb, seq)
    x1 = _mix(x2d, gates, yn, o, lw)
    x2 = _peer(x1, lw, norm_final, final)
    conv_new = xbc.reshape(nb, seq, CONV_DIM)[:, seq - (CONV_WIDTH - 1):, :]
    return x2, h_new, conv_new, lat.reshape(nb, seq, KV_LORA), kpe.reshape(nb, seq, QK_ROPE)


def _sample_layer(x2d, layer, lw, tabs, state_ssm, state_conv, cache_latent, cache_krope, page_table,
                  norm_final, final):
    nb = x2d.shape[0]
    gates, z, xbc, cq, ckv, sm1, sm2 = _in_proj(x2d, lw["norm_mix"], lw["wcat"])
    prev = state_conv[layer]
    u, dt = _ssd_step_pre(xbc, jnp.transpose(prev, (1, 0, 2)), sm2, lw)
    to_t = lambda v: jnp.transpose(v.reshape(nb, SSM_HEADS, SSM_HEAD_DIM), (0, 2, 1))
    n_bc = SSM_GROUPS * D_STATE
    h_new, yt = _ssd_step(state_ssm, layer, to_t(u[:, :D_INNER]), to_t(z),
                          u[:, D_INNER:D_INNER + n_bc].reshape(nb, 1, n_bc),
                          u[:, D_INNER + n_bc:].reshape(nb, 1, n_bc), dt.reshape(nb, 1, LANES), lw)
    yn = jnp.transpose(yt, (0, 2, 1)).reshape(nb, D_INNER).astype(BF16)
    conv_new = jnp.concatenate([prev[:, 1:], xbc[:, None, :]], axis=1)

    qcat, kcat, lat, kpe = _mla_prep(cq, ckv, sm1, sm2, tabs[0], tabs[1], lw, 1, nb)
    qdec = jnp.transpose(qcat[0], (1, 0, 2))
    o = _attn_decode(page_table, qdec, kcat.reshape(nb, 1, QK_CAT), lw["w_vb_all"], cache_latent, cache_krope, layer)
    x1 = _mix(x2d, gates, yn, o.reshape(nb, MLA_HEADS * V_HEAD), lw)
    pad = (-nb) % LANES
    x2 = _peer(jnp.pad(x1, ((0, pad), (0, 0))), lw, norm_final, final)[:nb]
    return x2, h_new, conv_new, lat.reshape(nb, 1, KV_LORA), kpe.reshape(nb, 1, QK_ROPE)


def kernel(x_prompt, x_sample, cache_latent, cache_krope, state_ssm, state_conv, page_table, norm_mix, w_in, b_gate,
           conv_w, conv_b, dt_bias, a_log, d_skip, ssm_norm, w_ssm_proj, q_norm, w_qb, kv_norm, w_kvb, w_mla_proj,
           w_out, norm_ffn, peer_wq, peer_keys, peer_u, peer_v, norm_final):
    bp, sp = x_prompt.shape[:2]
    bs, ss = x_sample.shape[:2]
    depth = w_in.shape[0]
    assert ss == 1 and sp % SSD_CHUNK == 0
    past_len = page_table.shape[1] * cache_latent.shape[2]
    tabs_p = _rope_tables(jnp.arange(sp, dtype=I32))
    tabs_s = _rope_tables(jnp.full((bs,), past_len, I32))
    nf = norm_final.reshape(1, D_MODEL)
    cache_krope = jnp.swapaxes(cache_krope, 2, 3).astype(BF16)
    xp = x_prompt.reshape(bp * sp, D_MODEL)
    xs = x_sample.reshape(bs, D_MODEL)
    outs_p, outs_s = [], []
    for l in range(depth):
        lw = _layer_weights(l, norm_mix, w_in, b_gate, conv_w, conv_b, dt_bias, a_log, d_skip, ssm_norm, w_ssm_proj,
                            q_norm, w_qb, kv_norm, w_kvb, w_mla_proj, w_out, norm_ffn, peer_wq, peer_keys, peer_u,
                            peer_v)
        final = l == depth - 1
        xp, hp, cp, lp, kp = _prompt_layer(xp, bp, sp, lw, tabs_p, nf, final)
        xs, hs, cs, lsn, ksn = _sample_layer(xs, l, lw, tabs_s, state_ssm, state_conv, cache_latent, cache_krope,
                                             page_table, nf, final)
        outs_p.append((lp, kp, hp, cp))
        outs_s.append((lsn, ksn, hs, cs))
    stack = lambda outs, i: jnp.stack([o[i] for o in outs])
    return (xp.reshape(bp, sp, D_MODEL), xs.reshape(bs, ss, D_MODEL),
            stack(outs_p, 0), stack(outs_p, 1), stack(outs_p, 2), stack(outs_p, 3),
            stack(outs_s, 0), stack(outs_s, 1), stack(outs_s, 2), stack(outs_s, 3))
```
